```python
import jax, jax.numpy as jnp
from jax import lax
import numpy as np

D_MODEL = 1024
BATCH = 8
SEQ = 4096
DEPTH = 2
DEC_BATCH = 32
DEC_SEQ = 64
PAST_LEN = 1024

CHUNK = 64
N_A = DEPTH // 2
N_B = DEPTH - N_A
MLP_CHUNK = 128
GMLP_GROUPS = 8
D_GATE = D_MODEL
GMLP_GC = D_GATE // GMLP_GROUPS
N_HEADS = 16
N_KV_HEADS = 4
HEAD_DIM = D_MODEL // N_HEADS
GQA_GROUP = N_HEADS // N_KV_HEADS
WINDOW = 128
WIN_CHUNKS = WINDOW // CHUNK
PEER_HEADS = 8
N_KEYS = 128
N_EXPERTS = N_KEYS * N_KEYS
PEER_TOPK = 16
D_KEY = 256
D_HALF = D_KEY // 2
PEER_BLOCK = 512
EPS = 1e-6
NEG = -1e30

kernel_name = "yoco_gmlp_swa_sink_alibi_peer_stream_step"


def rms_norm(x, g):
    xf = x.astype(jnp.float32)
    y = xf * lax.rsqrt(jnp.mean(xf * xf, axis=-1, keepdims=True) + EPS)
    return (y * g.astype(jnp.float32)).astype(x.dtype)


def alibi_slopes(n):
    return jnp.array([2.0 ** (-8.0 * (h + 1) / n) for h in range(n)], dtype=jnp.float32)


def gmlp_mix(xn, w_in, g_v, w_s, b_s, w_out):
    B, L, _ = xn.shape
    hmid = jax.nn.gelu(xn @ w_in)
    u, v = jnp.split(hmid, 2, axis=-1)
    v = rms_norm(v, g_v)
    lc = min(L, MLP_CHUNK)
    nc = L // lc
    pos = jnp.arange(lc)
    mask = (pos[None, :] // CHUNK) <= (pos[:, None] // CHUNK)
    ws = jnp.where(mask[None], w_s[:, :lc, :lc], jnp.zeros((), w_s.dtype))
    vb = v.reshape(B, nc, lc, GMLP_GROUPS, GMLP_GC)
    mixed = jnp.einsum('gij,bnjgc->bnigc', ws, vb) + b_s[:, :lc].T[None, None, :, :, None]
    out = u * mixed.reshape(B, L, D_GATE)
    return out @ w_out, v


def banded_attention(q, k, v, q_pos, k_pos, k_valid, slopes, sink):
    B, NB, Lq = q.shape[:3]
    qg = q.reshape(B, NB, Lq, N_KV_HEADS, GQA_GROUP, HEAD_DIM)
    s = jnp.einsum('bnqkgd,bnskd->bnkgqs', qg, k).astype(jnp.float32) * (HEAD_DIM ** -0.5)
    dist = jnp.abs(q_pos[:, :, None] - k_pos[:, None, :]).astype(jnp.float32)
    s = s - slopes.reshape(N_KV_HEADS, GQA_GROUP)[None, None, :, :, None, None] * dist[None, :, None, None]
    s = jnp.where(k_valid[None, :, None, None, None, :], s, NEG)
    sink_col = jnp.broadcast_to(
        sink.astype(jnp.float32).reshape(N_KV_HEADS, GQA_GROUP)[None, None, :, :, None, None],
        s.shape[:-1] + (1,))
    p = jax.nn.softmax(jnp.concatenate([s, sink_col], axis=-1), axis=-1)[..., :-1]
    o = jnp.einsum('bnkgqs,bnskd->bnqkgd', p.astype(v.dtype), v)
    return o.reshape(B, NB, Lq, N_HEADS * HEAD_DIM)


def prompt_window_attention(q, k, v, slopes, sink):
    B, L = q.shape[:2]
    nb = L // CHUNK
    qb = q.reshape(B, nb, CHUNK, N_HEADS, HEAD_DIM)

    def band(t):
        tp = jnp.pad(t, ((0, 0), (WINDOW, 0), (0, 0), (0, 0)))
        tp = tp.reshape(B, nb + WIN_CHUNKS, CHUNK, N_KV_HEADS, HEAD_DIM)
        return jnp.concatenate([tp[:, i:i + nb] for i in range(WIN_CHUNKS + 1)], axis=2)

    c0 = jnp.arange(nb)[:, None] * CHUNK
    q_pos = c0 + jnp.arange(CHUNK)[None]
    k_pos = c0 - WINDOW + jnp.arange(WINDOW + CHUNK)[None]
    o = banded_attention(qb, band(k), band(v), q_pos, k_pos, k_pos >= 0, slopes, sink)
    return o.reshape(B, L, N_HEADS * HEAD_DIM)


def cached_window_attention(q, k_new, v_new, ck, cv, slopes, sink):
    B, L = q.shape[:2]
    wc = ck.shape[1]
    kc = jnp.concatenate([ck, k_new], axis=1)
    vc = jnp.concatenate([cv, v_new], axis=1)
    q_pos = (PAST_LEN + jnp.arange(L))[None]
    k_pos = (PAST_LEN - wc + jnp.arange(wc + L))[None]
    valid = jnp.ones(k_pos.shape, dtype=bool)
    o = banded_attention(q[:, None], kc[:, None], vc[:, None], q_pos, k_pos, valid, slopes, sink)
    return o.reshape(B, L, N_HEADS * HEAD_DIM)


def peer_ffn(xn, w_q, sub_keys, exp_u, exp_v):
    T = xn.shape[0]
    blk = min(PEER_BLOCK, T)
    nblk = -(-T // blk)
    xp = jnp.pad(xn, ((0, nblk * blk - T), (0, 0))).reshape(nblk, blk, D_MODEL)

    def one_block(xb):
        q = (xb @ w_q).reshape(blk, PEER_HEADS, 2, D_HALF)
        sc = jnp.einsum('thpd,hpnd->thpn', q, sub_keys).astype(jnp.float32)
        top_s, top_i = lax.top_k(sc, PEER_TOPK)
        cand_s = top_s[:, :, 0, :, None] + top_s[:, :, 1, None, :]
        cand_i = top_i[:, :, 0, :, None] * N_KEYS + top_i[:, :, 1, None, :]
        best_s, sel = lax.top_k(cand_s.reshape(blk, PEER_HEADS, PEER_TOPK * PEER_TOPK), PEER_TOPK)
        idx = jnp.take_along_axis(cand_i.reshape(blk, PEER_HEADS, PEER_TOPK * PEER_TOPK), sel, axis=-1)
        gate = jax.nn.softmax(best_s, axis=-1)
        act = jax.nn.gelu(jnp.einsum('thkd,td->thk', exp_u[idx], xb))
        w = (gate * act.astype(jnp.float32)).astype(xb.dtype)
        return jnp.einsum('thk,thkd->td', w, exp_v[idx])

    return lax.map(one_block, xp).reshape(nblk * blk, D_MODEL)[:T]


def setup_inputs(seed: int = 0) -> dict:
    key = jax.random.key(seed)
    ks = jax.random.split(key, 24)
    f32 = jnp.float32
    nrm = lambda k, shape, s: jax.random.normal(k, shape, f32) * s
    w_cache = min(WINDOW, PAST_LEN)
    return {
        'x_prompt': nrm(ks[0], (BATCH, SEQ, D_MODEL), 1.0),
        'x_sample': nrm(ks[1], (DEC_BATCH, DEC_SEQ, D_MODEL), 1.0),
        'cache_k_win': nrm(ks[2], (DEC_BATCH, w_cache, N_KV_HEADS, HEAD_DIM), 1.0),
        'cache_v_win': nrm(ks[3], (DEC_BATCH, w_cache, N_KV_HEADS, HEAD_DIM), 1.0),
        'norm_mix': 1.0 + nrm(ks[4], (DEPTH, D_MODEL), 0.05),
        'norm_ffn': 1.0 + nrm(ks[5], (DEPTH, D_MODEL), 0.05),
        'gmlp_w_in': nrm(ks[6], (N_A, D_MODEL, 2 * D_GATE), D_MODEL ** -0.5),
        'gmlp_norm_v': 1.0 + nrm(ks[7], (N_A, D_GATE), 0.05),
        'gmlp_w_s': nrm(ks[8], (N_A, GMLP_GROUPS, MLP_CHUNK, MLP_CHUNK), MLP_CHUNK ** -0.5),
        'gmlp_b_s': 1.0 + nrm(ks[9], (N_A, GMLP_GROUPS, MLP_CHUNK), 0.1),
        'gmlp_w_out': nrm(ks[10], (N_A, D_GATE, D_MODEL), D_GATE ** -0.5),
        'norm_kv': 1.0 + nrm(ks[11], (D_MODEL,), 0.05),
        'w_kv': nrm(ks[12], (D_MODEL, 2 * N_KV_HEADS * HEAD_DIM), D_MODEL ** -0.5),
        'attn_w_q': nrm(ks[13], (N_B, D_MODEL, N_HEADS * HEAD_DIM), D_MODEL ** -0.5),
        'attn_sinks': nrm(ks[14], (N_B, N_HEADS), 0.5),
        'attn_w_o': nrm(ks[15], (N_B, N_HEADS * HEAD_DIM, D_MODEL), (N_HEADS * HEAD_DIM) ** -0.5),
        'peer_w_q': nrm(ks[16], (DEPTH, D_MODEL, PEER_HEADS * D_KEY), D_MODEL ** -0.5),
        'peer_sub_keys': nrm(ks[17], (DEPTH, PEER_HEADS, 2, N_KEYS, D_HALF), D_HALF ** -0.5),
        'peer_u': nrm(ks[18], (DEPTH, N_EXPERTS, D_MODEL), D_MODEL ** -0.5),
        'peer_v': nrm(ks[19], (DEPTH, N_EXPERTS, D_MODEL), 0.25),
        'norm_final': 1.0 + nrm(ks[20], (D_MODEL,), 0.05),
    }


def reference(x_prompt, x_sample, cache_k_win, cache_v_win, norm_mix, norm_ffn,
              gmlp_w_in, gmlp_norm_v, gmlp_w_s, gmlp_b_s, gmlp_w_out, norm_kv, w_kv,
              attn_w_q, attn_sinks, attn_w_o, peer_w_q, peer_sub_keys, peer_u, peer_v,
              norm_final):
    slopes = alibi_slopes(N_HEADS)

    def trunk(x, ck, cv):
        prompt = ck is None
        B, L, _ = x.shape
        h = x
        k_sh = None
        v_sh = None
        gmlp_rows = []
        for l in range(DEPTH):
            hn = rms_norm(h, norm_mix[l])
            if l < N_A:
                y, v_rows = gmlp_mix(hn, gmlp_w_in[l], gmlp_norm_v[l], gmlp_w_s[l],
                                     gmlp_b_s[l], gmlp_w_out[l])
                gmlp_rows.append(v_rows)
            else:
                j = l - N_A
                q = (hn @ attn_w_q[j]).reshape(B, L, N_HEADS, HEAD_DIM)
                if prompt:
                    o = prompt_window_attention(q, k_sh, v_sh, slopes, attn_sinks[j])
                else:
                    o = cached_window_attention(q, k_sh, v_sh, ck, cv, slopes, attn_sinks[j])
                y = o @ attn_w_o[j]
            h = h + y
            hf = rms_norm(h, norm_ffn[l]).reshape(B * L, D_MODEL)
            h = h + peer_ffn(hf, peer_w_q[l], peer_sub_keys[l], peer_u[l], peer_v[l]).reshape(B, L, D_MODEL)
            if l == N_A - 1:
                kv = (rms_norm(h, norm_kv) @ w_kv).reshape(B, L, 2, N_KV_HEADS, HEAD_DIM)
                k_sh = kv[:, :, 0]
                v_sh = kv[:, :, 1]
        return rms_norm(h, norm_final), k_sh, v_sh, gmlp_rows

    y_prompt, k_p, v_p, _ = trunk(x_prompt, None, None)
    prompt_k_win = k_p[:, -WINDOW:]
    prompt_v_win = v_p[:, -WINDOW:]

    y_sample, k_s, v_s, rows_s = trunk(x_sample, cache_k_win, cache_v_win)
    wc = cache_k_win.shape[1]
    sample_k_win = jnp.concatenate([cache_k_win, k_s], axis=1)[:, -wc:]
    sample_v_win = jnp.concatenate([cache_v_win, v_s], axis=1)[:, -wc:]
    sample_gmlp_v = jnp.stack(rows_s, axis=0)

    return (y_prompt, y_sample, prompt_k_win, prompt_v_win, sample_k_win, sample_v_win, sample_gmlp_v)
```

```python
import functools
import math

import jax
import jax.numpy as jnp
from jax import lax
from jax.experimental import pallas as pl
from jax.experimental.pallas import tpu as pltpu

F32 = jnp.float32
BF16 = jnp.bfloat16

D_MODEL = 1024
CHUNK = 64
MLP_CHUNK = 128
GMLP_GROUPS = 8
D_GATE = D_MODEL
N_HEADS = 16
N_KV_HEADS = 4
HEAD_DIM = 64
GQA_GROUP = N_HEADS // N_KV_HEADS
WINDOW = 128
PEER_HEADS = 8
N_KEYS = 128
N_EXPERTS = N_KEYS * N_KEYS
PEER_TOPK = 16
D_HALF = 128
N_PICKS = PEER_HEADS * PEER_TOPK
EPS = 1e-6
NEG = -1e30
KV_DIM = 2 * N_KV_HEADS * HEAD_DIM

LANES = 128
SUBLANES = 8
ROWS_PER_EXPERT = 2 * D_MODEL // LANES
PICK_PITCH = 20
SLOT_ROWS = N_PICKS * PICK_PITCH

VMEM_LIMIT = 48 * 1024 * 1024


def _rms(x, g):
    return x * lax.rsqrt(jnp.mean(x * x, axis=-1, keepdims=True) + EPS) * g


def _cparams(n_axes):
    return pltpu.CompilerParams(
        dimension_semantics=("arbitrary",) * n_axes, vmem_limit_bytes=VMEM_LIMIT)


def _gmlp_kernel(x_ref, gmix_ref, win_ref, gv_ref, wmix_ref, bias_ref, wout_ref,
                 h_ref, v_ref, *, tm):
    x = x_ref[...]
    hn = _rms(x, gmix_ref[...])
    hmid = jax.nn.gelu(jnp.dot(hn.astype(BF16), win_ref[...], preferred_element_type=F32))
    u = hmid[:, :D_GATE]
    v = _rms(hmid[:, D_GATE:], gv_ref[...])
    v_ref[...] = v
    vb = v.astype(BF16)
    bias = bias_ref[...]
    outs = []
    for c in range(tm // MLP_CHUNK):
        r0 = c * MLP_CHUNK
        cols = []
        for g in range(GMLP_GROUPS):
            c0 = g * LANES
            cols.append(jnp.dot(wmix_ref[g], vb[r0:r0 + MLP_CHUNK, c0:c0 + LANES],
                                preferred_element_type=F32))
        mixed = jnp.concatenate(cols, axis=1) + bias
        outs.append(u[r0:r0 + MLP_CHUNK, :] * mixed)
    out = outs[0] if len(outs) == 1 else jnp.concatenate(outs, axis=0)
    y = jnp.dot(out.astype(BF16), wout_ref[...], preferred_element_type=F32)
    h_ref[...] = x + y


def _gmlp(x, gmix, win, gv, wmix, bias, wout, tm):
    t = x.shape[0]
    row = lambda i: (i, 0)
    const2 = lambda i: (0, 0)
    return pl.pallas_call(
        functools.partial(_gmlp_kernel, tm=tm),
        grid=(t // tm,),
        in_specs=[
            pl.BlockSpec((tm, D_MODEL), row),
            pl.BlockSpec((1, D_MODEL), const2),
            pl.BlockSpec((D_MODEL, 2 * D_GATE), const2),
            pl.BlockSpec((1, D_GATE), const2),
            pl.BlockSpec((GMLP_GROUPS, MLP_CHUNK, MLP_CHUNK), lambda i: (0, 0, 0)),
            pl.BlockSpec((MLP_CHUNK, D_GATE), const2),
            pl.BlockSpec((D_GATE, D_MODEL), const2),
        ],
        out_specs=[pl.BlockSpec((tm, D_MODEL), row), pl.BlockSpec((tm, D_GATE), row)],
        out_shape=[jax.ShapeDtypeStruct((t, D_MODEL), F32),
                   jax.ShapeDtypeStruct((t, D_GATE), F32)],
        compiler_params=_cparams(1),
        name="gmlp_mixer",
    )(x, gmix, win, gv, wmix, bias, wout)


def _route_kernel(h_ref, g_ref, wq_ref, sk_ref, idx_ref, gate_ref,
                  sc_s, ts_s, ti_s, best_s, sel_s, e_s, gt_s, *, tm):
    hf = _rms(h_ref[...], g_ref[...])
    q = jnp.dot(hf.astype(BF16), wq_ref[...], preferred_element_type=F32).astype(BF16)
    n_hp = 2 * PEER_HEADS
    for hp in range(n_hp):
        sc_s[hp] = lax.dot_general(sk_ref[hp], q[:, hp * D_HALF:(hp + 1) * D_HALF],
                                   (((1,), (1,)), ((), ())), preferred_element_type=F32)

    key_id = lax.broadcasted_iota(jnp.int32, (N_KEYS, LANES), 0).astype(F32)

    def stage1(hp, carry):
        for lg in range(tm // LANES):
            l0 = lg * LANES
            s = sc_s[hp, :, l0:l0 + LANES]
            for k in range(PEER_TOPK):
                m = jnp.max(s, axis=0, keepdims=True)
                i = jnp.min(jnp.where(s == m, key_id, float(N_KEYS)), axis=0, keepdims=True)
                ts_s[hp, k:k + 1, l0:l0 + LANES] = m
                ti_s[hp, k:k + 1, l0:l0 + LANES] = i
                s = jnp.where(key_id == i, -jnp.inf, s)
        return carry

    lax.fori_loop(0, n_hp, stage1, 0)

    sub_id = lax.broadcasted_iota(jnp.int32, (PEER_TOPK, LANES), 0).astype(F32)
    n_cand = float(PEER_TOPK * PEER_TOPK)

    def stage2(hd, carry):
        for lg in range(tm // LANES):
            l0 = lg * LANES
            s0 = ts_s[2 * hd, :, l0:l0 + LANES]
            s1 = ts_s[2 * hd + 1, :, l0:l0 + LANES]
            i0 = ti_s[2 * hd, :, l0:l0 + LANES]
            i1 = ti_s[2 * hd + 1, :, l0:l0 + LANES]
            blocks = [s0[a:a + 1, :] + s1 for a in range(PEER_TOPK)]
            for k in range(PEER_TOPK):
                mm = blocks[0]
                for a in range(1, PEER_TOPK):
                    mm = jnp.maximum(mm, blocks[a])
                m = jnp.max(mm, axis=0, keepdims=True)
                cc = jnp.where(blocks[0] == m, sub_id, n_cand)
                for a in range(1, PEER_TOPK):
                    cc = jnp.minimum(cc, jnp.where(blocks[a] == m, sub_id + float(a * PEER_TOPK), n_cand))
                c = jnp.min(cc, axis=0, keepdims=True)
                best_s[k:k + 1, :] = m
                sel_s[k:k + 1, :] = c
                blocks = [jnp.where(sub_id == c - float(a * PEER_TOPK), -jnp.inf, blocks[a])
                          for a in range(PEER_TOPK)]
            best = best_s[...]
            sel = sel_s[...]
            sel_a = jnp.floor(sel * (1.0 / PEER_TOPK))
            sel_b = sel - sel_a * PEER_TOPK
            e0 = jnp.zeros_like(sel)
            e1 = jnp.zeros_like(sel)
            for a in range(PEER_TOPK):
                e0 = jnp.where(sel_a == float(a), i0[a:a + 1, :], e0)
                e1 = jnp.where(sel_b == float(a), i1[a:a + 1, :], e1)
            ex = jnp.exp(best - best[0:1, :])
            gate = ex / jnp.sum(ex, axis=0, keepdims=True)
            r0 = pl.multiple_of(hd * PEER_TOPK, PEER_TOPK)
            e_s[pl.ds(r0, PEER_TOPK), l0:l0 + LANES] = e0 * float(N_KEYS) + e1
            gt_s[pl.ds(r0, PEER_TOPK), l0:l0 + LANES] = gate
        return carry

    lax.fori_loop(0, PEER_HEADS, stage2, 0)

    idx_ref[...] = jnp.transpose(e_s[...]).astype(jnp.int32)
    gate_ref[...] = jnp.transpose(gt_s[...])


def _route(h, g, wq, sk, tm):
    t = h.shape[0]
    row = lambda i: (i, 0)
    const2 = lambda i: (0, 0)
    n_hp = 2 * PEER_HEADS
    return pl.pallas_call(
        functools.partial(_route_kernel, tm=tm),
        grid=(t // tm,),
        in_specs=[
            pl.BlockSpec((tm, D_MODEL), row),
            pl.BlockSpec((1, D_MODEL), const2),
            pl.BlockSpec((D_MODEL, n_hp * D_HALF), const2),
            pl.BlockSpec((n_hp, N_KEYS, D_HALF), lambda i: (0, 0, 0)),
        ],
        out_specs=[pl.BlockSpec((tm, N_PICKS), row), pl.BlockSpec((tm, N_PICKS), row)],
        out_shape=[jax.ShapeDtypeStruct((t, N_PICKS), jnp.int32),
                   jax.ShapeDtypeStruct((t, N_PICKS), F32)],
        scratch_shapes=[
            pltpu.VMEM((n_hp, N_KEYS, tm), F32),
            pltpu.VMEM((n_hp, PEER_TOPK, tm), F32),
            pltpu.VMEM((n_hp, PEER_TOPK, tm), F32),
            pltpu.VMEM((PEER_TOPK, LANES), F32),
            pltpu.VMEM((PEER_TOPK, LANES), F32),
            pltpu.VMEM((N_PICKS, tm), F32),
            pltpu.VMEM((N_PICKS, tm), F32),
        ],
        compiler_params=_cparams(1),
        name="peer_route",
    )(h, g, wq, sk)


def _gather_kernel(idx_ref, gate_ref, h_ref, g_ref, gfin_ref, tab_ref, out_ref,
                   hf_s, buf, sem, *, tt, final_norm):
    hf_s[...] = _rms(h_ref[...], g_ref[...])
    n_sub = D_MODEL // LANES

    def pick_copy(e, slot, r):
        dst = buf.at[pl.ds(slot * SLOT_ROWS + r * PICK_PITCH, ROWS_PER_EXPERT), :]
        return pltpu.make_async_copy(tab_ref.at[e], dst, sem.at[slot])

    def issue(j, slot):
        for r in range(N_PICKS):
            pick_copy(idx_ref[j, r], slot, r).start()

    def wait(slot):
        for r in range(N_PICKS):
            pick_copy(0, slot, r).wait()

    eye = (lax.broadcasted_iota(jnp.int32, (N_PICKS, N_PICKS), 0)
           == lax.broadcasted_iota(jnp.int32, (N_PICKS, N_PICKS), 1))
    sub = lax.broadcasted_iota(jnp.int32, (SUBLANES, LANES), 0)

    def compute(k, slot, x8, gate8, tiles):
        base = slot * SLOT_ROWS
        acc = jnp.zeros((N_PICKS, LANES), F32)
        for s in range(n_sub):
            u_s = buf[pl.ds(base + s, N_PICKS, stride=PICK_PITCH), :]
            acc = acc + u_s * x8[s][k:k + 1, :]
        act = jax.nn.gelu(jnp.sum(acc, axis=1, keepdims=True))
        gcol = jnp.sum(jnp.where(eye, gate8[k:k + 1, :], 0.0), axis=1, keepdims=True)
        w = gcol * act
        new_tiles = []
        for s in range(n_sub):
            v_s = buf[pl.ds(base + n_sub + s, N_PICKS, stride=PICK_PITCH), :]
            o_s = jnp.sum(v_s * w, axis=0, keepdims=True)
            new_tiles.append(jnp.where(sub == k, o_s, tiles[s]))
        return new_tiles

    issue(0, 0)

    def body(gi, carry):
        r0 = pl.multiple_of(gi * SUBLANES, SUBLANES)
        x8 = [hf_s[pl.ds(r0, SUBLANES), s * LANES:(s + 1) * LANES] for s in range(n_sub)]
        gate8 = gate_ref[pl.ds(r0, SUBLANES), :]
        tiles = [jnp.zeros((SUBLANES, LANES), F32) for _ in range(n_sub)]
        for k in range(SUBLANES):
            j = r0 + k
            nxt = (k + 1) % 2
            if k + 1 < SUBLANES:
                issue(j + 1, nxt)
            else:
                @pl.when(j + 1 < tt)
                def _():
                    issue(j + 1, nxt)
            wait(k % 2)
            tiles = compute(k, k % 2, x8, gate8, tiles)
        res = [h_ref[pl.ds(r0, SUBLANES), s * LANES:(s + 1) * LANES] + tiles[s] for s in range(n_sub)]
        if final_norm:
            ss = res[0] * res[0]
            for s in range(1, n_sub):
                ss = ss + res[s] * res[s]
            scale = lax.rsqrt(jnp.sum(ss, axis=1, keepdims=True) * (1.0 / D_MODEL) + EPS)
            res = [res[s] * scale * gfin_ref[:, s * LANES:(s + 1) * LANES] for s in range(n_sub)]
        for s in range(n_sub):
            out_ref[pl.ds(r0, SUBLANES), s * LANES:(s + 1) * LANES] = res[s]
        return carry

    lax.fori_loop(0, tt // SUBLANES, body, 0)


def _gather(idx, gate, h, g, gfin, tab, tt, final_norm):
    t = h.shape[0]
    row = lambda i: (i, 0)
    const2 = lambda i: (0, 0)
    return pl.pallas_call(
        functools.partial(_gather_kernel, tt=tt, final_norm=final_norm),
        grid=(t // tt,),
        in_specs=[
            pl.BlockSpec((tt, N_PICKS), row, memory_space=pltpu.SMEM),
            pl.BlockSpec((tt, N_PICKS), row),
            pl.BlockSpec((tt, D_MODEL), row),
            pl.BlockSpec((1, D_MODEL), const2),
            pl.BlockSpec((1, D_MODEL), const2),
            pl.BlockSpec(memory_space=pl.ANY),
        ],
        out_specs=pl.BlockSpec((tt, D_MODEL), row),
        out_shape=jax.ShapeDtypeStruct((t, D_MODEL), F32),
        scratch_shapes=[
            pltpu.VMEM((tt, D_MODEL), F32),
            pltpu.VMEM((2 * SLOT_ROWS, LANES), F32),
            pltpu.SemaphoreType.DMA((2,)),
        ],
        compiler_params=_cparams(1),
        name="peer_experts",
    )(idx, gate, h, g, gfin, tab)


def _proj_kernel(h_ref, gkv_ref, gq_ref, wkv_ref, wq_ref, kv_ref, q_ref):
    h = h_ref[...]
    n = h * lax.rsqrt(jnp.mean(h * h, axis=-1, keepdims=True) + EPS)
    kv_ref[...] = jnp.dot((n * gkv_ref[...]).astype(BF16), wkv_ref[...], preferred_element_type=F32)
    q_ref[...] = jnp.dot((n * gq_ref[...]).astype(BF16), wq_ref[...],
                         preferred_element_type=F32).astype(BF16)


def _proj(h, gkv, gq, wkv, wq, tm):
    t = h.shape[0]
    row = lambda i: (i, 0)
    const2 = lambda i: (0, 0)
    return pl.pallas_call(
        _proj_kernel,
        grid=(t // tm,),
        in_specs=[
            pl.BlockSpec((tm, D_MODEL), row),
            pl.BlockSpec((1, D_MODEL), const2),
            pl.BlockSpec((1, D_MODEL), const2),
            pl.BlockSpec((D_MODEL, KV_DIM), const2),
            pl.BlockSpec((D_MODEL, D_MODEL), const2),
        ],
        out_specs=[pl.BlockSpec((tm, KV_DIM), row), pl.BlockSpec((tm, D_MODEL), row)],
        out_shape=[jax.ShapeDtypeStruct((t, KV_DIM), F32),
                   jax.ShapeDtypeStruct((t, D_MODEL), BF16)],
        compiler_params=_cparams(1),
        name="kv_q_proj",
    )(h, gkv, gq, wkv, wq)


def _alibi_slopes():
    return [2.0 ** (-8.0 * (h + 1) / N_HEADS) for h in range(N_HEADS)]


def _attn_chunk(q, win, first_valid, sink_ref, wo_ref):
    n_keys = WINDOW + CHUNK
    qi = lax.broadcasted_iota(jnp.int32, (CHUNK, n_keys), 0)
    kj = lax.broadcasted_iota(jnp.int32, (CHUNK, n_keys), 1)
    dist = jnp.abs(WINDOW + qi - kj).astype(F32)
    valid = kj >= first_valid
    slopes = _alibi_slopes()
    y = jnp.zeros((CHUNK, D_MODEL), F32)
    for hh in range(N_HEADS):
        g = hh // GQA_GROUP
        qh = q[:, hh * HEAD_DIM:(hh + 1) * HEAD_DIM]
        kg = win[:, g * HEAD_DIM:(g + 1) * HEAD_DIM]
        vg = win[:, KV_DIM // 2 + g * HEAD_DIM:KV_DIM // 2 + (g + 1) * HEAD_DIM]
        s = lax.dot_general(qh, kg, (((1,), (1,)), ((), ())), preferred_element_type=F32)
        s = s * (HEAD_DIM ** -0.5) - slopes[hh] * dist
        s = jnp.where(valid, s, NEG)
        sink = sink_ref[0, hh]
        m = jnp.maximum(jnp.max(s, axis=1, keepdims=True), sink)
        e = jnp.exp(s - m)
        denom = jnp.sum(e, axis=1, keepdims=True) + jnp.exp(sink - m)
        p = e / denom
        o = jnp.dot(p.astype(BF16), vg, preferred_element_type=F32)
        y = y + jnp.dot(o.astype(BF16), wo_ref[hh * HEAD_DIM:(hh + 1) * HEAD_DIM, :],
                        preferred_element_type=F32)
    return y


def _attn_prompt_kernel(sink_ref, q_ref, kvp_ref, kvc_ref, h_ref, wo_ref, out_ref, *, qt):
    t = pl.program_id(1)
    kv = jnp.concatenate([kvp_ref[...], kvc_ref[...]], axis=0).astype(BF16)
    n_chunks = qt // CHUNK
    for c in range(n_chunks):
        r0 = c * CHUNK
        first_valid = WINDOW - CHUNK * (t * n_chunks + c)
        y = _attn_chunk(q_ref[r0:r0 + CHUNK, :], kv[r0:r0 + WINDOW + CHUNK, :],
                        first_valid, sink_ref, wo_ref)
        out_ref[r0:r0 + CHUNK, :] = h_ref[r0:r0 + CHUNK, :] + y


def _attn_prompt(sinks, q, kv, h, wo, batch, seq, qt):
    t = h.shape[0]
    tiles = seq // qt
    cur = lambda b, i: (b * tiles + i, 0)
    prev = lambda b, i: (jnp.maximum((b * tiles + i) * (qt // WINDOW) - 1, 0), 0)
    const2 = lambda b, i: (0, 0)
    return pl.pallas_call(
        functools.partial(_attn_prompt_kernel, qt=qt),
        grid=(batch, tiles),
        in_specs=[
            pl.BlockSpec((1, N_HEADS), const2, memory_space=pltpu.SMEM),
            pl.BlockSpec((qt, D_MODEL), cur),
            pl.BlockSpec((WINDOW, KV_DIM), prev),
            pl.BlockSpec((qt, KV_DIM), cur),
            pl.BlockSpec((qt, D_MODEL), cur),
            pl.BlockSpec((D_MODEL, D_MODEL), const2),
        ],
        out_specs=pl.BlockSpec((qt, D_MODEL), cur),
        out_shape=jax.ShapeDtypeStruct((t, D_MODEL), F32),
        compiler_params=_cparams(2),
        name="attn_prompt",
    )(sinks, q, kv, kv, h, wo)


def _attn_sample_kernel(sink_ref, q_ref, ck_ref, cv_ref, kvn_ref, h_ref, wo_ref, out_ref):
    kvn = kvn_ref[...]
    k = jnp.concatenate([ck_ref[...], kvn[:, :KV_DIM // 2]], axis=0)
    v = jnp.concatenate([cv_ref[...], kvn[:, KV_DIM // 2:]], axis=0)
    win = jnp.concatenate([k, v], axis=1).astype(BF16)
    y = _attn_chunk(q_ref[...], win, 0, sink_ref, wo_ref)
    out_ref[...] = h_ref[...] + y


def _attn_sample(sinks, q, ck, cv, kv, h, wo, batch):
    t = h.shape[0]
    row = lambda b: (b, 0)
    const2 = lambda b: (0, 0)
    return pl.pallas_call(
        _attn_sample_kernel,
        grid=(batch,),
        in_specs=[
            pl.BlockSpec((1, N_HEADS), const2, memory_space=pltpu.SMEM),
            pl.BlockSpec((CHUNK, D_MODEL), row),
            pl.BlockSpec((WINDOW, KV_DIM // 2), row),
            pl.BlockSpec((WINDOW, KV_DIM // 2), row),
            pl.BlockSpec((CHUNK, KV_DIM), row),
            pl.BlockSpec((CHUNK, D_MODEL), row),
            pl.BlockSpec((D_MODEL, D_MODEL), const2),
        ],
        out_specs=pl.BlockSpec((CHUNK, D_MODEL), row),
        out_shape=jax.ShapeDtypeStruct((t, D_MODEL), F32),
        compiler_params=_cparams(1),
        name="attn_sample",
    )(sinks, q, ck, cv, kv, h, wo)


def kernel(x_prompt, x_sample, cache_k_win, cache_v_win, norm_mix, norm_ffn, gmlp_w_in, gmlp_norm_v, gmlp_w_s, gmlp_b_s, gmlp_w_out, norm_kv, w_kv, attn_w_q, attn_sinks, attn_w_o, peer_w_q, peer_sub_keys, peer_u, peer_v, norm_final):
    batch, seq, _ = x_prompt.shape
    dec_batch, dec_seq, _ = x_sample.shape
    w_cache = cache_k_win.shape[1]
    assert dec_seq == CHUNK and w_cache == WINDOW and seq % MLP_CHUNK == 0

    row2 = lambda a: a.reshape(1, -1)
    win = gmlp_w_in[0].astype(BF16)
    wout = gmlp_w_out[0].astype(BF16)
    wkv = w_kv.astype(BF16)
    wq_attn = attn_w_q[0].astype(BF16)
    wo = attn_w_o[0].astype(BF16)
    wq_peer = peer_w_q.astype(BF16)
    sub_keys = peer_sub_keys.reshape(2, 2 * PEER_HEADS, N_KEYS, D_HALF).astype(BF16)
    tables = [
        jnp.concatenate([peer_u[l].reshape(N_EXPERTS, D_MODEL // LANES, LANES),
                         peer_v[l].reshape(N_EXPERTS, D_MODEL // LANES, LANES)], axis=1)
        for l in range(2)
    ]

    pos = jnp.arange(MLP_CHUNK)
    blk_i = pos[:, None] // CHUNK
    blk_j = pos[None, :] // CHUNK
    ws = gmlp_w_s[0]
    bs = gmlp_b_s[0]
    wmix_p = jnp.where((blk_j <= blk_i)[None], ws, 0.0).astype(BF16)
    ws64 = ws[:, :CHUNK, :CHUNK]
    wmix_s = jnp.where((blk_j == blk_i)[None], jnp.tile(ws64, (1, 2, 2)), 0.0).astype(BF16)
    bias_p = jnp.repeat(bs.T, LANES, axis=1)
    bias_s = jnp.repeat(jnp.tile(bs[:, :CHUNK], (1, 2)).T, LANES, axis=1)

    def trunk(x, prompt):
        t = x.shape[0]
        tm = 256
        tt = 64
        h, v_rows = _gmlp(x, row2(norm_mix[0]), win, row2(gmlp_norm_v[0]),
                          wmix_p if prompt else wmix_s, bias_p if prompt else bias_s, wout, tm)
        idx, gate = _route(h, row2(norm_ffn[0]), wq_peer[0], sub_keys[0], tm)
        h = _gather(idx, gate, h, row2(norm_ffn[0]), row2(norm_final), tables[0], tt, False)
        kv, q = _proj(h, row2(norm_kv), row2(norm_mix[1]), wkv, wq_attn, tm)
        if prompt:
            h = _attn_prompt(attn_sinks, q, kv, h, wo, batch, seq, 256)
        else:
            ck = cache_k_win.reshape(dec_batch * w_cache, KV_DIM // 2)
            cv = cache_v_win.reshape(dec_batch * w_cache, KV_DIM // 2)
            h = _attn_sample(attn_sinks, q, ck, cv, kv, h, wo, dec_batch)
        idx, gate = _route(h, row2(norm_ffn[1]), wq_peer[1], sub_keys[1], tm)
        y = _gather(idx, gate, h, row2(norm_ffn[1]), row2(norm_final), tables[1], tt, True)
        return y, kv, v_rows

    y_p, kv_p, _ = trunk(x_prompt.reshape(batch * seq, D_MODEL), True)
    y_s, kv_s, v_s = trunk(x_sample.reshape(dec_batch * dec_seq, D_MODEL), False)

    half = KV_DIM // 2
    kv_p = kv_p.reshape(batch, seq, KV_DIM)[:, seq - WINDOW:]
    prompt_k_win = kv_p[..., :half].reshape(batch, WINDOW, N_KV_HEADS, HEAD_DIM)
    prompt_v_win = kv_p[..., half:].reshape(batch, WINDOW, N_KV_HEADS, HEAD_DIM)
    kv_s = kv_s.reshape(dec_batch, dec_seq, KV_DIM)
    k_s = kv_s[..., :half].reshape(dec_batch, dec_seq, N_KV_HEADS, HEAD_DIM)
    v_s_new = kv_s[..., half:].reshape(dec_batch, dec_seq, N_KV_HEADS, HEAD_DIM)
    sample_k_win = jnp.concatenate([cache_k_win, k_s], axis=1)[:, -w_cache:]
    sample_v_win = jnp.concatenate([cache_v_win, v_s_new], axis=1)[:, -w_cache:]
    sample_gmlp_v = v_s.reshape(1, dec_batch, dec_seq, D_GATE)

    return (y_p.reshape(batch, seq, D_MODEL), y_s.reshape(dec_batch, dec_seq, D_MODEL),
            prompt_k_win, prompt_v_win, sample_k_win, sample_v_win, sample_gmlp_v)
```

```python
import functools
import math

import jax
import jax.numpy as jnp
from jax import lax
from jax.experimental import pallas as pl
from jax.experimental.pallas import tpu as pltpu

F32 = jnp.float32
BF16 = jnp.bfloat16

D_MODEL = 1024
CHUNK = 64
MLP_CHUNK = 128
GMLP_GROUPS = 8
D_GATE = D_MODEL
N_HEADS = 16
N_KV_HEADS = 4
HEAD_DIM = 64
GQA_GROUP = N_HEADS // N_KV_HEADS
WINDOW = 128
PEER_HEADS = 8
N_KEYS = 128
N_EXPERTS = N_KEYS * N_KEYS
PEER_TOPK = 16
D_HALF = 128
N_PICKS = PEER_HEADS * PEER_TOPK
EPS = 1e-6
NEG = -1e30
KV_DIM = 2 * N_KV_HEADS * HEAD_DIM

LANES = 128
SUBLANES = 8
ROWS_PER_EXPERT = 2 * D_MODEL // LANES
PICK_PITCH = 20
SLOT_ROWS = N_PICKS * PICK_PITCH
N_SLOTS = SUBLANES
PREFETCH_DIST = 6

VMEM_LIMIT = 48 * 1024 * 1024


def _rms(x, g):
    return x * lax.rsqrt(jnp.mean(x * x, axis=-1, keepdims=True) + EPS) * g


def _cparams(n_axes):
    return pltpu.CompilerParams(
        dimension_semantics=("arbitrary",) * n_axes, vmem_limit_bytes=VMEM_LIMIT)


def _gmlp_kernel(x_ref, gmix_ref, win_ref, gv_ref, wmix_ref, bias_ref, wout_ref,
                 h_ref, v_ref, *, tm):
    x = x_ref[...]
    hn = _rms(x, gmix_ref[...])
    hmid = jax.nn.gelu(jnp.dot(hn.astype(BF16), win_ref[...], preferred_element_type=F32))
    u = hmid[:, :D_GATE]
    v = _rms(hmid[:, D_GATE:], gv_ref[...])
    v_ref[...] = v
    vb = v.astype(BF16)
    bias = bias_ref[...]
    outs = []
    for c in range(tm // MLP_CHUNK):
        r0 = c * MLP_CHUNK
        cols = []
        for g in range(GMLP_GROUPS):
            c0 = g * LANES
            cols.append(jnp.dot(wmix_ref[g], vb[r0:r0 + MLP_CHUNK, c0:c0 + LANES],
                                preferred_element_type=F32))
        mixed = jnp.concatenate(cols, axis=1) + bias
        outs.append(u[r0:r0 + MLP_CHUNK, :] * mixed)
    out = outs[0] if len(outs) == 1 else jnp.concatenate(outs, axis=0)
    y = jnp.dot(out.astype(BF16), wout_ref[...], preferred_element_type=F32)
    h_ref[...] = x + y


def _gmlp(x, gmix, win, gv, wmix, bias, wout, tm):
    t = x.shape[0]
    row = lambda i: (i, 0)
    const2 = lambda i: (0, 0)
    return pl.pallas_call(
        functools.partial(_gmlp_kernel, tm=tm),
        grid=(t // tm,),
        in_specs=[
            pl.BlockSpec((tm, D_MODEL), row),
            pl.BlockSpec((1, D_MODEL), const2),
            pl.BlockSpec((D_MODEL, 2 * D_GATE), const2),
            pl.BlockSpec((1, D_GATE), const2),
            pl.BlockSpec((GMLP_GROUPS, MLP_CHUNK, MLP_CHUNK), lambda i: (0, 0, 0)),
            pl.BlockSpec((MLP_CHUNK, D_GATE), const2),
            pl.BlockSpec((D_GATE, D_MODEL), const2),
        ],
        out_specs=[pl.BlockSpec((tm, D_MODEL), row), pl.BlockSpec((tm, D_GATE), row)],
        out_shape=[jax.ShapeDtypeStruct((t, D_MODEL), F32),
                   jax.ShapeDtypeStruct((t, D_GATE), F32)],
        compiler_params=_cparams(1),
        name="gmlp_mixer",
    )(x, gmix, win, gv, wmix, bias, wout)


def _route_kernel(h_ref, g_ref, wq_ref, sk_ref, idx_ref, gate_ref,
                  sc_s, ts_s, ti_s, best_s, sel_s, e_s, gt_s, *, tm):
    hf = _rms(h_ref[...], g_ref[...])
    q = jnp.dot(hf.astype(BF16), wq_ref[...], preferred_element_type=F32).astype(BF16)
    n_hp = 2 * PEER_HEADS
    for hp in range(n_hp):
        sc_s[hp] = lax.dot_general(sk_ref[hp], q[:, hp * D_HALF:(hp + 1) * D_HALF],
                                   (((1,), (1,)), ((), ())), preferred_element_type=F32)

    key_id = lax.broadcasted_iota(jnp.int32, (N_KEYS, LANES), 0).astype(F32)

    def stage1(hp, carry):
        for lg in range(tm // LANES):
            l0 = lg * LANES
            s = sc_s[hp, :, l0:l0 + LANES]
            for k in range(PEER_TOPK):
                m = jnp.max(s, axis=0, keepdims=True)
                i = jnp.min(jnp.where(s == m, key_id, float(N_KEYS)), axis=0, keepdims=True)
                ts_s[hp, k:k + 1, l0:l0 + LANES] = m
                ti_s[hp, k:k + 1, l0:l0 + LANES] = i
                s = jnp.where(key_id == i, -jnp.inf, s)
        return carry

    lax.fori_loop(0, n_hp, stage1, 0)

    sub_id = lax.broadcasted_iota(jnp.int32, (PEER_TOPK, LANES), 0).astype(F32)
    n_cand = float(PEER_TOPK * PEER_TOPK)

    def stage2(hd, carry):
        for lg in range(tm // LANES):
            l0 = lg * LANES
            s0 = ts_s[2 * hd, :, l0:l0 + LANES]
            s1 = ts_s[2 * hd + 1, :, l0:l0 + LANES]
            i0 = ti_s[2 * hd, :, l0:l0 + LANES]
            i1 = ti_s[2 * hd + 1, :, l0:l0 + LANES]
            blocks = [s0[a:a + 1, :] + s1 for a in range(PEER_TOPK)]
            for k in range(PEER_TOPK):
                mm = blocks[0]
                for a in range(1, PEER_TOPK):
                    mm = jnp.maximum(mm, blocks[a])
                m = jnp.max(mm, axis=0, keepdims=True)
                cc = jnp.where(blocks[0] == m, sub_id, n_cand)
                for a in range(1, PEER_TOPK):
                    cc = jnp.minimum(cc, jnp.where(blocks[a] == m, sub_id + float(a * PEER_TOPK), n_cand))
                c = jnp.min(cc, axis=0, keepdims=True)
                best_s[k:k + 1, :] = m
                sel_s[k:k + 1, :] = c
                blocks = [jnp.where(sub_id == c - float(a * PEER_TOPK), -jnp.inf, blocks[a])
                          for a in range(PEER_TOPK)]
            best = best_s[...]
            sel = sel_s[...]
            sel_a = jnp.floor(sel * (1.0 / PEER_TOPK))
            sel_b = sel - sel_a * PEER_TOPK
            e0 = jnp.zeros_like(sel)
            e1 = jnp.zeros_like(sel)
            for a in range(PEER_TOPK):
                e0 = jnp.where(sel_a == float(a), i0[a:a + 1, :], e0)
                e1 = jnp.where(sel_b == float(a), i1[a:a + 1, :], e1)
            ex = jnp.exp(best - best[0:1, :])
            gate = ex / jnp.sum(ex, axis=0, keepdims=True)
            r0 = pl.multiple_of(hd * PEER_TOPK, PEER_TOPK)
            e_s[pl.ds(r0, PEER_TOPK), l0:l0 + LANES] = e0 * float(N_KEYS) + e1
            gt_s[pl.ds(r0, PEER_TOPK), l0:l0 + LANES] = gate
        return carry

    lax.fori_loop(0, PEER_HEADS, stage2, 0)

    idx_ref[...] = jnp.transpose(e_s[...]).astype(jnp.int32)
    gate_ref[...] = jnp.transpose(gt_s[...])


def _route(h, g, wq, sk, tm):
    t = h.shape[0]
    row = lambda i: (i, 0)
    const2 = lambda i: (0, 0)
    n_hp = 2 * PEER_HEADS
    return pl.pallas_call(
        functools.partial(_route_kernel, tm=tm),
        grid=(t // tm,),
        in_specs=[
            pl.BlockSpec((tm, D_MODEL), row),
            pl.BlockSpec((1, D_MODEL), const2),
            pl.BlockSpec((D_MODEL, n_hp * D_HALF), const2),
            pl.BlockSpec((n_hp, N_KEYS, D_HALF), lambda i: (0, 0, 0)),
        ],
        out_specs=[pl.BlockSpec((tm, N_PICKS), row), pl.BlockSpec((tm, N_PICKS), row)],
        out_shape=[jax.ShapeDtypeStruct((t, N_PICKS), jnp.int32),
                   jax.ShapeDtypeStruct((t, N_PICKS), F32)],
        scratch_shapes=[
            pltpu.VMEM((n_hp, N_KEYS, tm), F32),
            pltpu.VMEM((n_hp, PEER_TOPK, tm), F32),
            pltpu.VMEM((n_hp, PEER_TOPK, tm), F32),
            pltpu.VMEM((PEER_TOPK, LANES), F32),
            pltpu.VMEM((PEER_TOPK, LANES), F32),
            pltpu.VMEM((N_PICKS, tm), F32),
            pltpu.VMEM((N_PICKS, tm), F32),
        ],
        compiler_params=_cparams(1),
        name="peer_route",
    )(h, g, wq, sk)


def _gather_kernel(idx_ref, gate_ref, h_ref, g_ref, gfin_ref, tab_ref, out_ref,
                   hf_s, buf, sem, *, tt, final_norm):
    hf_s[...] = _rms(h_ref[...], g_ref[...])
    n_sub = D_MODEL // LANES
    n_groups = tt // N_SLOTS
    n_pieces = 2 * n_sub
    picks_per_piece = N_PICKS // n_pieces

    def pick_copy(e, slot, r):
        dst = buf.at[pl.ds(slot * SLOT_ROWS + r * PICK_PITCH, ROWS_PER_EXPERT), :]
        return pltpu.make_async_copy(tab_ref.at[e], dst, sem.at[slot])

    def issue(j, slot, r_lo, r_hi):
        for r in range(r_lo, r_hi):
            pick_copy(idx_ref[j, r], slot, r).start()

    def wait(slot):
        for r in range(N_PICKS):
            pick_copy(0, slot, r).wait()

    eye = (lax.broadcasted_iota(jnp.int32, (N_PICKS, N_PICKS), 0)
           == lax.broadcasted_iota(jnp.int32, (N_PICKS, N_PICKS), 1))
    sub = lax.broadcasted_iota(jnp.int32, (SUBLANES, LANES), 0)

    def token(k, x8, gate8, tiles, prefetch):
        base = k * SLOT_ROWS
        wait(k)
        piece = 0
        acc = jnp.zeros((N_PICKS, LANES), F32)
        for s in range(n_sub):
            u_s = buf[pl.ds(base + s, N_PICKS, stride=PICK_PITCH), :]
            acc = acc + u_s * x8[s][k:k + 1, :]
            prefetch(piece)
            piece += 1
        act = jax.nn.gelu(jnp.sum(acc, axis=1, keepdims=True))
        gcol = jnp.sum(jnp.where(eye, gate8[k:k + 1, :], 0.0), axis=1, keepdims=True)
        w = gcol * act
        new_tiles = []
        for s in range(n_sub):
            v_s = buf[pl.ds(base + n_sub + s, N_PICKS, stride=PICK_PITCH), :]
            o_s = jnp.sum(v_s * w, axis=0, keepdims=True)
            new_tiles.append(jnp.where(sub == k, o_s, tiles[s]))
            prefetch(piece)
            piece += 1
        return new_tiles

    def group(gi, last):
        r0 = pl.multiple_of(gi * N_SLOTS, N_SLOTS)
        x8 = [hf_s[pl.ds(r0, N_SLOTS), s * LANES:(s + 1) * LANES] for s in range(n_sub)]
        gate8 = gate_ref[pl.ds(r0, N_SLOTS), :]
        tiles = [jnp.zeros((SUBLANES, LANES), F32) for _ in range(n_sub)]
        for k in range(N_SLOTS):
            ahead = k + PREFETCH_DIST
            if last and ahead >= N_SLOTS:
                prefetch = lambda piece: None
            else:
                def prefetch(piece, ahead=ahead):
                    issue(r0 + ahead, ahead % N_SLOTS,
                          piece * picks_per_piece, (piece + 1) * picks_per_piece)
            tiles = token(k, x8, gate8, tiles, prefetch)
        res = [h_ref[pl.ds(r0, N_SLOTS), s * LANES:(s + 1) * LANES] + tiles[s] for s in range(n_sub)]
        if final_norm:
            ss = res[0] * res[0]
            for s in range(1, n_sub):
                ss = ss + res[s] * res[s]
            scale = lax.rsqrt(jnp.sum(ss, axis=1, keepdims=True) * (1.0 / D_MODEL) + EPS)
            res = [res[s] * scale * gfin_ref[:, s * LANES:(s + 1) * LANES] for s in range(n_sub)]
        for s in range(n_sub):
            out_ref[pl.ds(r0, N_SLOTS), s * LANES:(s + 1) * LANES] = res[s]

    for j in range(PREFETCH_DIST):
        issue(j, j, 0, N_PICKS)

    def body(gi, carry):
        group(gi, False)
        return carry

    lax.fori_loop(0, n_groups - 1, body, 0)
    group(n_groups - 1, True)


def _gather(idx, gate, h, g, gfin, tab, tt, final_norm):
    t = h.shape[0]
    row = lambda i: (i, 0)
    const2 = lambda i: (0, 0)
    return pl.pallas_call(
        functools.partial(_gather_kernel, tt=tt, final_norm=final_norm),
        grid=(t // tt,),
        in_specs=[
            pl.BlockSpec((tt, N_PICKS), row, memory_space=pltpu.SMEM),
            pl.BlockSpec((tt, N_PICKS), row),
            pl.BlockSpec((tt, D_MODEL), row),
            pl.BlockSpec((1, D_MODEL), const2),
            pl.BlockSpec((1, D_MODEL), const2),
            pl.BlockSpec(memory_space=pl.ANY),
        ],
        out_specs=pl.BlockSpec((tt, D_MODEL), row),
        out_shape=jax.ShapeDtypeStruct((t, D_MODEL), F32),
        scratch_shapes=[
            pltpu.VMEM((tt, D_MODEL), F32),
            pltpu.VMEM((N_SLOTS * SLOT_ROWS, LANES), F32),
            pltpu.SemaphoreType.DMA((N_SLOTS,)),
        ],
        compiler_params=_cparams(1),
        name="peer_experts",
    )(idx, gate, h, g, gfin, tab)


def _proj_kernel(h_ref, gkv_ref, gq_ref, wkv_ref, wq_ref, kv_ref, q_ref):
    h = h_ref[...]
    n = h * lax.rsqrt(jnp.mean(h * h, axis=-1, keepdims=True) + EPS)
    kv_ref[...] = jnp.dot((n * gkv_ref[...]).astype(BF16), wkv_ref[...], preferred_element_type=F32)
    q_ref[...] = jnp.dot((n * gq_ref[...]).astype(BF16), wq_ref[...],
                         preferred_element_type=F32).astype(BF16)


def _proj(h, gkv, gq, wkv, wq, tm):
    t = h.shape[0]
    row = lambda i: (i, 0)
    const2 = lambda i: (0, 0)
    return pl.pallas_call(
        _proj_kernel,
        grid=(t // tm,),
        in_specs=[
            pl.BlockSpec((tm, D_MODEL), row),
            pl.BlockSpec((1, D_MODEL), const2),
            pl.BlockSpec((1, D_MODEL), const2),
            pl.BlockSpec((D_MODEL, KV_DIM), const2),
            pl.BlockSpec((D_MODEL, D_MODEL), const2),
        ],
        out_specs=[pl.BlockSpec((tm, KV_DIM), row), pl.BlockSpec((tm, D_MODEL), row)],
        out_shape=[jax.ShapeDtypeStruct((t, KV_DIM), F32),
                   jax.ShapeDtypeStruct((t, D_MODEL), BF16)],
        compiler_params=_cparams(1),
        name="kv_q_proj",
    )(h, gkv, gq, wkv, wq)


def _alibi_slopes():
    return [2.0 ** (-8.0 * (h + 1) / N_HEADS) for h in range(N_HEADS)]


def _attn_chunk(q, win, first_valid, sink_ref, wo_ref):
    n_keys = WINDOW + CHUNK
    qi = lax.broadcasted_iota(jnp.int32, (CHUNK, n_keys), 0)
    kj = lax.broadcasted_iota(jnp.int32, (CHUNK, n_keys), 1)
    dist = jnp.abs(WINDOW + qi - kj).astype(F32)
    valid = kj >= first_valid
    slopes = _alibi_slopes()
    y = jnp.zeros((CHUNK, D_MODEL), F32)
    for hh in range(N_HEADS):
        g = hh // GQA_GROUP
        qh = q[:, hh * HEAD_DIM:(hh + 1) * HEAD_DIM]
        kg = win[:, g * HEAD_DIM:(g + 1) * HEAD_DIM]
        vg = win[:, KV_DIM // 2 + g * HEAD_DIM:KV_DIM // 2 + (g + 1) * HEAD_DIM]
        s = lax.dot_general(qh, kg, (((1,), (1,)), ((), ())), preferred_element_type=F32)
        s = s * (HEAD_DIM ** -0.5) - slopes[hh] * dist
        s = jnp.where(valid, s, NEG)
        sink = sink_ref[0, hh]
        m = jnp.maximum(jnp.max(s, axis=1, keepdims=True), sink)
        e = jnp.exp(s - m)
        denom = jnp.sum(e, axis=1, keepdims=True) + jnp.exp(sink - m)
        p = e / denom
        o = jnp.dot(p.astype(BF16), vg, preferred_element_type=F32)
        y = y + jnp.dot(o.astype(BF16), wo_ref[hh * HEAD_DIM:(hh + 1) * HEAD_DIM, :],
                        preferred_element_type=F32)
    return y


def _attn_prompt_kernel(sink_ref, q_ref, kvp_ref, kvc_ref, h_ref, wo_ref, out_ref, *, qt):
    t = pl.program_id(1)
    kv = jnp.concatenate([kvp_ref[...], kvc_ref[...]], axis=0).astype(BF16)
    n_chunks = qt // CHUNK
    for c in range(n_chunks):
        r0 = c * CHUNK
        first_valid = WINDOW - CHUNK * (t * n_chunks + c)
        y = _attn_chunk(q_ref[r0:r0 + CHUNK, :], kv[r0:r0 + WINDOW + CHUNK, :],
                        first_valid, sink_ref, wo_ref)
        out_ref[r0:r0 + CHUNK, :] = h_ref[r0:r0 + CHUNK, :] + y


def _attn_prompt(sinks, q, kv, h, wo, batch, seq, qt):
    t = h.shape[0]
    tiles = seq // qt
    cur = lambda b, i: (b * tiles + i, 0)
    prev = lambda b, i: (jnp.maximum((b * tiles + i) * (qt // WINDOW) - 1, 0), 0)
    const2 = lambda b, i: (0, 0)
    return pl.pallas_call(
        functools.partial(_attn_prompt_kernel, qt=qt),
        grid=(batch, tiles),
        in_specs=[
            pl.BlockSpec((1, N_HEADS), const2, memory_space=pltpu.SMEM),
            pl.BlockSpec((qt, D_MODEL), cur),
            pl.BlockSpec((WINDOW, KV_DIM), prev),
            pl.BlockSpec((qt, KV_DIM), cur),
            pl.BlockSpec((qt, D_MODEL), cur),
            pl.BlockSpec((D_MODEL, D_MODEL), const2),
        ],
        out_specs=pl.BlockSpec((qt, D_MODEL), cur),
        out_shape=jax.ShapeDtypeStruct((t, D_MODEL), F32),
        compiler_params=_cparams(2),
        name="attn_prompt",
    )(sinks, q, kv, kv, h, wo)


def _attn_sample_kernel(sink_ref, q_ref, ck_ref, cv_ref, kvn_ref, h_ref, wo_ref, out_ref):
    kvn = kvn_ref[...]
    k = jnp.concatenate([ck_ref[...], kvn[:, :KV_DIM // 2]], axis=0)
    v = jnp.concatenate([cv_ref[...], kvn[:, KV_DIM // 2:]], axis=0)
    win = jnp.concatenate([k, v], axis=1).astype(BF16)
    y = _attn_chunk(q_ref[...], win, 0, sink_ref, wo_ref)
    out_ref[...] = h_ref[...] + y


def _attn_sample(sinks, q, ck, cv, kv, h, wo, batch):
    t = h.shape[0]
    row = lambda b: (b, 0)
    const2 = lambda b: (0, 0)
    return pl.pallas_call(
        _attn_sample_kernel,
        grid=(batch,),
        in_specs=[
            pl.BlockSpec((1, N_HEADS), const2, memory_space=pltpu.SMEM),
            pl.BlockSpec((CHUNK, D_MODEL), row),
            pl.BlockSpec((WINDOW, KV_DIM // 2), row),
            pl.BlockSpec((WINDOW, KV_DIM // 2), row),
            pl.BlockSpec((CHUNK, KV_DIM), row),
            pl.BlockSpec((CHUNK, D_MODEL), row),
            pl.BlockSpec((D_MODEL, D_MODEL), const2),
        ],
        out_specs=pl.BlockSpec((CHUNK, D_MODEL), row),
        out_shape=jax.ShapeDtypeStruct((t, D_MODEL), F32),
        compiler_params=_cparams(1),
        name="attn_sample",
    )(sinks, q, ck, cv, kv, h, wo)


def kernel(x_prompt, x_sample, cache_k_win, cache_v_win, norm_mix, norm_ffn, gmlp_w_in, gmlp_norm_v, gmlp_w_s, gmlp_b_s, gmlp_w_out, norm_kv, w_kv, attn_w_q, attn_sinks, attn_w_o, peer_w_q, peer_sub_keys, peer_u, peer_v, norm_final):
    batch, seq, _ = x_prompt.shape
    dec_batch, dec_seq, _ = x_sample.shape
    w_cache = cache_k_win.shape[1]
    assert dec_seq == CHUNK and w_cache == WINDOW and seq % MLP_CHUNK == 0

    row2 = lambda a: a.reshape(1, -1)
    win = gmlp_w_in[0].astype(BF16)
    wout = gmlp_w_out[0].astype(BF16)
    wkv = w_kv.astype(BF16)
    wq_attn = attn_w_q[0].astype(BF16)
    wo = attn_w_o[0].astype(BF16)
    wq_peer = peer_w_q.astype(BF16)
    sub_keys = peer_sub_keys.reshape(2, 2 * PEER_HEADS, N_KEYS, D_HALF).astype(BF16)
    tables = [
        jnp.concatenate([peer_u[l].reshape(N_EXPERTS, D_MODEL // LANES, LANES),
                         peer_v[l].reshape(N_EXPERTS, D_MODEL // LANES, LANES)], axis=1)
        for l in range(2)
    ]

    pos = jnp.arange(MLP_CHUNK)
    blk_i = pos[:, None] // CHUNK
    blk_j = pos[None, :] // CHUNK
    ws = gmlp_w_s[0]
    bs = gmlp_b_s[0]
    wmix_p = jnp.where((blk_j <= blk_i)[None], ws, 0.0).astype(BF16)
    ws64 = ws[:, :CHUNK, :CHUNK]
    wmix_s = jnp.where((blk_j == blk_i)[None], jnp.tile(ws64, (1, 2, 2)), 0.0).astype(BF16)
    bias_p = jnp.repeat(bs.T, LANES, axis=1)
    bias_s = jnp.repeat(jnp.tile(bs[:, :CHUNK], (1, 2)).T, LANES, axis=1)

    def trunk(x, prompt):
        t = x.shape[0]
        tm = 256
        tt = 128
        h, v_rows = _gmlp(x, row2(norm_mix[0]), win, row2(gmlp_norm_v[0]),
                          wmix_p if prompt else wmix_s, bias_p if prompt else bias_s, wout, tm)
        idx, gate = _route(h, row2(norm_ffn[0]), wq_peer[0], sub_keys[0], tm)
        h = _gather(idx, gate, h, row2(norm_ffn[0]), row2(norm_final), tables[0], tt, False)
        kv, q = _proj(h, row2(norm_kv), row2(norm_mix[1]), wkv, wq_attn, tm)
        if prompt:
            h = _attn_prompt(attn_sinks, q, kv, h, wo, batch, seq, 256)
        else:
            ck = cache_k_win.reshape(dec_batch * w_cache, KV_DIM // 2)
            cv = cache_v_win.reshape(dec_batch * w_cache, KV_DIM // 2)
            h = _attn_sample(attn_sinks, q, ck, cv, kv, h, wo, dec_batch)
        idx, gate = _route(h, row2(norm_ffn[1]), wq_peer[1], sub_keys[1], tm)
        y = _gather(idx, gate, h, row2(norm_ffn[1]), row2(norm_final), tables[1], tt, True)
        return y, kv, v_rows

    y_p, kv_p, _ = trunk(x_prompt.reshape(batch * seq, D_MODEL), True)
    y_s, kv_s, v_s = trunk(x_sample.reshape(dec_batch * dec_seq, D_MODEL), False)

    half = KV_DIM // 2
    kv_p = kv_p.reshape(batch, seq, KV_DIM)[:, seq - WINDOW:]
    prompt_k_win = kv_p[..., :half].reshape(batch, WINDOW, N_KV_HEADS, HEAD_DIM)
    prompt_v_win = kv_p[..., half:].reshape(batch, WINDOW, N_KV_HEADS, HEAD_DIM)
    kv_s = kv_s.reshape(dec_batch, dec_seq, KV_DIM)
    k_s = kv_s[..., :half].reshape(dec_batch, dec_seq, N_KV_HEADS, HEAD_DIM)
    v_s_new = kv_s[..., half:].reshape(dec_batch, dec_seq, N_KV_HEADS, HEAD_DIM)
    sample_k_win = jnp.concatenate([cache_k_win, k_s], axis=1)[:, -w_cache:]
    sample_v_win = jnp.concatenate([cache_v_win, v_s_new], axis=1)[:, -w_cache:]
    sample_gmlp_v = v_s.reshape(1, dec_batch, dec_seq, D_GATE)

    return (y_p.reshape(batch, seq, D_MODEL), y_s.reshape(dec_batch, dec_seq, D_MODEL),
            prompt_k_win, prompt_v_win, sample_k_win, sample_v_win, sample_gmlp_v)
```

```python
import functools
import math

import jax
import jax.numpy as jnp
from jax import lax
from jax.experimental import pallas as pl
from jax.experimental.pallas import tpu as pltpu

F32 = jnp.float32
BF16 = jnp.bfloat16

D_MODEL = 1024
CHUNK = 64
MLP_CHUNK = 128
GMLP_GROUPS = 8
D_GATE = D_MODEL
N_HEADS = 16
N_KV_HEADS = 4
HEAD_DIM = 64
GQA_GROUP = N_HEADS // N_KV_HEADS
WINDOW = 128
PEER_HEADS = 8
N_KEYS = 128
N_EXPERTS = N_KEYS * N_KEYS
PEER_TOPK = 16
D_HALF = 128
N_PICKS = PEER_HEADS * PEER_TOPK
EPS = 1e-6
NEG = -1e30
KV_DIM = 2 * N_KV_HEADS * HEAD_DIM

LANES = 128
SUBLANES = 8
ROWS_PER_EXPERT = 2 * D_MODEL // LANES
PICK_PITCH = 20
SLOT_ROWS = N_PICKS * PICK_PITCH
N_SLOTS = SUBLANES
PREFETCH_DIST = 6

VMEM_LIMIT = 48 * 1024 * 1024


def _rms(x, g):
    return x * lax.rsqrt(jnp.mean(x * x, axis=-1, keepdims=True) + EPS) * g


def _cparams(n_axes):
    return pltpu.CompilerParams(
        dimension_semantics=("arbitrary",) * n_axes, vmem_limit_bytes=VMEM_LIMIT)


def _gmlp_kernel(x_ref, gmix_ref, win_ref, gv_ref, wmix_ref, bias_ref, wout_ref,
                 h_ref, v_ref, *, tm):
    x = x_ref[...]
    hn = _rms(x, gmix_ref[...])
    hmid = jax.nn.gelu(jnp.dot(hn.astype(BF16), win_ref[...], preferred_element_type=F32))
    u = hmid[:, :D_GATE]
    v = _rms(hmid[:, D_GATE:], gv_ref[...])
    v_ref[...] = v
    vb = v.astype(BF16)
    bias = bias_ref[...]
    outs = []
    for c in range(tm // MLP_CHUNK):
        r0 = c * MLP_CHUNK
        cols = []
        for g in range(GMLP_GROUPS):
            c0 = g * LANES
            cols.append(jnp.dot(wmix_ref[g], vb[r0:r0 + MLP_CHUNK, c0:c0 + LANES],
                                preferred_element_type=F32))
        mixed = jnp.concatenate(cols, axis=1) + bias
        outs.append(u[r0:r0 + MLP_CHUNK, :] * mixed)
    out = outs[0] if len(outs) == 1 else jnp.concatenate(outs, axis=0)
    y = jnp.dot(out.astype(BF16), wout_ref[...], preferred_element_type=F32)
    h_ref[...] = x + y


def _gmlp(x, gmix, win, gv, wmix, bias, wout, tm):
    t = x.shape[0]
    row = lambda i: (i, 0)
    const2 = lambda i: (0, 0)
    return pl.pallas_call(
        functools.partial(_gmlp_kernel, tm=tm),
        grid=(t // tm,),
        in_specs=[
            pl.BlockSpec((tm, D_MODEL), row),
            pl.BlockSpec((1, D_MODEL), const2),
            pl.BlockSpec((D_MODEL, 2 * D_GATE), const2),
            pl.BlockSpec((1, D_GATE), const2),
            pl.BlockSpec((GMLP_GROUPS, MLP_CHUNK, MLP_CHUNK), lambda i: (0, 0, 0)),
            pl.BlockSpec((MLP_CHUNK, D_GATE), const2),
            pl.BlockSpec((D_GATE, D_MODEL), const2),
        ],
        out_specs=[pl.BlockSpec((tm, D_MODEL), row), pl.BlockSpec((tm, D_GATE), row)],
        out_shape=[jax.ShapeDtypeStruct((t, D_MODEL), F32),
                   jax.ShapeDtypeStruct((t, D_GATE), F32)],
        compiler_params=_cparams(1),
        name="gmlp_mixer",
    )(x, gmix, win, gv, wmix, bias, wout)


def _route_kernel(h_ref, g_ref, wq_ref, sk_ref, idx_ref, gate_ref,
                  sc_s, ts_s, ti_s, best_s, sel_s, e_s, gt_s, *, tm):
    hf = _rms(h_ref[...], g_ref[...])
    q = jnp.dot(hf.astype(BF16), wq_ref[...], preferred_element_type=F32).astype(BF16)
    n_hp = 2 * PEER_HEADS
    for hp in range(n_hp):
        sc_s[hp] = lax.dot_general(sk_ref[hp], q[:, hp * D_HALF:(hp + 1) * D_HALF],
                                   (((1,), (1,)), ((), ())), preferred_element_type=F32)

    key_id = lax.broadcasted_iota(jnp.int32, (N_KEYS, LANES), 0).astype(F32)

    def stage1(hp, carry):
        for lg in range(tm // LANES):
            l0 = lg * LANES
            s = sc_s[hp, :, l0:l0 + LANES]
            for k in range(PEER_TOPK):
                m = jnp.max(s, axis=0, keepdims=True)
                i = jnp.min(jnp.where(s == m, key_id, float(N_KEYS)), axis=0, keepdims=True)
                ts_s[hp, k:k + 1, l0:l0 + LANES] = m
                ti_s[hp, k:k + 1, l0:l0 + LANES] = i
                s = jnp.where(key_id == i, -jnp.inf, s)
        return carry

    lax.fori_loop(0, n_hp, stage1, 0)

    sub_id = lax.broadcasted_iota(jnp.int32, (PEER_TOPK, LANES), 0).astype(F32)
    n_cand = float(PEER_TOPK * PEER_TOPK)

    def stage2(hd, carry):
        for lg in range(tm // LANES):
            l0 = lg * LANES
            s0 = ts_s[2 * hd, :, l0:l0 + LANES]
            s1 = ts_s[2 * hd + 1, :, l0:l0 + LANES]
            i0 = ti_s[2 * hd, :, l0:l0 + LANES]
            i1 = ti_s[2 * hd + 1, :, l0:l0 + LANES]
            blocks = [s0[a:a + 1, :] + s1 for a in range(PEER_TOPK)]
            for k in range(PEER_TOPK):
                mm = blocks[0]
                for a in range(1, PEER_TOPK):
                    mm = jnp.maximum(mm, blocks[a])
                m = jnp.max(mm, axis=0, keepdims=True)
                cc = jnp.where(blocks[0] == m, sub_id, n_cand)
                for a in range(1, PEER_TOPK):
                    cc = jnp.minimum(cc, jnp.where(blocks[a] == m, sub_id + float(a * PEER_TOPK), n_cand))
                c = jnp.min(cc, axis=0, keepdims=True)
                best_s[k:k + 1, :] = m
                sel_s[k:k + 1, :] = c
                blocks = [jnp.where(sub_id == c - float(a * PEER_TOPK), -jnp.inf, blocks[a])
                          for a in range(PEER_TOPK)]
            best = best_s[...]
            sel = sel_s[...]
            sel_a = jnp.floor(sel * (1.0 / PEER_TOPK))
            sel_b = sel - sel_a * PEER_TOPK
            e0 = jnp.zeros_like(sel)
            e1 = jnp.zeros_like(sel)
            for a in range(PEER_TOPK):
                e0 = jnp.where(sel_a == float(a), i0[a:a + 1, :], e0)
                e1 = jnp.where(sel_b == float(a), i1[a:a + 1, :], e1)
            ex = jnp.exp(best - best[0:1, :])
            gate = ex / jnp.sum(ex, axis=0, keepdims=True)
            r0 = pl.multiple_of(hd * PEER_TOPK, PEER_TOPK)
            e_s[pl.ds(r0, PEER_TOPK), l0:l0 + LANES] = e0 * float(N_KEYS) + e1
            gt_s[pl.ds(r0, PEER_TOPK), l0:l0 + LANES] = gate
        return carry

    lax.fori_loop(0, PEER_HEADS, stage2, 0)

    idx_ref[...] = jnp.transpose(e_s[...]).astype(jnp.int32)
    gate_ref[...] = jnp.transpose(gt_s[...])


def _route(h, g, wq, sk, tm):
    t = h.shape[0]
    row = lambda i: (i, 0)
    const2 = lambda i: (0, 0)
    n_hp = 2 * PEER_HEADS
    return pl.pallas_call(
        functools.partial(_route_kernel, tm=tm),
        grid=(t // tm,),
        in_specs=[
            pl.BlockSpec((tm, D_MODEL), row),
            pl.BlockSpec((1, D_MODEL), const2),
            pl.BlockSpec((D_MODEL, n_hp * D_HALF), const2),
            pl.BlockSpec((n_hp, N_KEYS, D_HALF), lambda i: (0, 0, 0)),
        ],
        out_specs=[pl.BlockSpec((tm, N_PICKS), row), pl.BlockSpec((tm, N_PICKS), row)],
        out_shape=[jax.ShapeDtypeStruct((t, N_PICKS), jnp.int32),
                   jax.ShapeDtypeStruct((t, N_PICKS), F32)],
        scratch_shapes=[
            pltpu.VMEM((n_hp, N_KEYS, tm), F32),
            pltpu.VMEM((n_hp, PEER_TOPK, tm), F32),
            pltpu.VMEM((n_hp, PEER_TOPK, tm), F32),
            pltpu.VMEM((PEER_TOPK, LANES), F32),
            pltpu.VMEM((PEER_TOPK, LANES), F32),
            pltpu.VMEM((N_PICKS, tm), F32),
            pltpu.VMEM((N_PICKS, tm), F32),
        ],
        compiler_params=_cparams(1),
        name="peer_route",
    )(h, g, wq, sk)


def _gather_kernel(idx_ref, gate_ref, h_ref, g_ref, gfin_ref, tab_ref, out_ref,
                   hf_s, buf, sem, *, tt, final_norm):
    hf_s[...] = _rms(h_ref[...], g_ref[...])
    n_sub = D_MODEL // LANES
    n_groups = tt // N_SLOTS
    n_pieces = 2 * n_sub
    picks_per_piece = N_PICKS // n_pieces

    def pick_copy(e, slot, r):
        dst = buf.at[pl.ds(slot * SLOT_ROWS + r * PICK_PITCH, ROWS_PER_EXPERT), :]
        return pltpu.make_async_copy(tab_ref.at[e], dst, sem.at[slot])

    def issue(j, slot, r_lo, r_hi):
        for r in range(r_lo, r_hi):
            pick_copy(idx_ref[j, r], slot, r).start(priority=r % 2)

    def wait(slot):
        for r in range(N_PICKS):
            pick_copy(0, slot, r).wait()

    eye = (lax.broadcasted_iota(jnp.int32, (N_PICKS, N_PICKS), 0)
           == lax.broadcasted_iota(jnp.int32, (N_PICKS, N_PICKS), 1))
    sub = lax.broadcasted_iota(jnp.int32, (SUBLANES, LANES), 0)

    def token(k, x8, gate8, tiles, prefetch):
        base = k * SLOT_ROWS
        wait(k)
        piece = 0
        acc = jnp.zeros((N_PICKS, LANES), F32)
        for s in range(n_sub):
            u_s = buf[pl.ds(base + s, N_PICKS, stride=PICK_PITCH), :]
            acc = acc + u_s * x8[s][k:k + 1, :]
            prefetch(piece)
            piece += 1
        act = jax.nn.gelu(jnp.sum(acc, axis=1, keepdims=True))
        gcol = jnp.sum(jnp.where(eye, gate8[k:k + 1, :], 0.0), axis=1, keepdims=True)
        w = gcol * act
        new_tiles = []
        for s in range(n_sub):
            v_s = buf[pl.ds(base + n_sub + s, N_PICKS, stride=PICK_PITCH), :]
            o_s = jnp.sum(v_s * w, axis=0, keepdims=True)
            new_tiles.append(jnp.where(sub == k, o_s, tiles[s]))
            prefetch(piece)
            piece += 1
        return new_tiles

    def group(gi, last):
        r0 = pl.multiple_of(gi * N_SLOTS, N_SLOTS)
        x8 = [hf_s[pl.ds(r0, N_SLOTS), s * LANES:(s + 1) * LANES] for s in range(n_sub)]
        gate8 = gate_ref[pl.ds(r0, N_SLOTS), :]
        tiles = [jnp.zeros((SUBLANES, LANES), F32) for _ in range(n_sub)]
        for k in range(N_SLOTS):
            ahead = k + PREFETCH_DIST
            if last and ahead >= N_SLOTS:
                prefetch = lambda piece: None
            else:
                def prefetch(piece, ahead=ahead):
                    issue(r0 + ahead, ahead % N_SLOTS,
                          piece * picks_per_piece, (piece + 1) * picks_per_piece)
            tiles = token(k, x8, gate8, tiles, prefetch)
        res = [h_ref[pl.ds(r0, N_SLOTS), s * LANES:(s + 1) * LANES] + tiles[s] for s in range(n_sub)]
        if final_norm:
            ss = res[0] * res[0]
            for s in range(1, n_sub):
                ss = ss + res[s] * res[s]
            scale = lax.rsqrt(jnp.sum(ss, axis=1, keepdims=True) * (1.0 / D_MODEL) + EPS)
            res = [res[s] * scale * gfin_ref[:, s * LANES:(s + 1) * LANES] for s in range(n_sub)]
        for s in range(n_sub):
            out_ref[pl.ds(r0, N_SLOTS), s * LANES:(s + 1) * LANES] = res[s]

    for j in range(PREFETCH_DIST):
        issue(j, j, 0, N_PICKS)

    def body(gi, carry):
        group(gi, False)
        return carry

    lax.fori_loop(0, n_groups - 1, body, 0)
    group(n_groups - 1, True)


def _gather(idx, gate, h, g, gfin, tab, tt, final_norm):
    t = h.shape[0]
    row = lambda i: (i, 0)
    const2 = lambda i: (0, 0)
    return pl.pallas_call(
        functools.partial(_gather_kernel, tt=tt, final_norm=final_norm),
        grid=(t // tt,),
        in_specs=[
            pl.BlockSpec((tt, N_PICKS), row, memory_space=pltpu.SMEM),
            pl.BlockSpec((tt, N_PICKS), row),
            pl.BlockSpec((tt, D_MODEL), row),
            pl.BlockSpec((1, D_MODEL), const2),
            pl.BlockSpec((1, D_MODEL), const2),
            pl.BlockSpec(memory_space=pl.ANY),
        ],
        out_specs=pl.BlockSpec((tt, D_MODEL), row),
        out_shape=jax.ShapeDtypeStruct((t, D_MODEL), F32),
        scratch_shapes=[
            pltpu.VMEM((tt, D_MODEL), F32),
            pltpu.VMEM((N_SLOTS * SLOT_ROWS, LANES), F32),
            pltpu.SemaphoreType.DMA((N_SLOTS,)),
        ],
        compiler_params=_cparams(1),
        name="peer_experts",
    )(idx, gate, h, g, gfin, tab)


def _proj_kernel(h_ref, gkv_ref, gq_ref, wkv_ref, wq_ref, kv_ref, q_ref):
    h = h_ref[...]
    n = h * lax.rsqrt(jnp.mean(h * h, axis=-1, keepdims=True) + EPS)
    kv_ref[...] = jnp.dot((n * gkv_ref[...]).astype(BF16), wkv_ref[...], preferred_element_type=F32)
    q_ref[...] = jnp.dot((n * gq_ref[...]).astype(BF16), wq_ref[...],
                         preferred_element_type=F32).astype(BF16)


def _proj(h, gkv, gq, wkv, wq, tm):
    t = h.shape[0]
    row = lambda i: (i, 0)
    const2 = lambda i: (0, 0)
    return pl.pallas_call(
        _proj_kernel,
        grid=(t // tm,),
        in_specs=[
            pl.BlockSpec((tm, D_MODEL), row),
            pl.BlockSpec((1, D_MODEL), const2),
            pl.BlockSpec((1, D_MODEL), const2),
            pl.BlockSpec((D_MODEL, KV_DIM), const2),
            pl.BlockSpec((D_MODEL, D_MODEL), const2),
        ],
        out_specs=[pl.BlockSpec((tm, KV_DIM), row), pl.BlockSpec((tm, D_MODEL), row)],
        out_shape=[jax.ShapeDtypeStruct((t, KV_DIM), F32),
                   jax.ShapeDtypeStruct((t, D_MODEL), BF16)],
        compiler_params=_cparams(1),
        name="kv_q_proj",
    )(h, gkv, gq, wkv, wq)


def _alibi_slopes():
    return [2.0 ** (-8.0 * (h + 1) / N_HEADS) for h in range(N_HEADS)]


def _attn_chunk(q, win, first_valid, sink_ref, wo_ref):
    n_keys = WINDOW + CHUNK
    qi = lax.broadcasted_iota(jnp.int32, (CHUNK, n_keys), 0)
    kj = lax.broadcasted_iota(jnp.int32, (CHUNK, n_keys), 1)
    dist = jnp.abs(WINDOW + qi - kj).astype(F32)
    valid = kj >= first_valid
    slopes = _alibi_slopes()
    y = jnp.zeros((CHUNK, D_MODEL), F32)
    for hh in range(N_HEADS):
        g = hh // GQA_GROUP
        qh = q[:, hh * HEAD_DIM:(hh + 1) * HEAD_DIM]
        kg = win[:, g * HEAD_DIM:(g + 1) * HEAD_DIM]
        vg = win[:, KV_DIM // 2 + g * HEAD_DIM:KV_DIM // 2 + (g + 1) * HEAD_DIM]
        s = lax.dot_general(qh, kg, (((1,), (1,)), ((), ())), preferred_element_type=F32)
        s = s * (HEAD_DIM ** -0.5) - slopes[hh] * dist
        s = jnp.where(valid, s, NEG)
        sink = sink_ref[0, hh]
        m = jnp.maximum(jnp.max(s, axis=1, keepdims=True), sink)
        e = jnp.exp(s - m)
        denom = jnp.sum(e, axis=1, keepdims=True) + jnp.exp(sink - m)
        p = e / denom
        o = jnp.dot(p.astype(BF16), vg, preferred_element_type=F32)
        y = y + jnp.dot(o.astype(BF16), wo_ref[hh * HEAD_DIM:(hh + 1) * HEAD_DIM, :],
                        preferred_element_type=F32)
    return y


def _attn_prompt_kernel(sink_ref, q_ref, kvp_ref, kvc_ref, h_ref, wo_ref, out_ref, *, qt):
    t = pl.program_id(1)
    kv = jnp.concatenate([kvp_ref[...], kvc_ref[...]], axis=0).astype(BF16)
    n_chunks = qt // CHUNK
    for c in range(n_chunks):
        r0 = c * CHUNK
        first_valid = WINDOW - CHUNK * (t * n_chunks + c)
        y = _attn_chunk(q_ref[r0:r0 + CHUNK, :], kv[r0:r0 + WINDOW + CHUNK, :],
                        first_valid, sink_ref, wo_ref)
        out_ref[r0:r0 + CHUNK, :] = h_ref[r0:r0 + CHUNK, :] + y


def _attn_prompt(sinks, q, kv, h, wo, batch, seq, qt):
    t = h.shape[0]
    tiles = seq // qt
    cur = lambda b, i: (b * tiles + i, 0)
    prev = lambda b, i: (jnp.maximum((b * tiles + i) * (qt // WINDOW) - 1, 0), 0)
    const2 = lambda b, i: (0, 0)
    return pl.pallas_call(
        functools.partial(_attn_prompt_kernel, qt=qt),
        grid=(batch, tiles),
        in_specs=[
            pl.BlockSpec((1, N_HEADS), const2, memory_space=pltpu.SMEM),
            pl.BlockSpec((qt, D_MODEL), cur),
            pl.BlockSpec((WINDOW, KV_DIM), prev),
            pl.BlockSpec((qt, KV_DIM), cur),
            pl.BlockSpec((qt, D_MODEL), cur),
            pl.BlockSpec((D_MODEL, D_MODEL), const2),
        ],
        out_specs=pl.BlockSpec((qt, D_MODEL), cur),
        out_shape=jax.ShapeDtypeStruct((t, D_MODEL), F32),
        compiler_params=_cparams(2),
        name="attn_prompt",
    )(sinks, q, kv, kv, h, wo)


def _attn_sample_kernel(sink_ref, q_ref, ck_ref, cv_ref, kvn_ref, h_ref, wo_ref, out_ref):
    kvn = kvn_ref[...]
    k = jnp.concatenate([ck_ref[...], kvn[:, :KV_DIM // 2]], axis=0)
    v = jnp.concatenate([cv_ref[...], kvn[:, KV_DIM // 2:]], axis=0)
    win = jnp.concatenate([k, v], axis=1).astype(BF16)
    y = _attn_chunk(q_ref[...], win, 0, sink_ref, wo_ref)
    out_ref[...] = h_ref[...] + y


def _attn_sample(sinks, q, ck, cv, kv, h, wo, batch):
    t = h.shape[0]
    row = lambda b: (b, 0)
    const2 = lambda b: (0, 0)
    return pl.pallas_call(
        _attn_sample_kernel,
        grid=(batch,),
        in_specs=[
            pl.BlockSpec((1, N_HEADS), const2, memory_space=pltpu.SMEM),
            pl.BlockSpec((CHUNK, D_MODEL), row),
            pl.BlockSpec((WINDOW, KV_DIM // 2), row),
            pl.BlockSpec((WINDOW, KV_DIM // 2), row),
            pl.BlockSpec((CHUNK, KV_DIM), row),
            pl.BlockSpec((CHUNK, D_MODEL), row),
            pl.BlockSpec((D_MODEL, D_MODEL), const2),
        ],
        out_specs=pl.BlockSpec((CHUNK, D_MODEL), row),
        out_shape=jax.ShapeDtypeStruct((t, D_MODEL), F32),
        compiler_params=_cparams(1),
        name="attn_sample",
    )(sinks, q, ck, cv, kv, h, wo)


def kernel(x_prompt, x_sample, cache_k_win, cache_v_win, norm_mix, norm_ffn, gmlp_w_in, gmlp_norm_v, gmlp_w_s, gmlp_b_s, gmlp_w_out, norm_kv, w_kv, attn_w_q, attn_sinks, attn_w_o, peer_w_q, peer_sub_keys, peer_u, peer_v, norm_final):
    batch, seq, _ = x_prompt.shape
    dec_batch, dec_seq, _ = x_sample.shape
    w_cache = cache_k_win.shape[1]
    assert dec_seq == CHUNK and w_cache == WINDOW and seq % MLP_CHUNK == 0

    row2 = lambda a: a.reshape(1, -1)
    win = gmlp_w_in[0].astype(BF16)
    wout = gmlp_w_out[0].astype(BF16)
    wkv = w_kv.astype(BF16)
    wq_attn = attn_w_q[0].astype(BF16)
    wo = attn_w_o[0].astype(BF16)
    wq_peer = peer_w_q.astype(BF16)
    sub_keys = peer_sub_keys.reshape(2, 2 * PEER_HEADS, N_KEYS, D_HALF).astype(BF16)
    tables = [
        jnp.concatenate([peer_u[l].reshape(N_EXPERTS, D_MODEL // LANES, LANES),
                         peer_v[l].reshape(N_EXPERTS, D_MODEL // LANES, LANES)], axis=1)
        for l in range(2)
    ]

    pos = jnp.arange(MLP_CHUNK)
    blk_i = pos[:, None] // CHUNK
    blk_j = pos[None, :] // CHUNK
    ws = gmlp_w_s[0]
    bs = gmlp_b_s[0]
    wmix_p = jnp.where((blk_j <= blk_i)[None], ws, 0.0).astype(BF16)
    ws64 = ws[:, :CHUNK, :CHUNK]
    wmix_s = jnp.where((blk_j == blk_i)[None], jnp.tile(ws64, (1, 2, 2)), 0.0).astype(BF16)
    bias_p = jnp.repeat(bs.T, LANES, axis=1)
    bias_s = jnp.repeat(jnp.tile(bs[:, :CHUNK], (1, 2)).T, LANES, axis=1)

    def trunk(x, prompt):
        t = x.shape[0]
        tm = 256
        tt = 128
        h, v_rows = _gmlp(x, row2(norm_mix[0]), win, row2(gmlp_norm_v[0]),
                          wmix_p if prompt else wmix_s, bias_p if prompt else bias_s, wout, tm)
        idx, gate = _route(h, row2(norm_ffn[0]), wq_peer[0], sub_keys[0], tm)
        h = _gather(idx, gate, h, row2(norm_ffn[0]), row2(norm_final), tables[0], tt, False)
        kv, q = _proj(h, row2(norm_kv), row2(norm_mix[1]), wkv, wq_attn, tm)
        if prompt:
            h = _attn_prompt(attn_sinks, q, kv, h, wo, batch, seq, 256)
        else:
            ck = cache_k_win.reshape(dec_batch * w_cache, KV_DIM // 2)
            cv = cache_v_win.reshape(dec_batch * w_cache, KV_DIM // 2)
            h = _attn_sample(attn_sinks, q, ck, cv, kv, h, wo, dec_batch)
        idx, gate = _route(h, row2(norm_ffn[1]), wq_peer[1], sub_keys[1], tm)
        y = _gather(idx, gate, h, row2(norm_ffn[1]), row2(norm_final), tables[1], tt, True)
        return y, kv, v_rows

    y_p, kv_p, _ = trunk(x_prompt.reshape(batch * seq, D_MODEL), True)
    y_s, kv_s, v_s = trunk(x_sample.reshape(dec_batch * dec_seq, D_MODEL), False)

    half = KV_DIM // 2
    kv_p = kv_p.reshape(batch, seq, KV_DIM)[:, seq - WINDOW:]
    prompt_k_win = kv_p[..., :half].reshape(batch, WINDOW, N_KV_HEADS, HEAD_DIM)
    prompt_v_win = kv_p[..., half:].reshape(batch, WINDOW, N_KV_HEADS, HEAD_DIM)
    kv_s = kv_s.reshape(dec_batch, dec_seq, KV_DIM)
    k_s = kv_s[..., :half].reshape(dec_batch, dec_seq, N_KV_HEADS, HEAD_DIM)
    v_s_new = kv_s[..., half:].reshape(dec_batch, dec_seq, N_KV_HEADS, HEAD_DIM)
    sample_k_win = jnp.concatenate([cache_k_win, k_s], axis=1)[:, -w_cache:]
    sample_v_win = jnp.concatenate([cache_v_win, v_s_new], axis=1)[:, -w_cache:]
    sample_gmlp_v = v_s.reshape(1, dec_batch, dec_seq, D_GATE)

    return (y_p.reshape(batch, seq, D_MODEL), y_s.reshape(dec_batch, dec_seq, D_MODEL),
            prompt_k_win, prompt_v_win, sample_k_win, sample_v_win, sample_gmlp_v)
```

```python
import functools
import math

import jax
import jax.numpy as jnp
from jax import lax
from jax.experimental import pallas as pl
from jax.experimental.pallas import tpu as pltpu

F32 = jnp.float32
BF16 = jnp.bfloat16

D_MODEL = 1024
CHUNK = 64
MLP_CHUNK = 128
GMLP_GROUPS = 8
D_GATE = D_MODEL
N_HEADS = 16
N_KV_HEADS = 4
HEAD_DIM = 64
GQA_GROUP = N_HEADS // N_KV_HEADS
WINDOW = 128
PEER_HEADS = 8
N_KEYS = 128
N_EXPERTS = N_KEYS * N_KEYS
PEER_TOPK = 16
D_HALF = 128
N_PICKS = PEER_HEADS * PEER_TOPK
EPS = 1e-6
NEG = -1e30
KV_DIM = 2 * N_KV_HEADS * HEAD_DIM

LANES = 128
SUBLANES = 8
ROWS_PER_EXPERT = 2 * D_MODEL // LANES
PICK_PITCH = 20
SLOT_ROWS = N_PICKS * PICK_PITCH
N_SLOTS = SUBLANES
PREFETCH_DIST = 6

VMEM_LIMIT = 48 * 1024 * 1024


def _rms(x, g):
    return x * lax.rsqrt(jnp.mean(x * x, axis=-1, keepdims=True) + EPS) * g


def _cparams(n_axes):
    return pltpu.CompilerParams(
        dimension_semantics=("arbitrary",) * n_axes, vmem_limit_bytes=VMEM_LIMIT)


def _gmlp_kernel(x_ref, gmix_ref, win_ref, gv_ref, wmix_ref, bias_ref, wout_ref,
                 h_ref, v_ref, *, tm):
    x = x_ref[...]
    hn = _rms(x, gmix_ref[...])
    hmid = jax.nn.gelu(jnp.dot(hn.astype(BF16), win_ref[...], preferred_element_type=F32))
    u = hmid[:, :D_GATE]
    v = _rms(hmid[:, D_GATE:], gv_ref[...])
    v_ref[...] = v
    vb = v.astype(BF16)
    bias = bias_ref[...]
    outs = []
    for c in range(tm // MLP_CHUNK):
        r0 = c * MLP_CHUNK
        cols = []
        for g in range(GMLP_GROUPS):
            c0 = g * LANES
            cols.append(jnp.dot(wmix_ref[g], vb[r0:r0 + MLP_CHUNK, c0:c0 + LANES],
                                preferred_element_type=F32))
        mixed = jnp.concatenate(cols, axis=1) + bias
        outs.append(u[r0:r0 + MLP_CHUNK, :] * mixed)
    out = outs[0] if len(outs) == 1 else jnp.concatenate(outs, axis=0)
    y = jnp.dot(out.astype(BF16), wout_ref[...], preferred_element_type=F32)
    h_ref[...] = x + y


def _gmlp(x, gmix, win, gv, wmix, bias, wout, tm):
    t = x.shape[0]
    row = lambda i: (i, 0)
    const2 = lambda i: (0, 0)
    return pl.pallas_call(
        functools.partial(_gmlp_kernel, tm=tm),
        grid=(t // tm,),
        in_specs=[
            pl.BlockSpec((tm, D_MODEL), row),
            pl.BlockSpec((1, D_MODEL), const2),
            pl.BlockSpec((D_MODEL, 2 * D_GATE), const2),
            pl.BlockSpec((1, D_GATE), const2),
            pl.BlockSpec((GMLP_GROUPS, MLP_CHUNK, MLP_CHUNK), lambda i: (0, 0, 0)),
            pl.BlockSpec((MLP_CHUNK, D_GATE), const2),
            pl.BlockSpec((D_GATE, D_MODEL), const2),
        ],
        out_specs=[pl.BlockSpec((tm, D_MODEL), row), pl.BlockSpec((tm, D_GATE), row)],
        out_shape=[jax.ShapeDtypeStruct((t, D_MODEL), F32),
                   jax.ShapeDtypeStruct((t, D_GATE), F32)],
        compiler_params=_cparams(1),
        name="gmlp_mixer",
    )(x, gmix, win, gv, wmix, bias, wout)


CAND_COUNT = [PEER_TOPK // (a + 1) for a in range(PEER_TOPK)]
CAND_OFFSET = [sum(CAND_COUNT[:a]) for a in range(PEER_TOPK)]
CAND_USED = sum(CAND_COUNT)
CAND_ROWS = -(-CAND_USED // SUBLANES) * SUBLANES


def _route_kernel(h_ref, g_ref, wq_ref, sk_ref, idx_ref, gate_ref,
                  sc_s, ts_s, ti_s, cand_s, best_s, sel_s, e_s, gt_s, *, tm):
    hf = _rms(h_ref[...], g_ref[...])
    q = jnp.dot(hf.astype(BF16), wq_ref[...], preferred_element_type=F32).astype(BF16)
    n_hp = 2 * PEER_HEADS
    for hp in range(n_hp):
        sc_s[hp] = lax.dot_general(sk_ref[hp], q[:, hp * D_HALF:(hp + 1) * D_HALF],
                                   (((1,), (1,)), ((), ())), preferred_element_type=F32)

    key_id = lax.broadcasted_iota(jnp.int32, (N_KEYS, LANES), 0).astype(F32)

    def stage1(hp, carry):
        for lg in range(tm // LANES):
            l0 = lg * LANES
            s = sc_s[hp, :, l0:l0 + LANES]
            for k in range(PEER_TOPK):
                m = jnp.max(s, axis=0, keepdims=True)
                i = jnp.min(jnp.where(s == m, key_id, float(N_KEYS)), axis=0, keepdims=True)
                ts_s[hp, k:k + 1, l0:l0 + LANES] = m
                ti_s[hp, k:k + 1, l0:l0 + LANES] = i
                s = jnp.where(key_id == i, -jnp.inf, s)
        return carry

    lax.fori_loop(0, n_hp, stage1, 0)

    n_cand = float(PEER_TOPK * PEER_TOPK)
    slot = lax.broadcasted_iota(jnp.int32, (CAND_ROWS, LANES), 0)
    cand_flat = slot
    for a in range(1, PEER_TOPK):
        step = (a * PEER_TOPK - CAND_OFFSET[a]) - ((a - 1) * PEER_TOPK - CAND_OFFSET[a - 1])
        cand_flat = cand_flat + jnp.where(slot >= CAND_OFFSET[a], step, 0)
    cand_flat = jnp.where(slot >= CAND_USED, PEER_TOPK * PEER_TOPK, cand_flat).astype(F32)

    def stage2(hd, carry):
        for lg in range(tm // LANES):
            l0 = lg * LANES
            s0 = ts_s[2 * hd, :, l0:l0 + LANES]
            s1 = ts_s[2 * hd + 1, :, l0:l0 + LANES]
            i0 = ti_s[2 * hd, :, l0:l0 + LANES]
            i1 = ti_s[2 * hd + 1, :, l0:l0 + LANES]
            for a in range(PEER_TOPK):
                n = CAND_COUNT[a]
                cand_s[CAND_OFFSET[a]:CAND_OFFSET[a] + n, :] = s0[a:a + 1, :] + s1[0:n, :]
            cand_s[CAND_USED:CAND_ROWS, :] = jnp.full((CAND_ROWS - CAND_USED, LANES), -jnp.inf, F32)
            cand = cand_s[...]
            for k in range(PEER_TOPK):
                m = jnp.max(cand, axis=0, keepdims=True)
                c = jnp.min(jnp.where(cand == m, cand_flat, n_cand), axis=0, keepdims=True)
                best_s[k:k + 1, :] = m
                sel_s[k:k + 1, :] = c
                cand = jnp.where(cand_flat == c, -jnp.inf, cand)
            best = best_s[...]
            sel = sel_s[...]
            sel_a = jnp.floor(sel * (1.0 / PEER_TOPK))
            sel_b = sel - sel_a * PEER_TOPK
            e0 = jnp.zeros_like(sel)
            e1 = jnp.zeros_like(sel)
            for a in range(PEER_TOPK):
                e0 = jnp.where(sel_a == float(a), i0[a:a + 1, :], e0)
                e1 = jnp.where(sel_b == float(a), i1[a:a + 1, :], e1)
            ex = jnp.exp(best - best[0:1, :])
            gate = ex / jnp.sum(ex, axis=0, keepdims=True)
            r0 = pl.multiple_of(hd * PEER_TOPK, PEER_TOPK)
            e_s[pl.ds(r0, PEER_TOPK), l0:l0 + LANES] = e0 * float(N_KEYS) + e1
            gt_s[pl.ds(r0, PEER_TOPK), l0:l0 + LANES] = gate
        return carry

    lax.fori_loop(0, PEER_HEADS, stage2, 0)

    idx_ref[...] = jnp.transpose(e_s[...]).astype(jnp.int32)
    gate_ref[...] = jnp.transpose(gt_s[...])


def _route(h, g, wq, sk, tm):
    t = h.shape[0]
    row = lambda i: (i, 0)
    const2 = lambda i: (0, 0)
    n_hp = 2 * PEER_HEADS
    return pl.pallas_call(
        functools.partial(_route_kernel, tm=tm),
        grid=(t // tm,),
        in_specs=[
            pl.BlockSpec((tm, D_MODEL), row),
            pl.BlockSpec((1, D_MODEL), const2),
            pl.BlockSpec((D_MODEL, n_hp * D_HALF), const2),
            pl.BlockSpec((n_hp, N_KEYS, D_HALF), lambda i: (0, 0, 0)),
        ],
        out_specs=[pl.BlockSpec((tm, N_PICKS), row), pl.BlockSpec((tm, N_PICKS), row)],
        out_shape=[jax.ShapeDtypeStruct((t, N_PICKS), jnp.int32),
                   jax.ShapeDtypeStruct((t, N_PICKS), F32)],
        scratch_shapes=[
            pltpu.VMEM((n_hp, N_KEYS, tm), F32),
            pltpu.VMEM((n_hp, PEER_TOPK, tm), F32),
            pltpu.VMEM((n_hp, PEER_TOPK, tm), F32),
            pltpu.VMEM((CAND_ROWS, LANES), F32),
            pltpu.VMEM((PEER_TOPK, LANES), F32),
            pltpu.VMEM((PEER_TOPK, LANES), F32),
            pltpu.VMEM((N_PICKS, tm), F32),
            pltpu.VMEM((N_PICKS, tm), F32),
        ],
        compiler_params=_cparams(1),
        name="peer_route",
    )(h, g, wq, sk)


def _gather_kernel(idx_ref, gate_ref, h_ref, g_ref, gfin_ref, tab_ref, out_ref,
                   hf_s, buf, sem, *, tt, final_norm):
    hf_s[...] = _rms(h_ref[...], g_ref[...])
    n_sub = D_MODEL // LANES
    n_groups = tt // N_SLOTS
    n_pieces = 2 * n_sub
    picks_per_piece = N_PICKS // n_pieces

    def pick_copy(e, slot, r):
        dst = buf.at[pl.ds(slot * SLOT_ROWS + r * PICK_PITCH, ROWS_PER_EXPERT), :]
        return pltpu.make_async_copy(tab_ref.at[e], dst, sem.at[slot])

    def issue(j, slot, r_lo, r_hi):
        for r in range(r_lo, r_hi):
            pick_copy(idx_ref[j, r], slot, r).start(priority=r % 2)

    def wait(slot):
        for r in range(N_PICKS):
            pick_copy(0, slot, r).wait()

    eye = (lax.broadcasted_iota(jnp.int32, (N_PICKS, N_PICKS), 0)
           == lax.broadcasted_iota(jnp.int32, (N_PICKS, N_PICKS), 1))
    sub = lax.broadcasted_iota(jnp.int32, (SUBLANES, LANES), 0)

    def token(k, x8, gate8, tiles, prefetch):
        base = k * SLOT_ROWS
        wait(k)
        piece = 0
        acc = jnp.zeros((N_PICKS, LANES), F32)
        for s in range(n_sub):
            u_s = buf[pl.ds(base + s, N_PICKS, stride=PICK_PITCH), :]
            acc = acc + u_s * x8[s][k:k + 1, :]
            prefetch(piece)
            piece += 1
        act = jax.nn.gelu(jnp.sum(acc, axis=1, keepdims=True))
        gcol = jnp.sum(jnp.where(eye, gate8[k:k + 1, :], 0.0), axis=1, keepdims=True)
        w = gcol * act
        new_tiles = []
        for s in range(n_sub):
            v_s = buf[pl.ds(base + n_sub + s, N_PICKS, stride=PICK_PITCH), :]
            o_s = jnp.sum(v_s * w, axis=0, keepdims=True)
            new_tiles.append(jnp.where(sub == k, o_s, tiles[s]))
            prefetch(piece)
            piece += 1
        return new_tiles

    def group(gi, last):
        r0 = pl.multiple_of(gi * N_SLOTS, N_SLOTS)
        x8 = [hf_s[pl.ds(r0, N_SLOTS), s * LANES:(s + 1) * LANES] for s in range(n_sub)]
        gate8 = gate_ref[pl.ds(r0, N_SLOTS), :]
        tiles = [jnp.zeros((SUBLANES, LANES), F32) for _ in range(n_sub)]
        for k in range(N_SLOTS):
            ahead = k + PREFETCH_DIST
            if last and ahead >= N_SLOTS:
                prefetch = lambda piece: None
            else:
                def prefetch(piece, ahead=ahead):
                    issue(r0 + ahead, ahead % N_SLOTS,
                          piece * picks_per_piece, (piece + 1) * picks_per_piece)
            tiles = token(k, x8, gate8, tiles, prefetch)
        res = [h_ref[pl.ds(r0, N_SLOTS), s * LANES:(s + 1) * LANES] + tiles[s] for s in range(n_sub)]
        if final_norm:
            ss = res[0] * res[0]
            for s in range(1, n_sub):
                ss = ss + res[s] * res[s]
            scale = lax.rsqrt(jnp.sum(ss, axis=1, keepdims=True) * (1.0 / D_MODEL) + EPS)
            res = [res[s] * scale * gfin_ref[:, s * LANES:(s + 1) * LANES] for s in range(n_sub)]
        for s in range(n_sub):
            out_ref[pl.ds(r0, N_SLOTS), s * LANES:(s + 1) * LANES] = res[s]

    for j in range(PREFETCH_DIST):
        issue(j, j, 0, N_PICKS)

    def body(gi, carry):
        group(gi, False)
        return carry

    lax.fori_loop(0, n_groups - 1, body, 0)
    group(n_groups - 1, True)


def _gather(idx, gate, h, g, gfin, tab, tt, final_norm):
    t = h.shape[0]
    row = lambda i: (i, 0)
    const2 = lambda i: (0, 0)
    return pl.pallas_call(
        functools.partial(_gather_kernel, tt=tt, final_norm=final_norm),
        grid=(t // tt,),
        in_specs=[
            pl.BlockSpec((tt, N_PICKS), row, memory_space=pltpu.SMEM),
            pl.BlockSpec((tt, N_PICKS), row),
            pl.BlockSpec((tt, D_MODEL), row),
            pl.BlockSpec((1, D_MODEL), const2),
            pl.BlockSpec((1, D_MODEL), const2),
            pl.BlockSpec(memory_space=pl.ANY),
        ],
        out_specs=pl.BlockSpec((tt, D_MODEL), row),
        out_shape=jax.ShapeDtypeStruct((t, D_MODEL), F32),
        scratch_shapes=[
            pltpu.VMEM((tt, D_MODEL), F32),
            pltpu.VMEM((N_SLOTS * SLOT_ROWS, LANES), F32),
            pltpu.SemaphoreType.DMA((N_SLOTS,)),
        ],
        compiler_params=_cparams(1),
        name="peer_experts",
    )(idx, gate, h, g, gfin, tab)


def _proj_kernel(h_ref, gkv_ref, gq_ref, wkv_ref, wq_ref, kv_ref, q_ref):
    h = h_ref[...]
    n = h * lax.rsqrt(jnp.mean(h * h, axis=-1, keepdims=True) + EPS)
    kv_ref[...] = jnp.dot((n * gkv_ref[...]).astype(BF16), wkv_ref[...], preferred_element_type=F32)
    q_ref[...] = jnp.dot((n * gq_ref[...]).astype(BF16), wq_ref[...],
                         preferred_element_type=F32).astype(BF16)


def _proj(h, gkv, gq, wkv, wq, tm):
    t = h.shape[0]
    row = lambda i: (i, 0)
    const2 = lambda i: (0, 0)
    return pl.pallas_call(
        _proj_kernel,
        grid=(t // tm,),
        in_specs=[
            pl.BlockSpec((tm, D_MODEL), row),
            pl.BlockSpec((1, D_MODEL), const2),
            pl.BlockSpec((1, D_MODEL), const2),
            pl.BlockSpec((D_MODEL, KV_DIM), const2),
            pl.BlockSpec((D_MODEL, D_MODEL), const2),
        ],
        out_specs=[pl.BlockSpec((tm, KV_DIM), row), pl.BlockSpec((tm, D_MODEL), row)],
        out_shape=[jax.ShapeDtypeStruct((t, KV_DIM), F32),
                   jax.ShapeDtypeStruct((t, D_MODEL), BF16)],
        compiler_params=_cparams(1),
        name="kv_q_proj",
    )(h, gkv, gq, wkv, wq)


def _alibi_slopes():
    return [2.0 ** (-8.0 * (h + 1) / N_HEADS) for h in range(N_HEADS)]


def _attn_chunk(q, win, first_valid, sink_ref):
    n_keys = WINDOW + CHUNK
    pair = 2 * CHUNK
    row = lax.broadcasted_iota(jnp.int32, (pair, n_keys), 0)
    kj = lax.broadcasted_iota(jnp.int32, (pair, n_keys), 1)
    dist = jnp.abs(WINDOW + (row & (CHUNK - 1)) - kj).astype(F32)
    valid = kj >= first_valid
    upper = lax.broadcasted_iota(jnp.int32, (pair, 1), 0) >= CHUNK
    low = lax.broadcasted_iota(jnp.int32, (n_keys, LANES), 1) < HEAD_DIM
    slopes = _alibi_slopes()
    half_kv = KV_DIM // 2
    o_tiles = [None] * (N_HEADS // 2)
    for g in range(N_KV_HEADS):
        t = g // 2
        ktile = win[:, t * LANES:(t + 1) * LANES]
        vtile = win[:, half_kv + t * LANES:half_kv + (t + 1) * LANES]
        kroll = pltpu.roll(ktile, HEAD_DIM, axis=1)
        vroll = pltpu.roll(vtile, HEAD_DIM, axis=1)
        if g % 2 == 0:
            k_pad = [jnp.where(low, ktile, 0.0), jnp.where(low, 0.0, kroll)]
            v_pad = [jnp.where(low, vtile, 0.0), jnp.where(low, 0.0, vroll)]
        else:
            k_pad = [jnp.where(low, kroll, 0.0), jnp.where(low, 0.0, ktile)]
            v_pad = [jnp.where(low, vroll, 0.0), jnp.where(low, 0.0, vtile)]
        qstack = jnp.concatenate([q[:, (2 * g) * LANES:(2 * g + 1) * LANES],
                                  q[:, (2 * g + 1) * LANES:(2 * g + 2) * LANES]], axis=0)
        o_stack = jnp.zeros((pair, LANES), F32)
        for half in range(2):
            h_lo = GQA_GROUP * g + half
            h_up = h_lo + 2
            s = lax.dot_general(qstack, k_pad[half].astype(BF16), (((1,), (1,)), ((), ())),
                                preferred_element_type=F32)
            slope = jnp.where(upper, slopes[h_up], slopes[h_lo])
            s = s * (HEAD_DIM ** -0.5) - slope * dist
            s = jnp.where(valid, s, NEG)
            sink = jnp.where(upper, sink_ref[0, h_up], sink_ref[0, h_lo])
            m = jnp.maximum(jnp.max(s, axis=1, keepdims=True), sink)
            e = jnp.exp(s - m)
            denom = jnp.sum(e, axis=1, keepdims=True) + jnp.exp(sink - m)
            p = e / denom
            o_stack = o_stack + jnp.dot(p.astype(BF16), v_pad[half].astype(BF16),
                                        preferred_element_type=F32)
        o_tiles[2 * g] = o_stack[:CHUNK]
        o_tiles[2 * g + 1] = o_stack[CHUNK:]
    return jnp.concatenate(o_tiles, axis=1)


def _attn_prompt_kernel(sink_ref, q_ref, kvp_ref, kvc_ref, h_ref, wo_ref, out_ref, *, qt):
    t = pl.program_id(1)
    kv = jnp.concatenate([kvp_ref[...], kvc_ref[...]], axis=0)
    n_chunks = qt // CHUNK
    outs = []
    for c in range(n_chunks):
        r0 = c * CHUNK
        first_valid = WINDOW - CHUNK * (t * n_chunks + c)
        outs.append(_attn_chunk(q_ref[r0:r0 + CHUNK, :], kv[r0:r0 + WINDOW + CHUNK, :],
                                first_valid, sink_ref))
    o = jnp.concatenate(outs, axis=0).astype(BF16)
    out_ref[...] = h_ref[...] + jnp.dot(o, wo_ref[...], preferred_element_type=F32)


def _attn_prompt(sinks, q, kv, h, wo, batch, seq, qt):
    t = h.shape[0]
    tiles = seq // qt
    cur = lambda b, i: (b * tiles + i, 0)
    prev = lambda b, i: (jnp.maximum((b * tiles + i) * (qt // WINDOW) - 1, 0), 0)
    const2 = lambda b, i: (0, 0)
    return pl.pallas_call(
        functools.partial(_attn_prompt_kernel, qt=qt),
        grid=(batch, tiles),
        in_specs=[
            pl.BlockSpec((1, N_HEADS), const2, memory_space=pltpu.SMEM),
            pl.BlockSpec((qt, D_MODEL), cur),
            pl.BlockSpec((WINDOW, KV_DIM), prev),
            pl.BlockSpec((qt, KV_DIM), cur),
            pl.BlockSpec((qt, D_MODEL), cur),
            pl.BlockSpec((D_MODEL, D_MODEL), const2),
        ],
        out_specs=pl.BlockSpec((qt, D_MODEL), cur),
        out_shape=jax.ShapeDtypeStruct((t, D_MODEL), F32),
        compiler_params=_cparams(2),
        name="attn_prompt",
    )(sinks, q, kv, kv, h, wo)


def _attn_sample_kernel(sink_ref, q_ref, ck_ref, cv_ref, kvn_ref, h_ref, wo_ref, out_ref, *, nb):
    half_kv = KV_DIM // 2
    outs = []
    for b in range(nb):
        kvn = kvn_ref[b * CHUNK:(b + 1) * CHUNK, :]
        k = jnp.concatenate([ck_ref[b * WINDOW:(b + 1) * WINDOW, :], kvn[:, :half_kv]], axis=0)
        v = jnp.concatenate([cv_ref[b * WINDOW:(b + 1) * WINDOW, :], kvn[:, half_kv:]], axis=0)
        win = jnp.concatenate([k, v], axis=1)
        outs.append(_attn_chunk(q_ref[b * CHUNK:(b + 1) * CHUNK, :], win, 0, sink_ref))
    o = jnp.concatenate(outs, axis=0).astype(BF16)
    out_ref[...] = h_ref[...] + jnp.dot(o, wo_ref[...], preferred_element_type=F32)


def _attn_sample(sinks, q, ck, cv, kv, h, wo, batch, nb):
    t = h.shape[0]
    row = lambda b: (b, 0)
    const2 = lambda b: (0, 0)
    return pl.pallas_call(
        functools.partial(_attn_sample_kernel, nb=nb),
        grid=(batch // nb,),
        in_specs=[
            pl.BlockSpec((1, N_HEADS), const2, memory_space=pltpu.SMEM),
            pl.BlockSpec((nb * CHUNK, D_MODEL), row),
            pl.BlockSpec((nb * WINDOW, KV_DIM // 2), row),
            pl.BlockSpec((nb * WINDOW, KV_DIM // 2), row),
            pl.BlockSpec((nb * CHUNK, KV_DIM), row),
            pl.BlockSpec((nb * CHUNK, D_MODEL), row),
            pl.BlockSpec((D_MODEL, D_MODEL), const2),
        ],
        out_specs=pl.BlockSpec((nb * CHUNK, D_MODEL), row),
        out_shape=jax.ShapeDtypeStruct((t, D_MODEL), F32),
        compiler_params=_cparams(1),
        name="attn_sample",
    )(sinks, q, ck, cv, kv, h, wo)


def kernel(x_prompt, x_sample, cache_k_win, cache_v_win, norm_mix, norm_ffn, gmlp_w_in, gmlp_norm_v, gmlp_w_s, gmlp_b_s, gmlp_w_out, norm_kv, w_kv, attn_w_q, attn_sinks, attn_w_o, peer_w_q, peer_sub_keys, peer_u, peer_v, norm_final):
    batch, seq, _ = x_prompt.shape
    dec_batch, dec_seq, _ = x_sample.shape
    w_cache = cache_k_win.shape[1]
    assert dec_seq == CHUNK and w_cache == WINDOW and seq % MLP_CHUNK == 0

    row2 = lambda a: a.reshape(1, -1)
    win = gmlp_w_in[0].astype(BF16)
    wout = gmlp_w_out[0].astype(BF16)
    wkv = w_kv.astype(BF16)
    wq_attn = attn_w_q[0].astype(BF16)
    wo = attn_w_o[0].astype(BF16)
    wq_peer = peer_w_q.astype(BF16)
    sub_keys = peer_sub_keys.reshape(2, 2 * PEER_HEADS, N_KEYS, D_HALF).astype(BF16)
    tables = [
        jnp.concatenate([peer_u[l].reshape(N_EXPERTS, D_MODEL // LANES, LANES),
                         peer_v[l].reshape(N_EXPERTS, D_MODEL // LANES, LANES)], axis=1)
        for l in range(2)
    ]

    pos = jnp.arange(MLP_CHUNK)
    blk_i = pos[:, None] // CHUNK
    blk_j = pos[None, :] // CHUNK
    ws = gmlp_w_s[0]
    bs = gmlp_b_s[0]
    wmix_p = jnp.where((blk_j <= blk_i)[None], ws, 0.0).astype(BF16)
    ws64 = ws[:, :CHUNK, :CHUNK]
    wmix_s = jnp.where((blk_j == blk_i)[None], jnp.tile(ws64, (1, 2, 2)), 0.0).astype(BF16)
    bias_p = jnp.repeat(bs.T, LANES, axis=1)
    bias_s = jnp.repeat(jnp.tile(bs[:, :CHUNK], (1, 2)).T, LANES, axis=1)

    def trunk(x, prompt):
        t = x.shape[0]
        tm = 256
        tm_route = 512
        tt = 128
        h, v_rows = _gmlp(x, row2(norm_mix[0]), win, row2(gmlp_norm_v[0]),
                          wmix_p if prompt else wmix_s, bias_p if prompt else bias_s, wout, tm)
        idx, gate = _route(h, row2(norm_ffn[0]), wq_peer[0], sub_keys[0], tm_route)
        h = _gather(idx, gate, h, row2(norm_ffn[0]), row2(norm_final), tables[0], tt, False)
        kv, q = _proj(h, row2(norm_kv), row2(norm_mix[1]), wkv, wq_attn, tm)
        if prompt:
            h = _attn_prompt(attn_sinks, q, kv, h, wo, batch, seq, 256)
        else:
            ck = cache_k_win.reshape(dec_batch * w_cache, KV_DIM // 2)
            cv = cache_v_win.reshape(dec_batch * w_cache, KV_DIM // 2)
            h = _attn_sample(attn_sinks, q, ck, cv, kv, h, wo, dec_batch, 4)
        idx, gate = _route(h, row2(norm_ffn[1]), wq_peer[1], sub_keys[1], tm_route)
        y = _gather(idx, gate, h, row2(norm_ffn[1]), row2(norm_final), tables[1], tt, True)
        return y, kv, v_rows

    y_p, kv_p, _ = trunk(x_prompt.reshape(batch * seq, D_MODEL), True)
    y_s, kv_s, v_s = trunk(x_sample.reshape(dec_batch * dec_seq, D_MODEL), False)

    half = KV_DIM // 2
    kv_p = kv_p.reshape(batch, seq, KV_DIM)[:, seq - WINDOW:]
    prompt_k_win = kv_p[..., :half].reshape(batch, WINDOW, N_KV_HEADS, HEAD_DIM)
    prompt_v_win = kv_p[..., half:].reshape(batch, WINDOW, N_KV_HEADS, HEAD_DIM)
    kv_s = kv_s.reshape(dec_batch, dec_seq, KV_DIM)
    k_s = kv_s[..., :half].reshape(dec_batch, dec_seq, N_KV_HEADS, HEAD_DIM)
    v_s_new = kv_s[..., half:].reshape(dec_batch, dec_seq, N_KV_HEADS, HEAD_DIM)
    sample_k_win = jnp.concatenate([cache_k_win, k_s], axis=1)[:, -w_cache:]
    sample_v_win = jnp.concatenate([cache_v_win, v_s_new], axis=1)[:, -w_cache:]
    sample_gmlp_v = v_s.reshape(1, dec_batch, dec_seq, D_GATE)

    return (y_p.reshape(batch, seq, D_MODEL), y_s.reshape(dec_batch, dec_seq, D_MODEL),
            prompt_k_win, prompt_v_win, sample_k_win, sample_v_win, sample_gmlp_v)
```

```python
import functools
import math

import jax
import jax.numpy as jnp
from jax import lax
from jax.experimental import pallas as pl
from jax.experimental.pallas import tpu as pltpu

F32 = jnp.float32
BF16 = jnp.bfloat16

D_MODEL = 1024
CHUNK = 64
MLP_CHUNK = 128
GMLP_GROUPS = 8
D_GATE = D_MODEL
N_HEADS = 16
N_KV_HEADS = 4
HEAD_DIM = 64
GQA_GROUP = N_HEADS // N_KV_HEADS
WINDOW = 128
PEER_HEADS = 8
N_KEYS = 128
N_EXPERTS = N_KEYS * N_KEYS
PEER_TOPK = 16
D_HALF = 128
N_PICKS = PEER_HEADS * PEER_TOPK
EPS = 1e-6
NEG = -1e30
KV_DIM = 2 * N_KV_HEADS * HEAD_DIM

LANES = 128
SUBLANES = 8
ROWS_PER_EXPERT = 2 * D_MODEL // LANES
PICK_PITCH = 20
SLOT_ROWS = N_PICKS * PICK_PITCH
N_SLOTS = SUBLANES
PREFETCH_DIST = 6

VMEM_LIMIT = 48 * 1024 * 1024


def _rms(x, g):
    return x * lax.rsqrt(jnp.mean(x * x, axis=-1, keepdims=True) + EPS) * g


def _cparams(n_axes):
    return pltpu.CompilerParams(
        dimension_semantics=("arbitrary",) * n_axes, vmem_limit_bytes=VMEM_LIMIT)


def _gmlp_kernel(x_ref, gmix_ref, win_ref, gv_ref, wmix_ref, bias_ref, wout_ref,
                 h_ref, v_ref, *, tm):
    x = x_ref[...]
    hn = _rms(x, gmix_ref[...])
    hmid = jax.nn.gelu(jnp.dot(hn.astype(BF16), win_ref[...], preferred_element_type=F32))
    u = hmid[:, :D_GATE]
    v = _rms(hmid[:, D_GATE:], gv_ref[...])
    v_ref[...] = v
    vb = v.astype(BF16)
    bias = bias_ref[...]
    outs = []
    for c in range(tm // MLP_CHUNK):
        r0 = c * MLP_CHUNK
        cols = []
        for g in range(GMLP_GROUPS):
            c0 = g * LANES
            cols.append(jnp.dot(wmix_ref[g], vb[r0:r0 + MLP_CHUNK, c0:c0 + LANES],
                                preferred_element_type=F32))
        mixed = jnp.concatenate(cols, axis=1) + bias
        outs.append(u[r0:r0 + MLP_CHUNK, :] * mixed)
    out = outs[0] if len(outs) == 1 else jnp.concatenate(outs, axis=0)
    y = jnp.dot(out.astype(BF16), wout_ref[...], preferred_element_type=F32)
    h_ref[...] = x + y


def _gmlp(x, gmix, win, gv, wmix, bias, wout, tm):
    t = x.shape[0]
    row = lambda i: (i, 0)
    const2 = lambda i: (0, 0)
    return pl.pallas_call(
        functools.partial(_gmlp_kernel, tm=tm),
        grid=(t // tm,),
        in_specs=[
            pl.BlockSpec((tm, D_MODEL), row),
            pl.BlockSpec((1, D_MODEL), const2),
            pl.BlockSpec((D_MODEL, 2 * D_GATE), const2),
            pl.BlockSpec((1, D_GATE), const2),
            pl.BlockSpec((GMLP_GROUPS, MLP_CHUNK, MLP_CHUNK), lambda i: (0, 0, 0)),
            pl.BlockSpec((MLP_CHUNK, D_GATE), const2),
            pl.BlockSpec((D_GATE, D_MODEL), const2),
        ],
        out_specs=[pl.BlockSpec((tm, D_MODEL), row), pl.BlockSpec((tm, D_GATE), row)],
        out_shape=[jax.ShapeDtypeStruct((t, D_MODEL), F32),
                   jax.ShapeDtypeStruct((t, D_GATE), F32)],
        compiler_params=_cparams(1),
        name="gmlp_mixer",
    )(x, gmix, win, gv, wmix, bias, wout)


CAND_COUNT = [PEER_TOPK // (a + 1) for a in range(PEER_TOPK)]
CAND_OFFSET = [sum(CAND_COUNT[:a]) for a in range(PEER_TOPK)]
CAND_USED = sum(CAND_COUNT)
CAND_ROWS = -(-CAND_USED // SUBLANES) * SUBLANES


def _route_kernel(h_ref, g_ref, wq_ref, sk_ref, idx_ref, gate_ref,
                  sc_s, ts_s, ti_s, cand_s, best_s, sel_s, e_s, gt_s, *, tm):
    hf = _rms(h_ref[...], g_ref[...])
    q = jnp.dot(hf.astype(BF16), wq_ref[...], preferred_element_type=F32).astype(BF16)
    n_hp = 2 * PEER_HEADS
    for hp in range(n_hp):
        sc_s[hp] = lax.dot_general(sk_ref[hp], q[:, hp * D_HALF:(hp + 1) * D_HALF],
                                   (((1,), (1,)), ((), ())), preferred_element_type=F32)

    n_rows = N_KEYS // SUBLANES
    sub_key = lax.broadcasted_iota(jnp.int32, (SUBLANES, LANES), 0).astype(F32)

    def stage1(hp, carry):
        for lg in range(tm // LANES):
            l0 = lg * LANES
            val = [sc_s[hp, v * SUBLANES:(v + 1) * SUBLANES, l0:l0 + LANES] for v in range(n_rows)]
            key = [sub_key + float(v * SUBLANES) for v in range(n_rows)]
            for rnd in range(n_rows):
                for a in range(rnd % 2, n_rows - 1, 2):
                    swap = val[a + 1] > val[a]
                    val[a], val[a + 1] = (jnp.where(swap, val[a + 1], val[a]),
                                          jnp.where(swap, val[a], val[a + 1]))
                    key[a], key[a + 1] = (jnp.where(swap, key[a + 1], key[a]),
                                          jnp.where(swap, key[a], key[a + 1]))
            for k in range(PEER_TOPK):
                m = jnp.max(val[0], axis=0, keepdims=True)
                i = jnp.min(jnp.where(val[0] == m, key[0], float(N_KEYS)), axis=0, keepdims=True)
                ts_s[hp, k:k + 1, l0:l0 + LANES] = m
                ti_s[hp, k:k + 1, l0:l0 + LANES] = i
                if k + 1 < PEER_TOPK:
                    taken = key[0] == i
                    for p in range(PEER_TOPK - 1 - k):
                        val[p] = jnp.where(taken, val[p + 1], val[p])
                        key[p] = jnp.where(taken, key[p + 1], key[p])
        return carry

    lax.fori_loop(0, n_hp, stage1, 0)

    n_cand = float(PEER_TOPK * PEER_TOPK)
    slot = lax.broadcasted_iota(jnp.int32, (CAND_ROWS, LANES), 0)
    cand_flat = slot
    for a in range(1, PEER_TOPK):
        step = (a * PEER_TOPK - CAND_OFFSET[a]) - ((a - 1) * PEER_TOPK - CAND_OFFSET[a - 1])
        cand_flat = cand_flat + jnp.where(slot >= CAND_OFFSET[a], step, 0)
    cand_flat = jnp.where(slot >= CAND_USED, PEER_TOPK * PEER_TOPK, cand_flat).astype(F32)

    def stage2(hd, carry):
        for lg in range(tm // LANES):
            l0 = lg * LANES
            s0 = ts_s[2 * hd, :, l0:l0 + LANES]
            s1 = ts_s[2 * hd + 1, :, l0:l0 + LANES]
            i0 = ti_s[2 * hd, :, l0:l0 + LANES]
            i1 = ti_s[2 * hd + 1, :, l0:l0 + LANES]
            for a in range(PEER_TOPK):
                n = CAND_COUNT[a]
                cand_s[CAND_OFFSET[a]:CAND_OFFSET[a] + n, :] = s0[a:a + 1, :] + s1[0:n, :]
            cand_s[CAND_USED:CAND_ROWS, :] = jnp.full((CAND_ROWS - CAND_USED, LANES), -jnp.inf, F32)
            cand = cand_s[...]
            for k in range(PEER_TOPK):
                m = jnp.max(cand, axis=0, keepdims=True)
                c = jnp.min(jnp.where(cand == m, cand_flat, n_cand), axis=0, keepdims=True)
                best_s[k:k + 1, :] = m
                sel_s[k:k + 1, :] = c
                cand = jnp.where(cand_flat == c, -jnp.inf, cand)
            best = best_s[...]
            sel = sel_s[...]
            sel_a = jnp.floor(sel * (1.0 / PEER_TOPK))
            sel_b = sel - sel_a * PEER_TOPK
            e0 = jnp.zeros_like(sel)
            e1 = jnp.zeros_like(sel)
            for a in range(PEER_TOPK):
                e0 = jnp.where(sel_a == float(a), i0[a:a + 1, :], e0)
                e1 = jnp.where(sel_b == float(a), i1[a:a + 1, :], e1)
            ex = jnp.exp(best - best[0:1, :])
            gate = ex / jnp.sum(ex, axis=0, keepdims=True)
            r0 = pl.multiple_of(hd * PEER_TOPK, PEER_TOPK)
            e_s[pl.ds(r0, PEER_TOPK), l0:l0 + LANES] = e0 * float(N_KEYS) + e1
            gt_s[pl.ds(r0, PEER_TOPK), l0:l0 + LANES] = gate
        return carry

    lax.fori_loop(0, PEER_HEADS, stage2, 0)

    idx_ref[...] = jnp.transpose(e_s[...]).astype(jnp.int32)
    gate_ref[...] = jnp.transpose(gt_s[...])


def _route(h, g, wq, sk, tm):
    t = h.shape[0]
    row = lambda i: (i, 0)
    const2 = lambda i: (0, 0)
    n_hp = 2 * PEER_HEADS
    return pl.pallas_call(
        functools.partial(_route_kernel, tm=tm),
        grid=(t // tm,),
        in_specs=[
            pl.BlockSpec((tm, D_MODEL), row),
            pl.BlockSpec((1, D_MODEL), const2),
            pl.BlockSpec((D_MODEL, n_hp * D_HALF), const2),
            pl.BlockSpec((n_hp, N_KEYS, D_HALF), lambda i: (0, 0, 0)),
        ],
        out_specs=[pl.BlockSpec((tm, N_PICKS), row), pl.BlockSpec((tm, N_PICKS), row)],
        out_shape=[jax.ShapeDtypeStruct((t, N_PICKS), jnp.int32),
                   jax.ShapeDtypeStruct((t, N_PICKS), F32)],
        scratch_shapes=[
            pltpu.VMEM((n_hp, N_KEYS, tm), F32),
            pltpu.VMEM((n_hp, PEER_TOPK, tm), F32),
            pltpu.VMEM((n_hp, PEER_TOPK, tm), F32),
            pltpu.VMEM((CAND_ROWS, LANES), F32),
            pltpu.VMEM((PEER_TOPK, LANES), F32),
            pltpu.VMEM((PEER_TOPK, LANES), F32),
            pltpu.VMEM((N_PICKS, tm), F32),
            pltpu.VMEM((N_PICKS, tm), F32),
        ],
        compiler_params=_cparams(1),
        name="peer_route",
    )(h, g, wq, sk)


def _gather_kernel(idx_ref, gate_ref, h_ref, g_ref, gfin_ref, tab_ref, out_ref,
                   hf_s, buf, sem, *, tt, final_norm):
    hf_s[...] = _rms(h_ref[...], g_ref[...])
    n_sub = D_MODEL // LANES
    n_groups = tt // N_SLOTS
    n_pieces = 2 * n_sub
    picks_per_piece = N_PICKS // n_pieces

    def pick_copy(e, slot, r):
        dst = buf.at[pl.ds(slot * SLOT_ROWS + r * PICK_PITCH, ROWS_PER_EXPERT), :]
        return pltpu.make_async_copy(tab_ref.at[e], dst, sem.at[slot])

    def issue(j, slot, r_lo, r_hi):
        for r in range(r_lo, r_hi):
            pick_copy(idx_ref[j, r], slot, r).start(priority=r % 2)

    def wait(slot):
        for r in range(N_PICKS):
            pick_copy(0, slot, r).wait()

    eye = (lax.broadcasted_iota(jnp.int32, (N_PICKS, N_PICKS), 0)
           == lax.broadcasted_iota(jnp.int32, (N_PICKS, N_PICKS), 1))
    sub = lax.broadcasted_iota(jnp.int32, (SUBLANES, LANES), 0)

    def token(k, x8, gate8, tiles, prefetch):
        base = k * SLOT_ROWS
        wait(k)
        piece = 0
        acc = jnp.zeros((N_PICKS, LANES), F32)
        for s in range(n_sub):
            u_s = buf[pl.ds(base + s, N_PICKS, stride=PICK_PITCH), :]
            acc = acc + u_s * x8[s][k:k + 1, :]
            prefetch(piece)
            piece += 1
        act = jax.nn.gelu(jnp.sum(acc, axis=1, keepdims=True))
        gcol = jnp.sum(jnp.where(eye, gate8[k:k + 1, :], 0.0), axis=1, keepdims=True)
        w = gcol * act
        new_tiles = []
        for s in range(n_sub):
            v_s = buf[pl.ds(base + n_sub + s, N_PICKS, stride=PICK_PITCH), :]
            o_s = jnp.sum(v_s * w, axis=0, keepdims=True)
            new_tiles.append(jnp.where(sub == k, o_s, tiles[s]))
            prefetch(piece)
            piece += 1
        return new_tiles

    def group(gi, last):
        r0 = pl.multiple_of(gi * N_SLOTS, N_SLOTS)
        x8 = [hf_s[pl.ds(r0, N_SLOTS), s * LANES:(s + 1) * LANES] for s in range(n_sub)]
        gate8 = gate_ref[pl.ds(r0, N_SLOTS), :]
        tiles = [jnp.zeros((SUBLANES, LANES), F32) for _ in range(n_sub)]
        for k in range(N_SLOTS):
            ahead = k + PREFETCH_DIST
            if last and ahead >= N_SLOTS:
                prefetch = lambda piece: None
            else:
                def prefetch(piece, ahead=ahead):
                    issue(r0 + ahead, ahead % N_SLOTS,
                          piece * picks_per_piece, (piece + 1) * picks_per_piece)
            tiles = token(k, x8, gate8, tiles, prefetch)
        res = [h_ref[pl.ds(r0, N_SLOTS), s * LANES:(s + 1) * LANES] + tiles[s] for s in range(n_sub)]
        if final_norm:
            ss = res[0] * res[0]
            for s in range(1, n_sub):
                ss = ss + res[s] * res[s]
            scale = lax.rsqrt(jnp.sum(ss, axis=1, keepdims=True) * (1.0 / D_MODEL) + EPS)
            res = [res[s] * scale * gfin_ref[:, s * LANES:(s + 1) * LANES] for s in range(n_sub)]
        for s in range(n_sub):
            out_ref[pl.ds(r0, N_SLOTS), s * LANES:(s + 1) * LANES] = res[s]

    for j in range(PREFETCH_DIST):
        issue(j, j, 0, N_PICKS)

    def body(gi, carry):
        group(gi, False)
        return carry

    lax.fori_loop(0, n_groups - 1, body, 0)
    group(n_groups - 1, True)


def _gather(idx, gate, h, g, gfin, tab, tt, final_norm):
    t = h.shape[0]
    row = lambda i: (i, 0)
    const2 = lambda i: (0, 0)
    return pl.pallas_call(
        functools.partial(_gather_kernel, tt=tt, final_norm=final_norm),
        grid=(t // tt,),
        in_specs=[
            pl.BlockSpec((tt, N_PICKS), row, memory_space=pltpu.SMEM),
            pl.BlockSpec((tt, N_PICKS), row),
            pl.BlockSpec((tt, D_MODEL), row),
            pl.BlockSpec((1, D_MODEL), const2),
            pl.BlockSpec((1, D_MODEL), const2),
            pl.BlockSpec(memory_space=pl.ANY),
        ],
        out_specs=pl.BlockSpec((tt, D_MODEL), row),
        out_shape=jax.ShapeDtypeStruct((t, D_MODEL), F32),
        scratch_shapes=[
            pltpu.VMEM((tt, D_MODEL), F32),
            pltpu.VMEM((N_SLOTS * SLOT_ROWS, LANES), F32),
            pltpu.SemaphoreType.DMA((N_SLOTS,)),
        ],
        compiler_params=_cparams(1),
        name="peer_experts",
    )(idx, gate, h, g, gfin, tab)


def _proj_kernel(h_ref, gkv_ref, gq_ref, wkv_ref, wq_ref, kv_ref, q_ref):
    h = h_ref[...]
    n = h * lax.rsqrt(jnp.mean(h * h, axis=-1, keepdims=True) + EPS)
    kv_ref[...] = jnp.dot((n * gkv_ref[...]).astype(BF16), wkv_ref[...], preferred_element_type=F32)
    q_ref[...] = jnp.dot((n * gq_ref[...]).astype(BF16), wq_ref[...],
                         preferred_element_type=F32).astype(BF16)


def _proj(h, gkv, gq, wkv, wq, tm):
    t = h.shape[0]
    row = lambda i: (i, 0)
    const2 = lambda i: (0, 0)
    return pl.pallas_call(
        _proj_kernel,
        grid=(t // tm,),
        in_specs=[
            pl.BlockSpec((tm, D_MODEL), row),
            pl.BlockSpec((1, D_MODEL), const2),
            pl.BlockSpec((1, D_MODEL), const2),
            pl.BlockSpec((D_MODEL, KV_DIM), const2),
            pl.BlockSpec((D_MODEL, D_MODEL), const2),
        ],
        out_specs=[pl.BlockSpec((tm, KV_DIM), row), pl.BlockSpec((tm, D_MODEL), row)],
        out_shape=[jax.ShapeDtypeStruct((t, KV_DIM), F32),
                   jax.ShapeDtypeStruct((t, D_MODEL), BF16)],
        compiler_params=_cparams(1),
        name="kv_q_proj",
    )(h, gkv, gq, wkv, wq)


def _alibi_slopes():
    return [2.0 ** (-8.0 * (h + 1) / N_HEADS) for h in range(N_HEADS)]


def _attn_chunk(q, win, first_valid, sink_ref):
    n_keys = WINDOW + CHUNK
    pair = 2 * CHUNK
    row = lax.broadcasted_iota(jnp.int32, (pair, n_keys), 0)
    kj = lax.broadcasted_iota(jnp.int32, (pair, n_keys), 1)
    dist = jnp.abs(WINDOW + (row & (CHUNK - 1)) - kj).astype(F32)
    valid = kj >= first_valid
    upper = lax.broadcasted_iota(jnp.int32, (pair, 1), 0) >= CHUNK
    low = lax.broadcasted_iota(jnp.int32, (n_keys, LANES), 1) < HEAD_DIM
    slopes = _alibi_slopes()
    half_kv = KV_DIM // 2
    o_tiles = [None] * (N_HEADS // 2)
    for g in range(N_KV_HEADS):
        t = g // 2
        ktile = win[:, t * LANES:(t + 1) * LANES]
        vtile = win[:, half_kv + t * LANES:half_kv + (t + 1) * LANES]
        kroll = pltpu.roll(ktile, HEAD_DIM, axis=1)
        vroll = pltpu.roll(vtile, HEAD_DIM, axis=1)
        if g % 2 == 0:
            k_pad = [jnp.where(low, ktile, 0.0), jnp.where(low, 0.0, kroll)]
            v_pad = [jnp.where(low, vtile, 0.0), jnp.where(low, 0.0, vroll)]
        else:
            k_pad = [jnp.where(low, kroll, 0.0), jnp.where(low, 0.0, ktile)]
            v_pad = [jnp.where(low, vroll, 0.0), jnp.where(low, 0.0, vtile)]
        qstack = jnp.concatenate([q[:, (2 * g) * LANES:(2 * g + 1) * LANES],
                                  q[:, (2 * g + 1) * LANES:(2 * g + 2) * LANES]], axis=0)
        o_stack = jnp.zeros((pair, LANES), F32)
        for half in range(2):
            h_lo = GQA_GROUP * g + half
            h_up = h_lo + 2
            s = lax.dot_general(qstack, k_pad[half].astype(BF16), (((1,), (1,)), ((), ())),
                                preferred_element_type=F32)
            slope = jnp.where(upper, slopes[h_up], slopes[h_lo])
            s = s * (HEAD_DIM ** -0.5) - slope * dist
            s = jnp.where(valid, s, NEG)
            sink = jnp.where(upper, sink_ref[0, h_up], sink_ref[0, h_lo])
            m = jnp.maximum(jnp.max(s, axis=1, keepdims=True), sink)
            e = jnp.exp(s - m)
            denom = jnp.sum(e, axis=1, keepdims=True) + jnp.exp(sink - m)
            p = e / denom
            o_stack = o_stack + jnp.dot(p.astype(BF16), v_pad[half].astype(BF16),
                                        preferred_element_type=F32)
        o_tiles[2 * g] = o_stack[:CHUNK]
        o_tiles[2 * g + 1] = o_stack[CHUNK:]
    return jnp.concatenate(o_tiles, axis=1)


def _attn_prompt_kernel(sink_ref, q_ref, kvp_ref, kvc_ref, h_ref, wo_ref, out_ref, *, qt):
    t = pl.program_id(1)
    kv = jnp.concatenate([kvp_ref[...], kvc_ref[...]], axis=0)
    n_chunks = qt // CHUNK
    outs = []
    for c in range(n_chunks):
        r0 = c * CHUNK
        first_valid = WINDOW - CHUNK * (t * n_chunks + c)
        outs.append(_attn_chunk(q_ref[r0:r0 + CHUNK, :], kv[r0:r0 + WINDOW + CHUNK, :],
                                first_valid, sink_ref))
    o = jnp.concatenate(outs, axis=0).astype(BF16)
    out_ref[...] = h_ref[...] + jnp.dot(o, wo_ref[...], preferred_element_type=F32)


def _attn_prompt(sinks, q, kv, h, wo, batch, seq, qt):
    t = h.shape[0]
    tiles = seq // qt
    cur = lambda b, i: (b * tiles + i, 0)
    prev = lambda b, i: (jnp.maximum((b * tiles + i) * (qt // WINDOW) - 1, 0), 0)
    const2 = lambda b, i: (0, 0)
    return pl.pallas_call(
        functools.partial(_attn_prompt_kernel, qt=qt),
        grid=(batch, tiles),
        in_specs=[
            pl.BlockSpec((1, N_HEADS), const2, memory_space=pltpu.SMEM),
            pl.BlockSpec((qt, D_MODEL), cur),
            pl.BlockSpec((WINDOW, KV_DIM), prev),
            pl.BlockSpec((qt, KV_DIM), cur),
            pl.BlockSpec((qt, D_MODEL), cur),
            pl.BlockSpec((D_MODEL, D_MODEL), const2),
        ],
        out_specs=pl.BlockSpec((qt, D_MODEL), cur),
        out_shape=jax.ShapeDtypeStruct((t, D_MODEL), F32),
        compiler_params=_cparams(2),
        name="attn_prompt",
    )(sinks, q, kv, kv, h, wo)


def _attn_sample_kernel(sink_ref, q_ref, ck_ref, cv_ref, kvn_ref, h_ref, wo_ref, out_ref, *, nb):
    half_kv = KV_DIM // 2
    outs = []
    for b in range(nb):
        kvn = kvn_ref[b * CHUNK:(b + 1) * CHUNK, :]
        k = jnp.concatenate([ck_ref[b * WINDOW:(b + 1) * WINDOW, :], kvn[:, :half_kv]], axis=0)
        v = jnp.concatenate([cv_ref[b * WINDOW:(b + 1) * WINDOW, :], kvn[:, half_kv:]], axis=0)
        win = jnp.concatenate([k, v], axis=1)
        outs.append(_attn_chunk(q_ref[b * CHUNK:(b + 1) * CHUNK, :], win, 0, sink_ref))
    o = jnp.concatenate(outs, axis=0).astype(BF16)
    out_ref[...] = h_ref[...] + jnp.dot(o, wo_ref[...], preferred_element_type=F32)


def _attn_sample(sinks, q, ck, cv, kv, h, wo, batch, nb):
    t = h.shape[0]
    row = lambda b: (b, 0)
    const2 = lambda b: (0, 0)
    return pl.pallas_call(
        functools.partial(_attn_sample_kernel, nb=nb),
        grid=(batch // nb,),
        in_specs=[
            pl.BlockSpec((1, N_HEADS), const2, memory_space=pltpu.SMEM),
            pl.BlockSpec((nb * CHUNK, D_MODEL), row),
            pl.BlockSpec((nb * WINDOW, KV_DIM // 2), row),
            pl.BlockSpec((nb * WINDOW, KV_DIM // 2), row),
            pl.BlockSpec((nb * CHUNK, KV_DIM), row),
            pl.BlockSpec((nb * CHUNK, D_MODEL), row),
            pl.BlockSpec((D_MODEL, D_MODEL), const2),
        ],
        out_specs=pl.BlockSpec((nb * CHUNK, D_MODEL), row),
        out_shape=jax.ShapeDtypeStruct((t, D_MODEL), F32),
        compiler_params=_cparams(1),
        name="attn_sample",
    )(sinks, q, ck, cv, kv, h, wo)


def kernel(x_prompt, x_sample, cache_k_win, cache_v_win, norm_mix, norm_ffn, gmlp_w_in, gmlp_norm_v, gmlp_w_s, gmlp_b_s, gmlp_w_out, norm_kv, w_kv, attn_w_q, attn_sinks, attn_w_o, peer_w_q, peer_sub_keys, peer_u, peer_v, norm_final):
    batch, seq, _ = x_prompt.shape
    dec_batch, dec_seq, _ = x_sample.shape
    w_cache = cache_k_win.shape[1]
    assert dec_seq == CHUNK and w_cache == WINDOW and seq % MLP_CHUNK == 0

    row2 = lambda a: a.reshape(1, -1)
    win = gmlp_w_in[0].astype(BF16)
    wout = gmlp_w_out[0].astype(BF16)
    wkv = w_kv.astype(BF16)
    wq_attn = attn_w_q[0].astype(BF16)
    wo = attn_w_o[0].astype(BF16)
    wq_peer = peer_w_q.astype(BF16)
    sub_keys = peer_sub_keys.reshape(2, 2 * PEER_HEADS, N_KEYS, D_HALF).astype(BF16)
    tables = [
        jnp.concatenate([peer_u[l].reshape(N_EXPERTS, D_MODEL // LANES, LANES),
                         peer_v[l].reshape(N_EXPERTS, D_MODEL // LANES, LANES)], axis=1)
        for l in range(2)
    ]

    pos = jnp.arange(MLP_CHUNK)
    blk_i = pos[:, None] // CHUNK
    blk_j = pos[None, :] // CHUNK
    ws = gmlp_w_s[0]
    bs = gmlp_b_s[0]
    wmix_p = jnp.where((blk_j <= blk_i)[None], ws, 0.0).astype(BF16)
    ws64 = ws[:, :CHUNK, :CHUNK]
    wmix_s = jnp.where((blk_j == blk_i)[None], jnp.tile(ws64, (1, 2, 2)), 0.0).astype(BF16)
    bias_p = jnp.repeat(bs.T, LANES, axis=1)
    bias_s = jnp.repeat(jnp.tile(bs[:, :CHUNK], (1, 2)).T, LANES, axis=1)

    def trunk(x, prompt):
        t = x.shape[0]
        tm = 256
        tm_route = 512
        tt = 128
        h, v_rows = _gmlp(x, row2(norm_mix[0]), win, row2(gmlp_norm_v[0]),
                          wmix_p if prompt else wmix_s, bias_p if prompt else bias_s, wout, tm)
        idx, gate = _route(h, row2(norm_ffn[0]), wq_peer[0], sub_keys[0], tm_route)
        h = _gather(idx, gate, h, row2(norm_ffn[0]), row2(norm_final), tables[0], tt, False)
        kv, q = _proj(h, row2(norm_kv), row2(norm_mix[1]), wkv, wq_attn, tm)
        if prompt:
            h = _attn_prompt(attn_sinks, q, kv, h, wo, batch, seq, 256)
        else:
            ck = cache_k_win.reshape(dec_batch * w_cache, KV_DIM // 2)
            cv = cache_v_win.reshape(dec_batch * w_cache, KV_DIM // 2)
            h = _attn_sample(attn_sinks, q, ck, cv, kv, h, wo, dec_batch, 4)
        idx, gate = _route(h, row2(norm_ffn[1]), wq_peer[1], sub_keys[1], tm_route)
        y = _gather(idx, gate, h, row2(norm_ffn[1]), row2(norm_final), tables[1], tt, True)
        return y, kv, v_rows

    y_p, kv_p, _ = trunk(x_prompt.reshape(batch * seq, D_MODEL), True)
    y_s, kv_s, v_s = trunk(x_sample.reshape(dec_batch * dec_seq, D_MODEL), False)

    half = KV_DIM // 2
    kv_p = kv_p.reshape(batch, seq, KV_DIM)[:, seq - WINDOW:]
    prompt_k_win = kv_p[..., :half].reshape(batch, WINDOW, N_KV_HEADS, HEAD_DIM)
    prompt_v_win = kv_p[..., half:].reshape(batch, WINDOW, N_KV_HEADS, HEAD_DIM)
    kv_s = kv_s.reshape(dec_batch, dec_seq, KV_DIM)
    k_s = kv_s[..., :half].reshape(dec_batch, dec_seq, N_KV_HEADS, HEAD_DIM)
    v_s_new = kv_s[..., half:].reshape(dec_batch, dec_seq, N_KV_HEADS, HEAD_DIM)
    sample_k_win = jnp.concatenate([cache_k_win, k_s], axis=1)[:, -w_cache:]
    sample_v_win = jnp.concatenate([cache_v_win, v_s_new], axis=1)[:, -w_cache:]
    sample_gmlp_v = v_s.reshape(1, dec_batch, dec_seq, D_GATE)

    return (y_p.reshape(batch, seq, D_MODEL), y_s.reshape(dec_batch, dec_seq, D_MODEL),
            prompt_k_win, prompt_v_win, sample_k_win, sample_v_win, sample_gmlp_v)
```

```python
import functools
import math

import jax
import jax.numpy as jnp
from jax import lax
from jax.experimental import pallas as pl
from jax.experimental.pallas import tpu as pltpu

F32 = jnp.float32
BF16 = jnp.bfloat16

D_MODEL = 1024
CHUNK = 64
MLP_CHUNK = 128
GMLP_GROUPS = 8
D_GATE = D_MODEL
N_HEADS = 16
N_KV_HEADS = 4
HEAD_DIM = 64
GQA_GROUP = N_HEADS // N_KV_HEADS
WINDOW = 128
PEER_HEADS = 8
N_KEYS = 128
N_EXPERTS = N_KEYS * N_KEYS
PEER_TOPK = 16
D_HALF = 128
N_PICKS = PEER_HEADS * PEER_TOPK
EPS = 1e-6
NEG = -1e30
KV_DIM = 2 * N_KV_HEADS * HEAD_DIM

LANES = 128
SUBLANES = 8
ROWS_PER_EXPERT = 2 * D_MODEL // LANES
PICK_PITCH = 20
SLOT_ROWS = N_PICKS * PICK_PITCH
N_SLOTS = SUBLANES
PREFETCH_DIST = 6

VMEM_LIMIT = 48 * 1024 * 1024


def _rms(x, g):
    return x * lax.rsqrt(jnp.mean(x * x, axis=-1, keepdims=True) + EPS) * g


def _cparams(n_axes):
    return pltpu.CompilerParams(
        dimension_semantics=("arbitrary",) * n_axes, vmem_limit_bytes=VMEM_LIMIT)


def _gmlp_kernel(x_ref, gmix_ref, win_ref, gv_ref, wmix_ref, bias_ref, wout_ref,
                 h_ref, v_ref, *, tm):
    x = x_ref[...]
    hn = _rms(x, gmix_ref[...])
    hmid = jax.nn.gelu(jnp.dot(hn.astype(BF16), win_ref[...], preferred_element_type=F32))
    u = hmid[:, :D_GATE]
    v = _rms(hmid[:, D_GATE:], gv_ref[...])
    v_ref[...] = v
    vb = v.astype(BF16)
    bias = bias_ref[...]
    outs = []
    for c in range(tm // MLP_CHUNK):
        r0 = c * MLP_CHUNK
        cols = []
        for g in range(GMLP_GROUPS):
            c0 = g * LANES
            cols.append(jnp.dot(wmix_ref[g], vb[r0:r0 + MLP_CHUNK, c0:c0 + LANES],
                                preferred_element_type=F32))
        mixed = jnp.concatenate(cols, axis=1) + bias
        outs.append(u[r0:r0 + MLP_CHUNK, :] * mixed)
    out = outs[0] if len(outs) == 1 else jnp.concatenate(outs, axis=0)
    y = jnp.dot(out.astype(BF16), wout_ref[...], preferred_element_type=F32)
    h_ref[...] = x + y


def _gmlp(x, gmix, win, gv, wmix, bias, wout, tm):
    t = x.shape[0]
    row = lambda i: (i, 0)
    const2 = lambda i: (0, 0)
    return pl.pallas_call(
        functools.partial(_gmlp_kernel, tm=tm),
        grid=(t // tm,),
        in_specs=[
            pl.BlockSpec((tm, D_MODEL), row),
            pl.BlockSpec((1, D_MODEL), const2),
            pl.BlockSpec((D_MODEL, 2 * D_GATE), const2),
            pl.BlockSpec((1, D_GATE), const2),
            pl.BlockSpec((GMLP_GROUPS, MLP_CHUNK, MLP_CHUNK), lambda i: (0, 0, 0)),
            pl.BlockSpec((MLP_CHUNK, D_GATE), const2),
            pl.BlockSpec((D_GATE, D_MODEL), const2),
        ],
        out_specs=[pl.BlockSpec((tm, D_MODEL), row), pl.BlockSpec((tm, D_GATE), row)],
        out_shape=[jax.ShapeDtypeStruct((t, D_MODEL), F32),
                   jax.ShapeDtypeStruct((t, D_GATE), F32)],
        compiler_params=_cparams(1),
        name="gmlp_mixer",
    )(x, gmix, win, gv, wmix, bias, wout)


CAND_COUNT = [PEER_TOPK // (a + 1) for a in range(PEER_TOPK)]
CAND_OFFSET = [sum(CAND_COUNT[:a]) for a in range(PEER_TOPK)]
CAND_USED = sum(CAND_COUNT)
CAND_ROWS = -(-CAND_USED // SUBLANES) * SUBLANES


def _route_kernel(h_ref, g_ref, wq_ref, sk_ref, idx_ref, gate_ref,
                  sc_s, ts_s, ti_s, cand_s, best_s, sel_s, e_s, gt_s, *, tm):
    hf = _rms(h_ref[...], g_ref[...])
    q = jnp.dot(hf.astype(BF16), wq_ref[...], preferred_element_type=F32).astype(BF16)
    n_hp = 2 * PEER_HEADS
    for hp in range(n_hp):
        sc_s[hp] = lax.dot_general(sk_ref[hp], q[:, hp * D_HALF:(hp + 1) * D_HALF],
                                   (((1,), (1,)), ((), ())), preferred_element_type=F32)

    n_rows = N_KEYS // SUBLANES
    sub_key = lax.broadcasted_iota(jnp.int32, (SUBLANES, LANES), 0).astype(F32)

    def stage1(hp, carry):
        for lg in range(tm // LANES):
            l0 = lg * LANES
            val = [sc_s[hp, v * SUBLANES:(v + 1) * SUBLANES, l0:l0 + LANES] for v in range(n_rows)]
            key = [sub_key + float(v * SUBLANES) for v in range(n_rows)]
            for rnd in range(n_rows):
                for a in range(rnd % 2, n_rows - 1, 2):
                    swap = val[a + 1] > val[a]
                    val[a], val[a + 1] = (jnp.where(swap, val[a + 1], val[a]),
                                          jnp.where(swap, val[a], val[a + 1]))
                    key[a], key[a + 1] = (jnp.where(swap, key[a + 1], key[a]),
                                          jnp.where(swap, key[a], key[a + 1]))
            for k in range(PEER_TOPK):
                m = jnp.max(val[0], axis=0, keepdims=True)
                i = jnp.min(jnp.where(val[0] == m, key[0], float(N_KEYS)), axis=0, keepdims=True)
                ts_s[hp, k:k + 1, l0:l0 + LANES] = m
                ti_s[hp, k:k + 1, l0:l0 + LANES] = i
                if k + 1 < PEER_TOPK:
                    taken = key[0] == i
                    for p in range(PEER_TOPK - 1 - k):
                        val[p] = jnp.where(taken, val[p + 1], val[p])
                        key[p] = jnp.where(taken, key[p + 1], key[p])
        return carry

    lax.fori_loop(0, n_hp, stage1, 0)

    n_cand = float(PEER_TOPK * PEER_TOPK)
    slot = lax.broadcasted_iota(jnp.int32, (CAND_ROWS, LANES), 0)
    cand_flat = slot
    for a in range(1, PEER_TOPK):
        step = (a * PEER_TOPK - CAND_OFFSET[a]) - ((a - 1) * PEER_TOPK - CAND_OFFSET[a - 1])
        cand_flat = cand_flat + jnp.where(slot >= CAND_OFFSET[a], step, 0)
    cand_flat = jnp.where(slot >= CAND_USED, PEER_TOPK * PEER_TOPK, cand_flat).astype(F32)

    def stage2(hd, carry):
        for lg in range(tm // LANES):
            l0 = lg * LANES
            s0 = ts_s[2 * hd, :, l0:l0 + LANES]
            s1 = ts_s[2 * hd + 1, :, l0:l0 + LANES]
            i0 = ti_s[2 * hd, :, l0:l0 + LANES]
            i1 = ti_s[2 * hd + 1, :, l0:l0 + LANES]
            for a in range(PEER_TOPK):
                n = CAND_COUNT[a]
                cand_s[CAND_OFFSET[a]:CAND_OFFSET[a] + n, :] = s0[a:a + 1, :] + s1[0:n, :]
            cand_s[CAND_USED:CAND_ROWS, :] = jnp.full((CAND_ROWS - CAND_USED, LANES), -jnp.inf, F32)
            cand = cand_s[...]
            for k in range(PEER_TOPK):
                m = jnp.max(cand, axis=0, keepdims=True)
                c = jnp.min(jnp.where(cand == m, cand_flat, n_cand), axis=0, keepdims=True)
                best_s[k:k + 1, :] = m
                sel_s[k:k + 1, :] = c
                cand = jnp.where(cand_flat == c, -jnp.inf, cand)
            best = best_s[...]
            sel = sel_s[...]
            sel_a = jnp.floor(sel * (1.0 / PEER_TOPK))
            sel_b = sel - sel_a * PEER_TOPK
            e0 = jnp.zeros_like(sel)
            e1 = jnp.zeros_like(sel)
            for a in range(PEER_TOPK):
                e0 = jnp.where(sel_a == float(a), i0[a:a + 1, :], e0)
                e1 = jnp.where(sel_b == float(a), i1[a:a + 1, :], e1)
            ex = jnp.exp(best - best[0:1, :])
            gate = ex / jnp.sum(ex, axis=0, keepdims=True)
            r0 = pl.multiple_of(hd * PEER_TOPK, PEER_TOPK)
            e_s[pl.ds(r0, PEER_TOPK), l0:l0 + LANES] = e0 * float(N_KEYS) + e1
            gt_s[pl.ds(r0, PEER_TOPK), l0:l0 + LANES] = gate
        return carry

    lax.fori_loop(0, PEER_HEADS, stage2, 0)

    idx_ref[...] = jnp.transpose(e_s[...]).astype(jnp.int32)
    gate_ref[...] = jnp.transpose(gt_s[...])


def _route(h, g, wq, sk, tm):
    t = h.shape[0]
    row = lambda i: (i, 0)
    const2 = lambda i: (0, 0)
    n_hp = 2 * PEER_HEADS
    return pl.pallas_call(
        functools.partial(_route_kernel, tm=tm),
        grid=(t // tm,),
        in_specs=[
            pl.BlockSpec((tm, D_MODEL), row),
            pl.BlockSpec((1, D_MODEL), const2),
            pl.BlockSpec((D_MODEL, n_hp * D_HALF), const2),
            pl.BlockSpec((n_hp, N_KEYS, D_HALF), lambda i: (0, 0, 0)),
        ],
        out_specs=[pl.BlockSpec((tm, N_PICKS), row), pl.BlockSpec((tm, N_PICKS), row)],
        out_shape=[jax.ShapeDtypeStruct((t, N_PICKS), jnp.int32),
                   jax.ShapeDtypeStruct((t, N_PICKS), F32)],
        scratch_shapes=[
            pltpu.VMEM((n_hp, N_KEYS, tm), F32),
            pltpu.VMEM((n_hp, PEER_TOPK, tm), F32),
            pltpu.VMEM((n_hp, PEER_TOPK, tm), F32),
            pltpu.VMEM((CAND_ROWS, LANES), F32),
            pltpu.VMEM((PEER_TOPK, LANES), F32),
            pltpu.VMEM((PEER_TOPK, LANES), F32),
            pltpu.VMEM((N_PICKS, tm), F32),
            pltpu.VMEM((N_PICKS, tm), F32),
        ],
        compiler_params=_cparams(1),
        name="peer_route",
    )(h, g, wq, sk)


def _gather_kernel(idx_ref, gate_ref, h_ref, g_ref, gfin_ref, tab_ref, out_ref,
                   hf_s, buf, sem, *, tt, final_norm):
    hf_s[...] = _rms(h_ref[...], g_ref[...])
    n_sub = D_MODEL // LANES
    n_groups = tt // N_SLOTS
    n_pieces = 2 * n_sub
    picks_per_piece = N_PICKS // n_pieces

    def pick_copy(e, slot, r):
        dst = buf.at[pl.ds(slot * SLOT_ROWS + r * PICK_PITCH, ROWS_PER_EXPERT), :]
        return pltpu.make_async_copy(tab_ref.at[e], dst, sem.at[slot])

    def issue(j, slot, r_lo, r_hi):
        for r in range(r_lo, r_hi):
            pick_copy(idx_ref[j, r], slot, r).start(priority=r % 2)

    def wait(slot):
        for r in range(N_PICKS):
            pick_copy(0, slot, r).wait()

    eye = (lax.broadcasted_iota(jnp.int32, (N_PICKS, N_PICKS), 0)
           == lax.broadcasted_iota(jnp.int32, (N_PICKS, N_PICKS), 1))
    sub = lax.broadcasted_iota(jnp.int32, (SUBLANES, LANES), 0)

    def token(k, x8, gate8, tiles, prefetch):
        base = k * SLOT_ROWS
        wait(k)
        piece = 0
        acc = jnp.zeros((N_PICKS, LANES), F32)
        for s in range(n_sub):
            u_s = buf[pl.ds(base + s, N_PICKS, stride=PICK_PITCH), :]
            acc = acc + u_s * x8[s][k:k + 1, :]
            prefetch(piece)
            piece += 1
        act = jax.nn.gelu(jnp.sum(acc, axis=1, keepdims=True))
        gcol = jnp.sum(jnp.where(eye, gate8[k:k + 1, :], 0.0), axis=1, keepdims=True)
        w = gcol * act
        new_tiles = []
        for s in range(n_sub):
            v_s = buf[pl.ds(base + n_sub + s, N_PICKS, stride=PICK_PITCH), :]
            o_s = jnp.sum(v_s * w, axis=0, keepdims=True)
            new_tiles.append(jnp.where(sub == k, o_s, tiles[s]))
            prefetch(piece)
            piece += 1
        return new_tiles

    def group(gi, last):
        r0 = pl.multiple_of(gi * N_SLOTS, N_SLOTS)
        x8 = [hf_s[pl.ds(r0, N_SLOTS), s * LANES:(s + 1) * LANES] for s in range(n_sub)]
        gate8 = gate_ref[pl.ds(r0, N_SLOTS), :]
        tiles = [jnp.zeros((SUBLANES, LANES), F32) for _ in range(n_sub)]
        for k in range(N_SLOTS):
            ahead = k + PREFETCH_DIST
            if last and ahead >= N_SLOTS:
                prefetch = lambda piece: None
            else:
                def prefetch(piece, ahead=ahead):
                    issue(r0 + ahead, ahead % N_SLOTS,
                          piece * picks_per_piece, (piece + 1) * picks_per_piece)
            tiles = token(k, x8, gate8, tiles, prefetch)
        res = [h_ref[pl.ds(r0, N_SLOTS), s * LANES:(s + 1) * LANES] + tiles[s] for s in range(n_sub)]
        if final_norm:
            ss = res[0] * res[0]
            for s in range(1, n_sub):
                ss = ss + res[s] * res[s]
            scale = lax.rsqrt(jnp.sum(ss, axis=1, keepdims=True) * (1.0 / D_MODEL) + EPS)
            res = [res[s] * scale * gfin_ref[:, s * LANES:(s + 1) * LANES] for s in range(n_sub)]
        for s in range(n_sub):
            out_ref[pl.ds(r0, N_SLOTS), s * LANES:(s + 1) * LANES] = res[s]

    for j in range(PREFETCH_DIST):
        issue(j, j, 0, N_PICKS)

    def body(gi, carry):
        group(gi, False)
        return carry

    lax.fori_loop(0, n_groups - 1, body, 0)
    group(n_groups - 1, True)


def _gather(idx, gate, h, g, gfin, tab, tt, final_norm):
    t = h.shape[0]
    row = lambda i: (i, 0)
    const2 = lambda i: (0, 0)
    return pl.pallas_call(
        functools.partial(_gather_kernel, tt=tt, final_norm=final_norm),
        grid=(t // tt,),
        in_specs=[
            pl.BlockSpec((tt, N_PICKS), row, memory_space=pltpu.SMEM),
            pl.BlockSpec((tt, N_PICKS), row),
            pl.BlockSpec((tt, D_MODEL), row),
            pl.BlockSpec((1, D_MODEL), const2),
            pl.BlockSpec((1, D_MODEL), const2),
            pl.BlockSpec(memory_space=pl.ANY),
        ],
        out_specs=pl.BlockSpec((tt, D_MODEL), row),
        out_shape=jax.ShapeDtypeStruct((t, D_MODEL), F32),
        scratch_shapes=[
            pltpu.VMEM((tt, D_MODEL), F32),
            pltpu.VMEM((N_SLOTS * SLOT_ROWS, LANES), F32),
            pltpu.SemaphoreType.DMA((N_SLOTS,)),
        ],
        compiler_params=_cparams(1),
        name="peer_experts",
    )(idx, gate, h, g, gfin, tab)


def _proj_kernel(h_ref, gkv_ref, gq_ref, wkv_ref, wq_ref, kv_ref, q_ref):
    h = h_ref[...]
    n = h * lax.rsqrt(jnp.mean(h * h, axis=-1, keepdims=True) + EPS)
    kv_ref[...] = jnp.dot((n * gkv_ref[...]).astype(BF16), wkv_ref[...], preferred_element_type=F32)
    q_ref[...] = jnp.dot((n * gq_ref[...]).astype(BF16), wq_ref[...],
                         preferred_element_type=F32).astype(BF16)


def _proj(h, gkv, gq, wkv, wq, tm):
    t = h.shape[0]
    row = lambda i: (i, 0)
    const2 = lambda i: (0, 0)
    return pl.pallas_call(
        _proj_kernel,
        grid=(t // tm,),
        in_specs=[
            pl.BlockSpec((tm, D_MODEL), row),
            pl.BlockSpec((1, D_MODEL), const2),
            pl.BlockSpec((1, D_MODEL), const2),
            pl.BlockSpec((D_MODEL, KV_DIM), const2),
            pl.BlockSpec((D_MODEL, D_MODEL), const2),
        ],
        out_specs=[pl.BlockSpec((tm, KV_DIM), row), pl.BlockSpec((tm, D_MODEL), row)],
        out_shape=[jax.ShapeDtypeStruct((t, KV_DIM), F32),
                   jax.ShapeDtypeStruct((t, D_MODEL), BF16)],
        compiler_params=_cparams(1),
        name="kv_q_proj",
    )(h, gkv, gq, wkv, wq)


def _alibi_slopes():
    return [2.0 ** (-8.0 * (h + 1) / N_HEADS) for h in range(N_HEADS)]


ATTN_KEYS = WINDOW + CHUNK
ATTN_PAIR = 2 * CHUNK
ATTN_ROW_BLOCK = 32


def _padded_heads(tile, g):
    low = lax.broadcasted_iota(jnp.int32, tile.shape, 1) < HEAD_DIM
    rolled = pltpu.roll(tile, HEAD_DIM, axis=1)
    if g % 2 == 0:
        pads = [jnp.where(low, tile, 0.0), jnp.where(low, 0.0, rolled)]
    else:
        pads = [jnp.where(low, rolled, 0.0), jnp.where(low, 0.0, tile)]
    return [p.astype(BF16) for p in pads]


def _attn_chunks(q_ref, windows, first_valids, sink_ref, s_scr, p_scr):
    half_kv = KV_DIM // 2
    slopes = _alibi_slopes()
    n_chunks = len(windows)

    def chain(c, g, half):
        return (c * N_KV_HEADS + g) * 2 + half

    for c in range(n_chunks):
        win = windows[c]()
        r0 = c * CHUNK
        for g in range(N_KV_HEADS):
            t = g // 2
            k_pad = _padded_heads(win[:, t * LANES:(t + 1) * LANES], g)
            qstack = jnp.concatenate(
                [q_ref[r0:r0 + CHUNK, (2 * g) * LANES:(2 * g + 1) * LANES],
                 q_ref[r0:r0 + CHUNK, (2 * g + 1) * LANES:(2 * g + 2) * LANES]], axis=0)
            for half in range(2):
                s_scr[chain(c, g, half)] = lax.dot_general(
                    qstack, k_pad[half], (((1,), (1,)), ((), ())), preferred_element_type=F32)

    rb = ATTN_ROW_BLOCK
    qi = lax.broadcasted_iota(jnp.int32, (rb, ATTN_KEYS), 0)
    kj = lax.broadcasted_iota(jnp.int32, (rb, ATTN_KEYS), 1)
    dists = [jnp.abs(WINDOW + qi + off - kj).astype(F32) for off in range(0, CHUNK, rb)]
    for c in range(n_chunks):
        valid = kj >= first_valids[c]
        for g in range(N_KV_HEADS):
            for half in range(2):
                ch = chain(c, g, half)
                for b in range(ATTN_PAIR // rb):
                    row0 = b * rb
                    head = GQA_GROUP * g + half + (2 if row0 >= CHUNK else 0)
                    s = s_scr[ch, row0:row0 + rb, :]
                    s = s * (HEAD_DIM ** -0.5) - slopes[head] * dists[(row0 % CHUNK) // rb]
                    s = jnp.where(valid, s, NEG)
                    sink = sink_ref[0, head]
                    m = jnp.maximum(jnp.max(s, axis=1, keepdims=True), sink)
                    e = jnp.exp(s - m)
                    denom = jnp.sum(e, axis=1, keepdims=True) + jnp.exp(sink - m)
                    p_scr[ch, row0:row0 + rb, :] = (e / denom).astype(BF16)

    outs = []
    for c in range(n_chunks):
        win = windows[c]()
        o_tiles = [None] * (N_HEADS // 2)
        for g in range(N_KV_HEADS):
            t = g // 2
            v_pad = _padded_heads(win[:, half_kv + t * LANES:half_kv + (t + 1) * LANES], g)
            o_stack = (jnp.dot(p_scr[chain(c, g, 0)], v_pad[0], preferred_element_type=F32)
                       + jnp.dot(p_scr[chain(c, g, 1)], v_pad[1], preferred_element_type=F32))
            o_tiles[2 * g] = o_stack[:CHUNK]
            o_tiles[2 * g + 1] = o_stack[CHUNK:]
        outs.append(jnp.concatenate(o_tiles, axis=1))
    return outs[0] if n_chunks == 1 else jnp.concatenate(outs, axis=0)


def _attn_scratch(n_chunks):
    n_chains = n_chunks * N_KV_HEADS * 2
    return [pltpu.VMEM((n_chains, ATTN_PAIR, ATTN_KEYS), F32),
            pltpu.VMEM((n_chains, ATTN_PAIR, ATTN_KEYS), BF16)]


def _attn_prompt_kernel(sink_ref, q_ref, kvp_ref, kvc_ref, h_ref, wo_ref, out_ref,
                        kv_s, s_scr, p_scr, *, qt):
    t = pl.program_id(1)
    kv_s[0:WINDOW, :] = kvp_ref[...]
    kv_s[WINDOW:WINDOW + qt, :] = kvc_ref[...]
    n_chunks = qt // CHUNK
    windows = [functools.partial(lambda r0: kv_s[r0:r0 + ATTN_KEYS, :], c * CHUNK)
               for c in range(n_chunks)]
    first_valids = [WINDOW - CHUNK * (t * n_chunks + c) for c in range(n_chunks)]
    o = _attn_chunks(q_ref, windows, first_valids, sink_ref, s_scr, p_scr).astype(BF16)
    out_ref[...] = h_ref[...] + jnp.dot(o, wo_ref[...], preferred_element_type=F32)


def _attn_prompt(sinks, q, kv, h, wo, batch, seq, qt):
    t = h.shape[0]
    tiles = seq // qt
    cur = lambda b, i: (b * tiles + i, 0)
    prev = lambda b, i: (jnp.maximum((b * tiles + i) * (qt // WINDOW) - 1, 0), 0)
    const2 = lambda b, i: (0, 0)
    return pl.pallas_call(
        functools.partial(_attn_prompt_kernel, qt=qt),
        grid=(batch, tiles),
        in_specs=[
            pl.BlockSpec((1, N_HEADS), const2, memory_space=pltpu.SMEM),
            pl.BlockSpec((qt, D_MODEL), cur),
            pl.BlockSpec((WINDOW, KV_DIM), prev),
            pl.BlockSpec((qt, KV_DIM), cur),
            pl.BlockSpec((qt, D_MODEL), cur),
            pl.BlockSpec((D_MODEL, D_MODEL), const2),
        ],
        out_specs=pl.BlockSpec((qt, D_MODEL), cur),
        out_shape=jax.ShapeDtypeStruct((t, D_MODEL), F32),
        scratch_shapes=[pltpu.VMEM((WINDOW + qt, KV_DIM), F32)] + _attn_scratch(qt // CHUNK),
        compiler_params=_cparams(2),
        name="attn_prompt",
    )(sinks, q, kv, kv, h, wo)


def _attn_sample_kernel(sink_ref, q_ref, ck_ref, cv_ref, kvn_ref, h_ref, wo_ref, out_ref,
                        s_scr, p_scr, *, nb):
    half_kv = KV_DIM // 2

    def window(b):
        kvn = kvn_ref[b * CHUNK:(b + 1) * CHUNK, :]
        k = jnp.concatenate([ck_ref[b * WINDOW:(b + 1) * WINDOW, :], kvn[:, :half_kv]], axis=0)
        v = jnp.concatenate([cv_ref[b * WINDOW:(b + 1) * WINDOW, :], kvn[:, half_kv:]], axis=0)
        return jnp.concatenate([k, v], axis=1)

    windows = [functools.partial(window, b) for b in range(nb)]
    o = _attn_chunks(q_ref, windows, [0] * nb, sink_ref, s_scr, p_scr).astype(BF16)
    out_ref[...] = h_ref[...] + jnp.dot(o, wo_ref[...], preferred_element_type=F32)


def _attn_sample(sinks, q, ck, cv, kv, h, wo, batch, nb):
    t = h.shape[0]
    row = lambda b: (b, 0)
    const2 = lambda b: (0, 0)
    return pl.pallas_call(
        functools.partial(_attn_sample_kernel, nb=nb),
        grid=(batch // nb,),
        in_specs=[
            pl.BlockSpec((1, N_HEADS), const2, memory_space=pltpu.SMEM),
            pl.BlockSpec((nb * CHUNK, D_MODEL), row),
            pl.BlockSpec((nb * WINDOW, KV_DIM // 2), row),
            pl.BlockSpec((nb * WINDOW, KV_DIM // 2), row),
            pl.BlockSpec((nb * CHUNK, KV_DIM), row),
            pl.BlockSpec((nb * CHUNK, D_MODEL), row),
            pl.BlockSpec((D_MODEL, D_MODEL), const2),
        ],
        scratch_shapes=_attn_scratch(nb),
        out_specs=pl.BlockSpec((nb * CHUNK, D_MODEL), row),
        out_shape=jax.ShapeDtypeStruct((t, D_MODEL), F32),
        compiler_params=_cparams(1),
        name="attn_sample",
    )(sinks, q, ck, cv, kv, h, wo)


def kernel(x_prompt, x_sample, cache_k_win, cache_v_win, norm_mix, norm_ffn, gmlp_w_in, gmlp_norm_v, gmlp_w_s, gmlp_b_s, gmlp_w_out, norm_kv, w_kv, attn_w_q, attn_sinks, attn_w_o, peer_w_q, peer_sub_keys, peer_u, peer_v, norm_final):
    batch, seq, _ = x_prompt.shape
    dec_batch, dec_seq, _ = x_sample.shape
    w_cache = cache_k_win.shape[1]
    assert dec_seq == CHUNK and w_cache == WINDOW and seq % MLP_CHUNK == 0

    row2 = lambda a: a.reshape(1, -1)
    win = gmlp_w_in[0].astype(BF16)
    wout = gmlp_w_out[0].astype(BF16)
    wkv = w_kv.astype(BF16)
    wq_attn = attn_w_q[0].astype(BF16)
    wo = attn_w_o[0].astype(BF16)
    wq_peer = peer_w_q.astype(BF16)
    sub_keys = peer_sub_keys.reshape(2, 2 * PEER_HEADS, N_KEYS, D_HALF).astype(BF16)
    tables = [
        jnp.concatenate([peer_u[l].reshape(N_EXPERTS, D_MODEL // LANES, LANES),
                         peer_v[l].reshape(N_EXPERTS, D_MODEL // LANES, LANES)], axis=1)
        for l in range(2)
    ]

    pos = jnp.arange(MLP_CHUNK)
    blk_i = pos[:, None] // CHUNK
    blk_j = pos[None, :] // CHUNK
    ws = gmlp_w_s[0]
    bs = gmlp_b_s[0]
    wmix_p = jnp.where((blk_j <= blk_i)[None], ws, 0.0).astype(BF16)
    ws64 = ws[:, :CHUNK, :CHUNK]
    wmix_s = jnp.where((blk_j == blk_i)[None], jnp.tile(ws64, (1, 2, 2)), 0.0).astype(BF16)
    bias_p = jnp.repeat(bs.T, LANES, axis=1)
    bias_s = jnp.repeat(jnp.tile(bs[:, :CHUNK], (1, 2)).T, LANES, axis=1)

    def trunk(x, prompt):
        t = x.shape[0]
        tm = 256
        tm_route = 512
        tt = 256
        h, v_rows = _gmlp(x, row2(norm_mix[0]), win, row2(gmlp_norm_v[0]),
                          wmix_p if prompt else wmix_s, bias_p if prompt else bias_s, wout, tm)
        idx, gate = _route(h, row2(norm_ffn[0]), wq_peer[0], sub_keys[0], tm_route)
        h = _gather(idx, gate, h, row2(norm_ffn[0]), row2(norm_final), tables[0], tt, False)
        kv, q = _proj(h, row2(norm_kv), row2(norm_mix[1]), wkv, wq_attn, tm)
        if prompt:
            h = _attn_prompt(attn_sinks, q, kv, h, wo, batch, seq, 256)
        else:
            ck = cache_k_win.reshape(dec_batch * w_cache, KV_DIM // 2)
            cv = cache_v_win.reshape(dec_batch * w_cache, KV_DIM // 2)
            h = _attn_sample(attn_sinks, q, ck, cv, kv, h, wo, dec_batch, 4)
        idx, gate = _route(h, row2(norm_ffn[1]), wq_peer[1], sub_keys[1], tm_route)
        y = _gather(idx, gate, h, row2(norm_ffn[1]), row2(norm_final), tables[1], tt, True)
        return y, kv, v_rows

    y_p, kv_p, _ = trunk(x_prompt.reshape(batch * seq, D_MODEL), True)
    y_s, kv_s, v_s = trunk(x_sample.reshape(dec_batch * dec_seq, D_MODEL), False)

    half = KV_DIM // 2
    kv_p = kv_p.reshape(batch, seq, KV_DIM)[:, seq - WINDOW:]
    prompt_k_win = kv_p[..., :half].reshape(batch, WINDOW, N_KV_HEADS, HEAD_DIM)
    prompt_v_win = kv_p[..., half:].reshape(batch, WINDOW, N_KV_HEADS, HEAD_DIM)
    kv_s = kv_s.reshape(dec_batch, dec_seq, KV_DIM)
    k_s = kv_s[..., :half].reshape(dec_batch, dec_seq, N_KV_HEADS, HEAD_DIM)
    v_s_new = kv_s[..., half:].reshape(dec_batch, dec_seq, N_KV_HEADS, HEAD_DIM)
    sample_k_win = jnp.concatenate([cache_k_win, k_s], axis=1)[:, -w_cache:]
    sample_v_win = jnp.concatenate([cache_v_win, v_s_new], axis=1)[:, -w_cache:]
    sample_gmlp_v = v_s.reshape(1, dec_batch, dec_seq, D_GATE)

    return (y_p.reshape(batch, seq, D_MODEL), y_s.reshape(dec_batch, dec_seq, D_MODEL),
            prompt_k_win, prompt_v_win, sample_k_win, sample_v_win, sample_gmlp_v)
```

```python
import functools
import math

import jax
import jax.numpy as jnp
from jax import lax
from jax.experimental import pallas as pl
from jax.experimental.pallas import tpu as pltpu

F32 = jnp.float32
BF16 = jnp.bfloat16

D_MODEL = 1024
CHUNK = 64
MLP_CHUNK = 128
GMLP_GROUPS = 8
D_GATE = D_MODEL
N_HEADS = 16
N_KV_HEADS = 4
HEAD_DIM = 64
GQA_GROUP = N_HEADS // N_KV_HEADS
WINDOW = 128
PEER_HEADS = 8
N_KEYS = 128
N_EXPERTS = N_KEYS * N_KEYS
PEER_TOPK = 16
D_HALF = 128
N_PICKS = PEER_HEADS * PEER_TOPK
EPS = 1e-6
NEG = -1e30
KV_DIM = 2 * N_KV_HEADS * HEAD_DIM

LANES = 128
SUBLANES = 8
WORD_ROWS = D_MODEL // 2 // LANES
ROWS_PER_EXPERT = 2 * WORD_ROWS
HIGH_HALF = -65536
PICK_PITCH = 12
SLOT_ROWS = N_PICKS * PICK_PITCH
N_SLOTS = SUBLANES
PREFETCH_DIST = 6

VMEM_LIMIT = 48 * 1024 * 1024


def _rms(x, g):
    return x * lax.rsqrt(jnp.mean(x * x, axis=-1, keepdims=True) + EPS) * g


def _cparams(n_axes):
    return pltpu.CompilerParams(
        dimension_semantics=("arbitrary",) * n_axes, vmem_limit_bytes=VMEM_LIMIT)


def _gmlp_kernel(x_ref, gmix_ref, win_ref, gv_ref, wmix_ref, bias_ref, wout_ref,
                 h_ref, v_ref, *, tm):
    x = x_ref[...]
    hn = _rms(x, gmix_ref[...])
    hmid = jax.nn.gelu(jnp.dot(hn.astype(BF16), win_ref[...], preferred_element_type=F32))
    u = hmid[:, :D_GATE]
    v = _rms(hmid[:, D_GATE:], gv_ref[...])
    v_ref[...] = v
    vb = v.astype(BF16)
    bias = bias_ref[...]
    outs = []
    for c in range(tm // MLP_CHUNK):
        r0 = c * MLP_CHUNK
        cols = []
        for g in range(GMLP_GROUPS):
            c0 = g * LANES
            cols.append(jnp.dot(wmix_ref[g], vb[r0:r0 + MLP_CHUNK, c0:c0 + LANES],
                                preferred_element_type=F32))
        mixed = jnp.concatenate(cols, axis=1) + bias
        outs.append(u[r0:r0 + MLP_CHUNK, :] * mixed)
    out = outs[0] if len(outs) == 1 else jnp.concatenate(outs, axis=0)
    y = jnp.dot(out.astype(BF16), wout_ref[...], preferred_element_type=F32)
    h_ref[...] = x + y


def _gmlp(x, gmix, win, gv, wmix, bias, wout, tm):
    t = x.shape[0]
    row = lambda i: (i, 0)
    const2 = lambda i: (0, 0)
    return pl.pallas_call(
        functools.partial(_gmlp_kernel, tm=tm),
        grid=(t // tm,),
        in_specs=[
            pl.BlockSpec((tm, D_MODEL), row),
            pl.BlockSpec((1, D_MODEL), const2),
            pl.BlockSpec((D_MODEL, 2 * D_GATE), const2),
            pl.BlockSpec((1, D_GATE), const2),
            pl.BlockSpec((GMLP_GROUPS, MLP_CHUNK, MLP_CHUNK), lambda i: (0, 0, 0)),
            pl.BlockSpec((MLP_CHUNK, D_GATE), const2),
            pl.BlockSpec((D_GATE, D_MODEL), const2),
        ],
        out_specs=[pl.BlockSpec((tm, D_MODEL), row), pl.BlockSpec((tm, D_GATE), row)],
        out_shape=[jax.ShapeDtypeStruct((t, D_MODEL), F32),
                   jax.ShapeDtypeStruct((t, D_GATE), F32)],
        compiler_params=_cparams(1),
        name="gmlp_mixer",
    )(x, gmix, win, gv, wmix, bias, wout)


CAND_COUNT = [PEER_TOPK // (a + 1) for a in range(PEER_TOPK)]
CAND_OFFSET = [sum(CAND_COUNT[:a]) for a in range(PEER_TOPK)]
CAND_USED = sum(CAND_COUNT)
CAND_ROWS = -(-CAND_USED // SUBLANES) * SUBLANES


def _route_kernel(h_ref, g_ref, wq_ref, sk_ref, idx_ref, gate_ref,
                  sc_s, ts_s, ti_s, cand_s, best_s, sel_s, e_s, gt_s, *, tm):
    hf = _rms(h_ref[...], g_ref[...])
    q = jnp.dot(hf.astype(BF16), wq_ref[...], preferred_element_type=F32).astype(BF16)
    n_hp = 2 * PEER_HEADS
    for hp in range(n_hp):
        sc_s[hp] = lax.dot_general(sk_ref[hp], q[:, hp * D_HALF:(hp + 1) * D_HALF],
                                   (((1,), (1,)), ((), ())), preferred_element_type=F32)

    n_rows = N_KEYS // SUBLANES
    sub_key = lax.broadcasted_iota(jnp.int32, (SUBLANES, LANES), 0).astype(F32)

    def stage1(hp, carry):
        for lg in range(tm // LANES):
            l0 = lg * LANES
            val = [sc_s[hp, v * SUBLANES:(v + 1) * SUBLANES, l0:l0 + LANES] for v in range(n_rows)]
            key = [sub_key + float(v * SUBLANES) for v in range(n_rows)]
            for rnd in range(n_rows):
                for a in range(rnd % 2, n_rows - 1, 2):
                    swap = val[a + 1] > val[a]
                    val[a], val[a + 1] = (jnp.where(swap, val[a + 1], val[a]),
                                          jnp.where(swap, val[a], val[a + 1]))
                    key[a], key[a + 1] = (jnp.where(swap, key[a + 1], key[a]),
                                          jnp.where(swap, key[a], key[a + 1]))
            for k in range(PEER_TOPK):
                m = jnp.max(val[0], axis=0, keepdims=True)
                i = jnp.min(jnp.where(val[0] == m, key[0], float(N_KEYS)), axis=0, keepdims=True)
                ts_s[hp, k:k + 1, l0:l0 + LANES] = m
                ti_s[hp, k:k + 1, l0:l0 + LANES] = i
                if k + 1 < PEER_TOPK:
                    taken = key[0] == i
                    for p in range(PEER_TOPK - 1 - k):
                        val[p] = jnp.where(taken, val[p + 1], val[p])
                        key[p] = jnp.where(taken, key[p + 1], key[p])
        return carry

    lax.fori_loop(0, n_hp, stage1, 0)

    n_cand = float(PEER_TOPK * PEER_TOPK)
    slot = lax.broadcasted_iota(jnp.int32, (CAND_ROWS, LANES), 0)
    cand_flat = slot
    for a in range(1, PEER_TOPK):
        step = (a * PEER_TOPK - CAND_OFFSET[a]) - ((a - 1) * PEER_TOPK - CAND_OFFSET[a - 1])
        cand_flat = cand_flat + jnp.where(slot >= CAND_OFFSET[a], step, 0)
    cand_flat = jnp.where(slot >= CAND_USED, PEER_TOPK * PEER_TOPK, cand_flat).astype(F32)

    def stage2(hd, carry):
        for lg in range(tm // LANES):
            l0 = lg * LANES
            s0 = ts_s[2 * hd, :, l0:l0 + LANES]
            s1 = ts_s[2 * hd + 1, :, l0:l0 + LANES]
            i0 = ti_s[2 * hd, :, l0:l0 + LANES]
            i1 = ti_s[2 * hd + 1, :, l0:l0 + LANES]
            for a in range(PEER_TOPK):
                n = CAND_COUNT[a]
                cand_s[CAND_OFFSET[a]:CAND_OFFSET[a] + n, :] = s0[a:a + 1, :] + s1[0:n, :]
            cand_s[CAND_USED:CAND_ROWS, :] = jnp.full((CAND_ROWS - CAND_USED, LANES), -jnp.inf, F32)
            cand = cand_s[...]
            for k in range(PEER_TOPK):
                m = jnp.max(cand, axis=0, keepdims=True)
                c = jnp.min(jnp.where(cand == m, cand_flat, n_cand), axis=0, keepdims=True)
                best_s[k:k + 1, :] = m
                sel_s[k:k + 1, :] = c
                cand = jnp.where(cand_flat == c, -jnp.inf, cand)
            best = best_s[...]
            sel = sel_s[...]
            sel_a = jnp.floor(sel * (1.0 / PEER_TOPK))
            sel_b = sel - sel_a * PEER_TOPK
            e0 = jnp.zeros_like(sel)
            e1 = jnp.zeros_like(sel)
            for a in range(PEER_TOPK):
                e0 = jnp.where(sel_a == float(a), i0[a:a + 1, :], e0)
                e1 = jnp.where(sel_b == float(a), i1[a:a + 1, :], e1)
            ex = jnp.exp(best - best[0:1, :])
            gate = ex / jnp.sum(ex, axis=0, keepdims=True)
            r0 = pl.multiple_of(hd * PEER_TOPK, PEER_TOPK)
            e_s[pl.ds(r0, PEER_TOPK), l0:l0 + LANES] = e0 * float(N_KEYS) + e1
            gt_s[pl.ds(r0, PEER_TOPK), l0:l0 + LANES] = gate
        return carry

    lax.fori_loop(0, PEER_HEADS, stage2, 0)

    idx_ref[...] = jnp.transpose(e_s[...]).astype(jnp.int32)
    gate_ref[...] = jnp.transpose(gt_s[...])


def _route(h, g, wq, sk, tm):
    t = h.shape[0]
    row = lambda i: (i, 0)
    const2 = lambda i: (0, 0)
    n_hp = 2 * PEER_HEADS
    return pl.pallas_call(
        functools.partial(_route_kernel, tm=tm),
        grid=(t // tm,),
        in_specs=[
            pl.BlockSpec((tm, D_MODEL), row),
            pl.BlockSpec((1, D_MODEL), const2),
            pl.BlockSpec((D_MODEL, n_hp * D_HALF), const2),
            pl.BlockSpec((n_hp, N_KEYS, D_HALF), lambda i: (0, 0, 0)),
        ],
        out_specs=[pl.BlockSpec((tm, N_PICKS), row), pl.BlockSpec((tm, N_PICKS), row)],
        out_shape=[jax.ShapeDtypeStruct((t, N_PICKS), jnp.int32),
                   jax.ShapeDtypeStruct((t, N_PICKS), F32)],
        scratch_shapes=[
            pltpu.VMEM((n_hp, N_KEYS, tm), F32),
            pltpu.VMEM((n_hp, PEER_TOPK, tm), F32),
            pltpu.VMEM((n_hp, PEER_TOPK, tm), F32),
            pltpu.VMEM((CAND_ROWS, LANES), F32),
            pltpu.VMEM((PEER_TOPK, LANES), F32),
            pltpu.VMEM((PEER_TOPK, LANES), F32),
            pltpu.VMEM((N_PICKS, tm), F32),
            pltpu.VMEM((N_PICKS, tm), F32),
        ],
        compiler_params=_cparams(1),
        name="peer_route",
    )(h, g, wq, sk)


def _gather_kernel(idx_ref, gate_ref, h_ref, g_ref, gfin_ref, tab_ref, out_ref,
                   hf_s, buf, sem, *, tt, final_norm):
    hf_s[...] = _rms(h_ref[...], g_ref[...])
    n_sub = D_MODEL // LANES
    n_groups = tt // N_SLOTS
    n_pieces = 2 * WORD_ROWS
    picks_per_piece = N_PICKS // n_pieces

    def unpack(words):
        return (pltpu.bitcast(words << 16, F32), pltpu.bitcast(words & HIGH_HALF, F32))

    def pick_copy(e, slot, r):
        dst = buf.at[pl.ds(slot * SLOT_ROWS + r * PICK_PITCH, ROWS_PER_EXPERT), :]
        return pltpu.make_async_copy(tab_ref.at[e], dst, sem.at[slot])

    def issue(j, slot, r_lo, r_hi):
        for r in range(r_lo, r_hi):
            pick_copy(idx_ref[j, r], slot, r).start(priority=r % 2)

    def wait(slot):
        for r in range(N_PICKS):
            pick_copy(0, slot, r).wait()

    eye = (lax.broadcasted_iota(jnp.int32, (N_PICKS, N_PICKS), 0)
           == lax.broadcasted_iota(jnp.int32, (N_PICKS, N_PICKS), 1))
    sub = lax.broadcasted_iota(jnp.int32, (SUBLANES, LANES), 0)

    def token(k, x8, gate8, tiles, prefetch):
        base = k * SLOT_ROWS
        wait(k)
        piece = 0
        acc = jnp.zeros((N_PICKS, LANES), F32)
        for t in range(WORD_ROWS):
            u_lo, u_hi = unpack(buf[pl.ds(base + t, N_PICKS, stride=PICK_PITCH), :])
            acc = acc + u_lo * x8[t][k:k + 1, :] + u_hi * x8[WORD_ROWS + t][k:k + 1, :]
            prefetch(piece)
            piece += 1
        act = jax.nn.gelu(jnp.sum(acc, axis=1, keepdims=True))
        gcol = jnp.sum(jnp.where(eye, gate8[k:k + 1, :], 0.0), axis=1, keepdims=True)
        w = gcol * act
        new_tiles = list(tiles)
        for t in range(WORD_ROWS):
            v_lo, v_hi = unpack(buf[pl.ds(base + WORD_ROWS + t, N_PICKS, stride=PICK_PITCH), :])
            o_lo = jnp.sum(v_lo * w, axis=0, keepdims=True)
            o_hi = jnp.sum(v_hi * w, axis=0, keepdims=True)
            new_tiles[t] = jnp.where(sub == k, o_lo, tiles[t])
            new_tiles[WORD_ROWS + t] = jnp.where(sub == k, o_hi, tiles[WORD_ROWS + t])
            prefetch(piece)
            piece += 1
        return new_tiles

    def group(gi, last):
        r0 = pl.multiple_of(gi * N_SLOTS, N_SLOTS)
        x8 = [hf_s[pl.ds(r0, N_SLOTS), s * LANES:(s + 1) * LANES] for s in range(n_sub)]
        gate8 = gate_ref[pl.ds(r0, N_SLOTS), :]
        tiles = [jnp.zeros((SUBLANES, LANES), F32) for _ in range(n_sub)]
        for k in range(N_SLOTS):
            ahead = k + PREFETCH_DIST
            if last and ahead >= N_SLOTS:
                prefetch = lambda piece: None
            else:
                def prefetch(piece, ahead=ahead):
                    issue(r0 + ahead, ahead % N_SLOTS,
                          piece * picks_per_piece, (piece + 1) * picks_per_piece)
            tiles = token(k, x8, gate8, tiles, prefetch)
        res = [h_ref[pl.ds(r0, N_SLOTS), s * LANES:(s + 1) * LANES] + tiles[s] for s in range(n_sub)]
        if final_norm:
            ss = res[0] * res[0]
            for s in range(1, n_sub):
                ss = ss + res[s] * res[s]
            scale = lax.rsqrt(jnp.sum(ss, axis=1, keepdims=True) * (1.0 / D_MODEL) + EPS)
            res = [res[s] * scale * gfin_ref[:, s * LANES:(s + 1) * LANES] for s in range(n_sub)]
        for s in range(n_sub):
            out_ref[pl.ds(r0, N_SLOTS), s * LANES:(s + 1) * LANES] = res[s]

    for j in range(PREFETCH_DIST):
        issue(j, j, 0, N_PICKS)

    def body(gi, carry):
        group(gi, False)
        return carry

    lax.fori_loop(0, n_groups - 1, body, 0)
    group(n_groups - 1, True)


def _gather(idx, gate, h, g, gfin, tab, tt, final_norm):
    t = h.shape[0]
    row = lambda i: (i, 0)
    const2 = lambda i: (0, 0)
    return pl.pallas_call(
        functools.partial(_gather_kernel, tt=tt, final_norm=final_norm),
        grid=(t // tt,),
        in_specs=[
            pl.BlockSpec((tt, N_PICKS), row, memory_space=pltpu.SMEM),
            pl.BlockSpec((tt, N_PICKS), row),
            pl.BlockSpec((tt, D_MODEL), row),
            pl.BlockSpec((1, D_MODEL), const2),
            pl.BlockSpec((1, D_MODEL), const2),
            pl.BlockSpec(memory_space=pl.ANY),
        ],
        out_specs=pl.BlockSpec((tt, D_MODEL), row),
        out_shape=jax.ShapeDtypeStruct((t, D_MODEL), F32),
        scratch_shapes=[
            pltpu.VMEM((tt, D_MODEL), F32),
            pltpu.VMEM((N_SLOTS * SLOT_ROWS, LANES), jnp.int32),
            pltpu.SemaphoreType.DMA((N_SLOTS,)),
        ],
        compiler_params=_cparams(1),
        name="peer_experts",
    )(idx, gate, h, g, gfin, tab)


def _proj_kernel(h_ref, gkv_ref, gq_ref, wkv_ref, wq_ref, kv_ref, q_ref):
    h = h_ref[...]
    n = h * lax.rsqrt(jnp.mean(h * h, axis=-1, keepdims=True) + EPS)
    kv_ref[...] = jnp.dot((n * gkv_ref[...]).astype(BF16), wkv_ref[...], preferred_element_type=F32)
    q_ref[...] = jnp.dot((n * gq_ref[...]).astype(BF16), wq_ref[...],
                         preferred_element_type=F32).astype(BF16)


def _proj(h, gkv, gq, wkv, wq, tm):
    t = h.shape[0]
    row = lambda i: (i, 0)
    const2 = lambda i: (0, 0)
    return pl.pallas_call(
        _proj_kernel,
        grid=(t // tm,),
        in_specs=[
            pl.BlockSpec((tm, D_MODEL), row),
            pl.BlockSpec((1, D_MODEL), const2),
            pl.BlockSpec((1, D_MODEL), const2),
            pl.BlockSpec((D_MODEL, KV_DIM), const2),
            pl.BlockSpec((D_MODEL, D_MODEL), const2),
        ],
        out_specs=[pl.BlockSpec((tm, KV_DIM), row), pl.BlockSpec((tm, D_MODEL), row)],
        out_shape=[jax.ShapeDtypeStruct((t, KV_DIM), F32),
                   jax.ShapeDtypeStruct((t, D_MODEL), BF16)],
        compiler_params=_cparams(1),
        name="kv_q_proj",
    )(h, gkv, gq, wkv, wq)


def _alibi_slopes():
    return [2.0 ** (-8.0 * (h + 1) / N_HEADS) for h in range(N_HEADS)]


ATTN_KEYS = WINDOW + CHUNK
ATTN_PAIR = 2 * CHUNK
ATTN_ROW_BLOCK = 32


def _padded_heads(tile, g):
    low = lax.broadcasted_iota(jnp.int32, tile.shape, 1) < HEAD_DIM
    rolled = pltpu.roll(tile, HEAD_DIM, axis=1)
    if g % 2 == 0:
        pads = [jnp.where(low, tile, 0.0), jnp.where(low, 0.0, rolled)]
    else:
        pads = [jnp.where(low, rolled, 0.0), jnp.where(low, 0.0, tile)]
    return [p.astype(BF16) for p in pads]


def _attn_chunks(q_ref, windows, first_valids, sink_ref, s_scr, p_scr):
    half_kv = KV_DIM // 2
    slopes = _alibi_slopes()
    n_chunks = len(windows)

    def chain(c, g, half):
        return (c * N_KV_HEADS + g) * 2 + half

    for c in range(n_chunks):
        win = windows[c]()
        r0 = c * CHUNK
        for g in range(N_KV_HEADS):
            t = g // 2
            k_pad = _padded_heads(win[:, t * LANES:(t + 1) * LANES], g)
            qstack = jnp.concatenate(
                [q_ref[r0:r0 + CHUNK, (2 * g) * LANES:(2 * g + 1) * LANES],
                 q_ref[r0:r0 + CHUNK, (2 * g + 1) * LANES:(2 * g + 2) * LANES]], axis=0)
            for half in range(2):
                s_scr[chain(c, g, half)] = lax.dot_general(
                    qstack, k_pad[half], (((1,), (1,)), ((), ())), preferred_element_type=F32)

    rb = ATTN_ROW_BLOCK
    qi = lax.broadcasted_iota(jnp.int32, (rb, ATTN_KEYS), 0)
    kj = lax.broadcasted_iota(jnp.int32, (rb, ATTN_KEYS), 1)
    dists = [jnp.abs(WINDOW + qi + off - kj).astype(F32) for off in range(0, CHUNK, rb)]
    for c in range(n_chunks):
        valid = kj >= first_valids[c]
        for g in range(N_KV_HEADS):
            for half in range(2):
                ch = chain(c, g, half)
                for b in range(ATTN_PAIR // rb):
                    row0 = b * rb
                    head = GQA_GROUP * g + half + (2 if row0 >= CHUNK else 0)
                    s = s_scr[ch, row0:row0 + rb, :]
                    s = s * (HEAD_DIM ** -0.5) - slopes[head] * dists[(row0 % CHUNK) // rb]
                    s = jnp.where(valid, s, NEG)
                    sink = sink_ref[0, head]
                    m = jnp.maximum(jnp.max(s, axis=1, keepdims=True), sink)
                    e = jnp.exp(s - m)
                    denom = jnp.sum(e, axis=1, keepdims=True) + jnp.exp(sink - m)
                    p_scr[ch, row0:row0 + rb, :] = (e / denom).astype(BF16)

    outs = []
    for c in range(n_chunks):
        win = windows[c]()
        o_tiles = [None] * (N_HEADS // 2)
        for g in range(N_KV_HEADS):
            t = g // 2
            v_pad = _padded_heads(win[:, half_kv + t * LANES:half_kv + (t + 1) * LANES], g)
            o_stack = (jnp.dot(p_scr[chain(c, g, 0)], v_pad[0], preferred_element_type=F32)
                       + jnp.dot(p_scr[chain(c, g, 1)], v_pad[1], preferred_element_type=F32))
            o_tiles[2 * g] = o_stack[:CHUNK]
            o_tiles[2 * g + 1] = o_stack[CHUNK:]
        outs.append(jnp.concatenate(o_tiles, axis=1))
    return outs[0] if n_chunks == 1 else jnp.concatenate(outs, axis=0)


def _attn_scratch(n_chunks):
    n_chains = n_chunks * N_KV_HEADS * 2
    return [pltpu.VMEM((n_chains, ATTN_PAIR, ATTN_KEYS), F32),
            pltpu.VMEM((n_chains, ATTN_PAIR, ATTN_KEYS), BF16)]


def _attn_prompt_kernel(sink_ref, q_ref, kvp_ref, kvc_ref, h_ref, wo_ref, out_ref,
                        kv_s, s_scr, p_scr, *, qt):
    t = pl.program_id(1)
    kv_s[0:WINDOW, :] = kvp_ref[...]
    kv_s[WINDOW:WINDOW + qt, :] = kvc_ref[...]
    n_chunks = qt // CHUNK
    windows = [functools.partial(lambda r0: kv_s[r0:r0 + ATTN_KEYS, :], c * CHUNK)
               for c in range(n_chunks)]
    first_valids = [WINDOW - CHUNK * (t * n_chunks + c) for c in range(n_chunks)]
    o = _attn_chunks(q_ref, windows, first_valids, sink_ref, s_scr, p_scr).astype(BF16)
    out_ref[...] = h_ref[...] + jnp.dot(o, wo_ref[...], preferred_element_type=F32)


def _attn_prompt(sinks, q, kv, h, wo, batch, seq, qt):
    t = h.shape[0]
    tiles = seq // qt
    cur = lambda b, i: (b * tiles + i, 0)
    prev = lambda b, i: (jnp.maximum((b * tiles + i) * (qt // WINDOW) - 1, 0), 0)
    const2 = lambda b, i: (0, 0)
    return pl.pallas_call(
        functools.partial(_attn_prompt_kernel, qt=qt),
        grid=(batch, tiles),
        in_specs=[
            pl.BlockSpec((1, N_HEADS), const2, memory_space=pltpu.SMEM),
            pl.BlockSpec((qt, D_MODEL), cur),
            pl.BlockSpec((WINDOW, KV_DIM), prev),
            pl.BlockSpec((qt, KV_DIM), cur),
            pl.BlockSpec((qt, D_MODEL), cur),
            pl.BlockSpec((D_MODEL, D_MODEL), const2),
        ],
        out_specs=pl.BlockSpec((qt, D_MODEL), cur),
        out_shape=jax.ShapeDtypeStruct((t, D_MODEL), F32),
        scratch_shapes=[pltpu.VMEM((WINDOW + qt, KV_DIM), F32)] + _attn_scratch(qt // CHUNK),
        compiler_params=_cparams(2),
        name="attn_prompt",
    )(sinks, q, kv, kv, h, wo)


def _attn_sample_kernel(sink_ref, q_ref, ck_ref, cv_ref, kvn_ref, h_ref, wo_ref, out_ref,
                        s_scr, p_scr, *, nb):
    half_kv = KV_DIM // 2

    def window(b):
        kvn = kvn_ref[b * CHUNK:(b + 1) * CHUNK, :]
        k = jnp.concatenate([ck_ref[b * WINDOW:(b + 1) * WINDOW, :], kvn[:, :half_kv]], axis=0)
        v = jnp.concatenate([cv_ref[b * WINDOW:(b + 1) * WINDOW, :], kvn[:, half_kv:]], axis=0)
        return jnp.concatenate([k, v], axis=1)

    windows = [functools.partial(window, b) for b in range(nb)]
    o = _attn_chunks(q_ref, windows, [0] * nb, sink_ref, s_scr, p_scr).astype(BF16)
    out_ref[...] = h_ref[...] + jnp.dot(o, wo_ref[...], preferred_element_type=F32)


def _attn_sample(sinks, q, ck, cv, kv, h, wo, batch, nb):
    t = h.shape[0]
    row = lambda b: (b, 0)
    const2 = lambda b: (0, 0)
    return pl.pallas_call(
        functools.partial(_attn_sample_kernel, nb=nb),
        grid=(batch // nb,),
        in_specs=[
            pl.BlockSpec((1, N_HEADS), const2, memory_space=pltpu.SMEM),
            pl.BlockSpec((nb * CHUNK, D_MODEL), row),
            pl.BlockSpec((nb * WINDOW, KV_DIM // 2), row),
            pl.BlockSpec((nb * WINDOW, KV_DIM // 2), row),
            pl.BlockSpec((nb * CHUNK, KV_DIM), row),
            pl.BlockSpec((nb * CHUNK, D_MODEL), row),
            pl.BlockSpec((D_MODEL, D_MODEL), const2),
        ],
        scratch_shapes=_attn_scratch(nb),
        out_specs=pl.BlockSpec((nb * CHUNK, D_MODEL), row),
        out_shape=jax.ShapeDtypeStruct((t, D_MODEL), F32),
        compiler_params=_cparams(1),
        name="attn_sample",
    )(sinks, q, ck, cv, kv, h, wo)


def kernel(x_prompt, x_sample, cache_k_win, cache_v_win, norm_mix, norm_ffn, gmlp_w_in, gmlp_norm_v, gmlp_w_s, gmlp_b_s, gmlp_w_out, norm_kv, w_kv, attn_w_q, attn_sinks, attn_w_o, peer_w_q, peer_sub_keys, peer_u, peer_v, norm_final):
    batch, seq, _ = x_prompt.shape
    dec_batch, dec_seq, _ = x_sample.shape
    w_cache = cache_k_win.shape[1]
    assert dec_seq == CHUNK and w_cache == WINDOW and seq % MLP_CHUNK == 0

    row2 = lambda a: a.reshape(1, -1)
    win = gmlp_w_in[0].astype(BF16)
    wout = gmlp_w_out[0].astype(BF16)
    wkv = w_kv.astype(BF16)
    wq_attn = attn_w_q[0].astype(BF16)
    wo = attn_w_o[0].astype(BF16)
    wq_peer = peer_w_q.astype(BF16)
    sub_keys = peer_sub_keys.reshape(2, 2 * PEER_HEADS, N_KEYS, D_HALF).astype(BF16)
    def packed_words(rows):
        bits = lax.bitcast_convert_type(rows.astype(BF16), jnp.uint16).astype(jnp.uint32)
        words = bits[:, :D_MODEL // 2] | (bits[:, D_MODEL // 2:] << 16)
        return lax.bitcast_convert_type(words, jnp.int32).reshape(N_EXPERTS, WORD_ROWS, LANES)

    tables = [jnp.concatenate([packed_words(peer_u[l]), packed_words(peer_v[l])], axis=1)
              for l in range(2)]

    pos = jnp.arange(MLP_CHUNK)
    blk_i = pos[:, None] // CHUNK
    blk_j = pos[None, :] // CHUNK
    ws = gmlp_w_s[0]
    bs = gmlp_b_s[0]
    wmix_p = jnp.where((blk_j <= blk_i)[None], ws, 0.0).astype(BF16)
    ws64 = ws[:, :CHUNK, :CHUNK]
    wmix_s = jnp.where((blk_j == blk_i)[None], jnp.tile(ws64, (1, 2, 2)), 0.0).astype(BF16)
    bias_p = jnp.repeat(bs.T, LANES, axis=1)
    bias_s = jnp.repeat(jnp.tile(bs[:, :CHUNK], (1, 2)).T, LANES, axis=1)

    def trunk(x, prompt):
        t = x.shape[0]
        tm = 256
        tm_route = 512
        tt = 256
        h, v_rows = _gmlp(x, row2(norm_mix[0]), win, row2(gmlp_norm_v[0]),
                          wmix_p if prompt else wmix_s, bias_p if prompt else bias_s, wout, tm)
        idx, gate = _route(h, row2(norm_ffn[0]), wq_peer[0], sub_keys[0], tm_route)
        h = _gather(idx, gate, h, row2(norm_ffn[0]), row2(norm_final), tables[0], tt, False)
        kv, q = _proj(h, row2(norm_kv), row2(norm_mix[1]), wkv, wq_attn, tm)
        if prompt:
            h = _attn_prompt(attn_sinks, q, kv, h, wo, batch, seq, 256)
        else:
            ck = cache_k_win.reshape(dec_batch * w_cache, KV_DIM // 2)
            cv = cache_v_win.reshape(dec_batch * w_cache, KV_DIM // 2)
            h = _attn_sample(attn_sinks, q, ck, cv, kv, h, wo, dec_batch, 4)
        idx, gate = _route(h, row2(norm_ffn[1]), wq_peer[1], sub_keys[1], tm_route)
        y = _gather(idx, gate, h, row2(norm_ffn[1]), row2(norm_final), tables[1], tt, True)
        return y, kv, v_rows

    y_p, kv_p, _ = trunk(x_prompt.reshape(batch * seq, D_MODEL), True)
    y_s, kv_s, v_s = trunk(x_sample.reshape(dec_batch * dec_seq, D_MODEL), False)

    half = KV_DIM // 2
    kv_p = kv_p.reshape(batch, seq, KV_DIM)[:, seq - WINDOW:]
    prompt_k_win = kv_p[..., :half].reshape(batch, WINDOW, N_KV_HEADS, HEAD_DIM)
    prompt_v_win = kv_p[..., half:].reshape(batch, WINDOW, N_KV_HEADS, HEAD_DIM)
    kv_s = kv_s.reshape(dec_batch, dec_seq, KV_DIM)
    k_s = kv_s[..., :half].reshape(dec_batch, dec_seq, N_KV_HEADS, HEAD_DIM)
    v_s_new = kv_s[..., half:].reshape(dec_batch, dec_seq, N_KV_HEADS, HEAD_DIM)
    sample_k_win = jnp.concatenate([cache_k_win, k_s], axis=1)[:, -w_cache:]
    sample_v_win = jnp.concatenate([cache_v_win, v_s_new], axis=1)[:, -w_cache:]
    sample_gmlp_v = v_s.reshape(1, dec_batch, dec_seq, D_GATE)

    return (y_p.reshape(batch, seq, D_MODEL), y_s.reshape(dec_batch, dec_seq, D_MODEL),
            prompt_k_win, prompt_v_win, sample_k_win, sample_v_win, sample_gmlp_v)
```

```python
import functools
import math

import jax
import jax.numpy as jnp
from jax import lax
from jax.experimental import pallas as pl
from jax.experimental.pallas import tpu as pltpu

F32 = jnp.float32
BF16 = jnp.bfloat16

D_MODEL = 1024
CHUNK = 64
MLP_CHUNK = 128
GMLP_GROUPS = 8
D_GATE = D_MODEL
N_HEADS = 16
N_KV_HEADS = 4
HEAD_DIM = 64
GQA_GROUP = N_HEADS // N_KV_HEADS
WINDOW = 128
PEER_HEADS = 8
N_KEYS = 128
N_EXPERTS = N_KEYS * N_KEYS
PEER_TOPK = 16
D_HALF = 128
N_PICKS = PEER_HEADS * PEER_TOPK
EPS = 1e-6
NEG = -1e30
KV_DIM = 2 * N_KV_HEADS * HEAD_DIM

LANES = 128
SUBLANES = 8
WORD_ROWS = D_MODEL // 2 // LANES
ROWS_PER_EXPERT = 2 * WORD_ROWS
HIGH_HALF = -65536
PICK_PITCH = 12
SLOT_ROWS = N_PICKS * PICK_PITCH
N_SLOTS = SUBLANES
PREFETCH_DIST = 6

VMEM_LIMIT = 48 * 1024 * 1024


def _rms(x, g):
    return x * lax.rsqrt(jnp.mean(x * x, axis=-1, keepdims=True) + EPS) * g


def _cparams(n_axes):
    return pltpu.CompilerParams(
        dimension_semantics=("arbitrary",) * n_axes, vmem_limit_bytes=VMEM_LIMIT)


def _gmlp_kernel(x_ref, gmix_ref, win_ref, gv_ref, wmix_ref, bias_ref, wout_ref,
                 h_ref, v_ref, *, tm):
    x = x_ref[...]
    hn = _rms(x, gmix_ref[...])
    hmid = jax.nn.gelu(jnp.dot(hn.astype(BF16), win_ref[...], preferred_element_type=F32))
    u = hmid[:, :D_GATE]
    v = _rms(hmid[:, D_GATE:], gv_ref[...])
    v_ref[...] = v
    vb = v.astype(BF16)
    bias = bias_ref[...]
    outs = []
    for c in range(tm // MLP_CHUNK):
        r0 = c * MLP_CHUNK
        cols = []
        for g in range(GMLP_GROUPS):
            c0 = g * LANES
            cols.append(jnp.dot(wmix_ref[g], vb[r0:r0 + MLP_CHUNK, c0:c0 + LANES],
                                preferred_element_type=F32))
        mixed = jnp.concatenate(cols, axis=1) + bias
        outs.append(u[r0:r0 + MLP_CHUNK, :] * mixed)
    out = outs[0] if len(outs) == 1 else jnp.concatenate(outs, axis=0)
    y = jnp.dot(out.astype(BF16), wout_ref[...], preferred_element_type=F32)
    h_ref[...] = x + y


def _gmlp(x, gmix, win, gv, wmix, bias, wout, tm):
    t = x.shape[0]
    row = lambda i: (i, 0)
    const2 = lambda i: (0, 0)
    return pl.pallas_call(
        functools.partial(_gmlp_kernel, tm=tm),
        grid=(t // tm,),
        in_specs=[
            pl.BlockSpec((tm, D_MODEL), row),
            pl.BlockSpec((1, D_MODEL), const2),
            pl.BlockSpec((D_MODEL, 2 * D_GATE), const2),
            pl.BlockSpec((1, D_GATE), const2),
            pl.BlockSpec((GMLP_GROUPS, MLP_CHUNK, MLP_CHUNK), lambda i: (0, 0, 0)),
            pl.BlockSpec((MLP_CHUNK, D_GATE), const2),
            pl.BlockSpec((D_GATE, D_MODEL), const2),
        ],
        out_specs=[pl.BlockSpec((tm, D_MODEL), row), pl.BlockSpec((tm, D_GATE), row)],
        out_shape=[jax.ShapeDtypeStruct((t, D_MODEL), F32),
                   jax.ShapeDtypeStruct((t, D_GATE), F32)],
        compiler_params=_cparams(1),
        name="gmlp_mixer",
    )(x, gmix, win, gv, wmix, bias, wout)


CAND_COUNT = [PEER_TOPK // (a + 1) for a in range(PEER_TOPK)]
CAND_OFFSET = [sum(CAND_COUNT[:a]) for a in range(PEER_TOPK)]
CAND_USED = sum(CAND_COUNT)
CAND_ROWS = -(-CAND_USED // SUBLANES) * SUBLANES


def _route_kernel(h_ref, g_ref, wq_ref, sk_ref, idx_ref, gate_ref,
                  sc_s, ts_s, ti_s, cand_s, best_s, sel_s, e_s, gt_s, *, tm):
    hf = _rms(h_ref[...], g_ref[...])
    q = jnp.dot(hf.astype(BF16), wq_ref[...], preferred_element_type=F32).astype(BF16)
    n_hp = 2 * PEER_HEADS
    for hp in range(n_hp):
        sc_s[hp] = lax.dot_general(sk_ref[hp], q[:, hp * D_HALF:(hp + 1) * D_HALF],
                                   (((1,), (1,)), ((), ())), preferred_element_type=F32)

    n_rows = N_KEYS // SUBLANES
    sub_key = lax.broadcasted_iota(jnp.int32, (SUBLANES, LANES), 0).astype(F32)

    def stage1(hp, carry):
        for lg in range(tm // LANES):
            l0 = lg * LANES
            val = [sc_s[hp, v * SUBLANES:(v + 1) * SUBLANES, l0:l0 + LANES] for v in range(n_rows)]
            key = [sub_key + float(v * SUBLANES) for v in range(n_rows)]
            for rnd in range(n_rows):
                for a in range(rnd % 2, n_rows - 1, 2):
                    swap = val[a + 1] > val[a]
                    val[a], val[a + 1] = (jnp.where(swap, val[a + 1], val[a]),
                                          jnp.where(swap, val[a], val[a + 1]))
                    key[a], key[a + 1] = (jnp.where(swap, key[a + 1], key[a]),
                                          jnp.where(swap, key[a], key[a + 1]))
            for k in range(PEER_TOPK):
                m = jnp.max(val[0], axis=0, keepdims=True)
                i = jnp.min(jnp.where(val[0] == m, key[0], float(N_KEYS)), axis=0, keepdims=True)
                ts_s[hp, k:k + 1, l0:l0 + LANES] = m
                ti_s[hp, k:k + 1, l0:l0 + LANES] = i
                if k + 1 < PEER_TOPK:
                    taken = key[0] == i
                    for p in range(PEER_TOPK - 1 - k):
                        val[p] = jnp.where(taken, val[p + 1], val[p])
                        key[p] = jnp.where(taken, key[p + 1], key[p])
        return carry

    lax.fori_loop(0, n_hp, stage1, 0)

    n_cand = float(PEER_TOPK * PEER_TOPK)
    slot = lax.broadcasted_iota(jnp.int32, (CAND_ROWS, LANES), 0)
    cand_flat = slot
    for a in range(1, PEER_TOPK):
        step = (a * PEER_TOPK - CAND_OFFSET[a]) - ((a - 1) * PEER_TOPK - CAND_OFFSET[a - 1])
        cand_flat = cand_flat + jnp.where(slot >= CAND_OFFSET[a], step, 0)
    cand_flat = jnp.where(slot >= CAND_USED, PEER_TOPK * PEER_TOPK, cand_flat).astype(F32)

    def stage2(hd, carry):
        for lg in range(tm // LANES):
            l0 = lg * LANES
            s0 = ts_s[2 * hd, :, l0:l0 + LANES]
            s1 = ts_s[2 * hd + 1, :, l0:l0 + LANES]
            i0 = ti_s[2 * hd, :, l0:l0 + LANES]
            i1 = ti_s[2 * hd + 1, :, l0:l0 + LANES]
            for a in range(PEER_TOPK):
                n = CAND_COUNT[a]
                cand_s[CAND_OFFSET[a]:CAND_OFFSET[a] + n, :] = s0[a:a + 1, :] + s1[0:n, :]
            cand_s[CAND_USED:CAND_ROWS, :] = jnp.full((CAND_ROWS - CAND_USED, LANES), -jnp.inf, F32)
            cand = cand_s[...]
            for k in range(PEER_TOPK):
                m = jnp.max(cand, axis=0, keepdims=True)
                c = jnp.min(jnp.where(cand == m, cand_flat, n_cand), axis=0, keepdims=True)
                best_s[k:k + 1, :] = m
                sel_s[k:k + 1, :] = c
                cand = jnp.where(cand_flat == c, -jnp.inf, cand)
            best = best_s[...]
            sel = sel_s[...]
            sel_a = jnp.floor(sel * (1.0 / PEER_TOPK))
            sel_b = sel - sel_a * PEER_TOPK
            e0 = jnp.zeros_like(sel)
            e1 = jnp.zeros_like(sel)
            for a in range(PEER_TOPK):
                e0 = jnp.where(sel_a == float(a), i0[a:a + 1, :], e0)
                e1 = jnp.where(sel_b == float(a), i1[a:a + 1, :], e1)
            ex = jnp.exp(best - best[0:1, :])
            gate = ex / jnp.sum(ex, axis=0, keepdims=True)
            r0 = pl.multiple_of(hd * PEER_TOPK, PEER_TOPK)
            e_s[pl.ds(r0, PEER_TOPK), l0:l0 + LANES] = e0 * float(N_KEYS) + e1
            gt_s[pl.ds(r0, PEER_TOPK), l0:l0 + LANES] = gate
        return carry

    lax.fori_loop(0, PEER_HEADS, stage2, 0)

    idx_ref[...] = jnp.transpose(e_s[...]).astype(jnp.int32)
    gate_ref[...] = jnp.transpose(gt_s[...])


def _route(h, g, wq, sk, tm):
    t = h.shape[0]
    row = lambda i: (i, 0)
    const2 = lambda i: (0, 0)
    n_hp = 2 * PEER_HEADS
    return pl.pallas_call(
        functools.partial(_route_kernel, tm=tm),
        grid=(t // tm,),
        in_specs=[
            pl.BlockSpec((tm, D_MODEL), row),
            pl.BlockSpec((1, D_MODEL), const2),
            pl.BlockSpec((D_MODEL, n_hp * D_HALF), const2),
            pl.BlockSpec((n_hp, N_KEYS, D_HALF), lambda i: (0, 0, 0)),
        ],
        out_specs=[pl.BlockSpec((tm, N_PICKS), row), pl.BlockSpec((tm, N_PICKS), row)],
        out_shape=[jax.ShapeDtypeStruct((t, N_PICKS), jnp.int32),
                   jax.ShapeDtypeStruct((t, N_PICKS), F32)],
        scratch_shapes=[
            pltpu.VMEM((n_hp, N_KEYS, tm), F32),
            pltpu.VMEM((n_hp, PEER_TOPK, tm), F32),
            pltpu.VMEM((n_hp, PEER_TOPK, tm), F32),
            pltpu.VMEM((CAND_ROWS, LANES), F32),
            pltpu.VMEM((PEER_TOPK, LANES), F32),
            pltpu.VMEM((PEER_TOPK, LANES), F32),
            pltpu.VMEM((N_PICKS, tm), F32),
            pltpu.VMEM((N_PICKS, tm), F32),
        ],
        compiler_params=_cparams(1),
        name="peer_route",
    )(h, g, wq, sk)


def _pack_kernel(u_ref, v_ref, o_ref):
    n = u_ref.shape[0]
    half = D_MODEL // 2
    for src, row0 in ((u_ref, 0), (v_ref, WORD_ROWS)):
        bits = pltpu.bitcast(src[...].astype(BF16).astype(F32), jnp.uint32)
        for t in range(WORD_ROWS):
            lo = bits[:, t * LANES:(t + 1) * LANES] >> 16
            hi = bits[:, half + t * LANES:half + (t + 1) * LANES] & jnp.uint32(0xFFFF0000)
            o_ref[pl.ds(row0 + t, n, stride=ROWS_PER_EXPERT), :] = pltpu.bitcast(lo | hi, jnp.int32)


def _pack_experts(u, v):
    n_blk = 256
    row = lambda i: (i, 0)
    out = pl.pallas_call(
        _pack_kernel,
        grid=(N_EXPERTS // n_blk,),
        in_specs=[pl.BlockSpec((n_blk, D_MODEL), row), pl.BlockSpec((n_blk, D_MODEL), row)],
        out_specs=pl.BlockSpec((n_blk * ROWS_PER_EXPERT, LANES), row),
        out_shape=jax.ShapeDtypeStruct((N_EXPERTS * ROWS_PER_EXPERT, LANES), jnp.int32),
        compiler_params=_cparams(1),
        name="pack_experts",
    )(u, v)
    return out.reshape(N_EXPERTS, ROWS_PER_EXPERT, LANES)


def _gather_kernel(idx_ref, gate_ref, h_ref, g_ref, gfin_ref, tab_ref, out_ref,
                   hf_s, buf, sem, *, tt, final_norm):
    hf_s[...] = _rms(h_ref[...], g_ref[...])
    n_sub = D_MODEL // LANES
    n_groups = tt // N_SLOTS
    n_pieces = 2 * WORD_ROWS
    picks_per_piece = N_PICKS // n_pieces

    def unpack(words):
        return (pltpu.bitcast(words << 16, F32), pltpu.bitcast(words & HIGH_HALF, F32))

    def pick_copy(e, slot, r):
        dst = buf.at[pl.ds(slot * SLOT_ROWS + r * PICK_PITCH, ROWS_PER_EXPERT), :]
        return pltpu.make_async_copy(tab_ref.at[e], dst, sem.at[slot])

    def issue(j, slot, r_lo, r_hi):
        for r in range(r_lo, r_hi):
            pick_copy(idx_ref[j, r], slot, r).start(priority=r % 2)

    def wait(slot):
        for r in range(N_PICKS):
            pick_copy(0, slot, r).wait()

    eye = (lax.broadcasted_iota(jnp.int32, (N_PICKS, N_PICKS), 0)
           == lax.broadcasted_iota(jnp.int32, (N_PICKS, N_PICKS), 1))
    sub = lax.broadcasted_iota(jnp.int32, (SUBLANES, LANES), 0)

    def token(k, x8, gate8, tiles, prefetch):
        base = k * SLOT_ROWS
        wait(k)
        piece = 0
        acc = jnp.zeros((N_PICKS, LANES), F32)
        for t in range(WORD_ROWS):
            u_lo, u_hi = unpack(buf[pl.ds(base + t, N_PICKS, stride=PICK_PITCH), :])
            acc = acc + u_lo * x8[t][k:k + 1, :] + u_hi * x8[WORD_ROWS + t][k:k + 1, :]
            prefetch(piece)
            piece += 1
        act = jax.nn.gelu(jnp.sum(acc, axis=1, keepdims=True))
        gcol = jnp.sum(jnp.where(eye, gate8[k:k + 1, :], 0.0), axis=1, keepdims=True)
        w = gcol * act
        new_tiles = list(tiles)
        for t in range(WORD_ROWS):
            v_lo, v_hi = unpack(buf[pl.ds(base + WORD_ROWS + t, N_PICKS, stride=PICK_PITCH), :])
            o_lo = jnp.sum(v_lo * w, axis=0, keepdims=True)
            o_hi = jnp.sum(v_hi * w, axis=0, keepdims=True)
            new_tiles[t] = jnp.where(sub == k, o_lo, tiles[t])
            new_tiles[WORD_ROWS + t] = jnp.where(sub == k, o_hi, tiles[WORD_ROWS + t])
            prefetch(piece)
            piece += 1
        return new_tiles

    def group(gi, last):
        r0 = pl.multiple_of(gi * N_SLOTS, N_SLOTS)
        x8 = [hf_s[pl.ds(r0, N_SLOTS), s * LANES:(s + 1) * LANES] for s in range(n_sub)]
        gate8 = gate_ref[pl.ds(r0, N_SLOTS), :]
        tiles = [jnp.zeros((SUBLANES, LANES), F32) for _ in range(n_sub)]
        for k in range(N_SLOTS):
            ahead = k + PREFETCH_DIST
            if last and ahead >= N_SLOTS:
                prefetch = lambda piece: None
            else:
                def prefetch(piece, ahead=ahead):
                    issue(r0 + ahead, ahead % N_SLOTS,
                          piece * picks_per_piece, (piece + 1) * picks_per_piece)
            tiles = token(k, x8, gate8, tiles, prefetch)
        res = [h_ref[pl.ds(r0, N_SLOTS), s * LANES:(s + 1) * LANES] + tiles[s] for s in range(n_sub)]
        if final_norm:
            ss = res[0] * res[0]
            for s in range(1, n_sub):
                ss = ss + res[s] * res[s]
            scale = lax.rsqrt(jnp.sum(ss, axis=1, keepdims=True) * (1.0 / D_MODEL) + EPS)
            res = [res[s] * scale * gfin_ref[:, s * LANES:(s + 1) * LANES] for s in range(n_sub)]
        for s in range(n_sub):
            out_ref[pl.ds(r0, N_SLOTS), s * LANES:(s + 1) * LANES] = res[s]

    for j in range(PREFETCH_DIST):
        issue(j, j, 0, N_PICKS)

    def body(gi, carry):
        group(gi, False)
        return carry

    lax.fori_loop(0, n_groups - 1, body, 0)
    group(n_groups - 1, True)


def _gather(idx, gate, h, g, gfin, tab, tt, final_norm):
    t = h.shape[0]
    row = lambda i: (i, 0)
    const2 = lambda i: (0, 0)
    return pl.pallas_call(
        functools.partial(_gather_kernel, tt=tt, final_norm=final_norm),
        grid=(t // tt,),
        in_specs=[
            pl.BlockSpec((tt, N_PICKS), row, memory_space=pltpu.SMEM),
            pl.BlockSpec((tt, N_PICKS), row),
            pl.BlockSpec((tt, D_MODEL), row),
            pl.BlockSpec((1, D_MODEL), const2),
            pl.BlockSpec((1, D_MODEL), const2),
            pl.BlockSpec(memory_space=pl.ANY),
        ],
        out_specs=pl.BlockSpec((tt, D_MODEL), row),
        out_shape=jax.ShapeDtypeStruct((t, D_MODEL), F32),
        scratch_shapes=[
            pltpu.VMEM((tt, D_MODEL), F32),
            pltpu.VMEM((N_SLOTS * SLOT_ROWS, LANES), jnp.int32),
            pltpu.SemaphoreType.DMA((N_SLOTS,)),
        ],
        compiler_params=_cparams(1),
        name="peer_experts",
    )(idx, gate, h, g, gfin, tab)


def _proj_kernel(h_ref, gkv_ref, gq_ref, wkv_ref, wq_ref, kv_ref, q_ref):
    h = h_ref[...]
    n = h * lax.rsqrt(jnp.mean(h * h, axis=-1, keepdims=True) + EPS)
    kv_ref[...] = jnp.dot((n * gkv_ref[...]).astype(BF16), wkv_ref[...], preferred_element_type=F32)
    q_ref[...] = jnp.dot((n * gq_ref[...]).astype(BF16), wq_ref[...],
                         preferred_element_type=F32).astype(BF16)


def _proj(h, gkv, gq, wkv, wq, tm):
    t = h.shape[0]
    row = lambda i: (i, 0)
    const2 = lambda i: (0, 0)
    return pl.pallas_call(
        _proj_kernel,
        grid=(t // tm,),
        in_specs=[
            pl.BlockSpec((tm, D_MODEL), row),
            pl.BlockSpec((1, D_MODEL), const2),
            pl.BlockSpec((1, D_MODEL), const2),
            pl.BlockSpec((D_MODEL, KV_DIM), const2),
            pl.BlockSpec((D_MODEL, D_MODEL), const2),
        ],
        out_specs=[pl.BlockSpec((tm, KV_DIM), row), pl.BlockSpec((tm, D_MODEL), row)],
        out_shape=[jax.ShapeDtypeStruct((t, KV_DIM), F32),
                   jax.ShapeDtypeStruct((t, D_MODEL), BF16)],
        compiler_params=_cparams(1),
        name="kv_q_proj",
    )(h, gkv, gq, wkv, wq)


def _alibi_slopes():
    return [2.0 ** (-8.0 * (h + 1) / N_HEADS) for h in range(N_HEADS)]


ATTN_KEYS = WINDOW + CHUNK
ATTN_PAIR = 2 * CHUNK
ATTN_ROW_BLOCK = 32


def _padded_heads(tile, g):
    low = lax.broadcasted_iota(jnp.int32, tile.shape, 1) < HEAD_DIM
    rolled = pltpu.roll(tile, HEAD_DIM, axis=1)
    if g % 2 == 0:
        pads = [jnp.where(low, tile, 0.0), jnp.where(low, 0.0, rolled)]
    else:
        pads = [jnp.where(low, rolled, 0.0), jnp.where(low, 0.0, tile)]
    return [p.astype(BF16) for p in pads]


def _attn_chunks(q_ref, windows, first_valids, sink_ref, s_scr, p_scr):
    half_kv = KV_DIM // 2
    slopes = _alibi_slopes()
    n_chunks = len(windows)

    def chain(c, g, half):
        return (c * N_KV_HEADS + g) * 2 + half

    for c in range(n_chunks):
        win = windows[c]()
        r0 = c * CHUNK
        for g in range(N_KV_HEADS):
            t = g // 2
            k_pad = _padded_heads(win[:, t * LANES:(t + 1) * LANES], g)
            qstack = jnp.concatenate(
                [q_ref[r0:r0 + CHUNK, (2 * g) * LANES:(2 * g + 1) * LANES],
                 q_ref[r0:r0 + CHUNK, (2 * g + 1) * LANES:(2 * g + 2) * LANES]], axis=0)
            for half in range(2):
                s_scr[chain(c, g, half)] = lax.dot_general(
                    qstack, k_pad[half], (((1,), (1,)), ((), ())), preferred_element_type=F32)

    rb = ATTN_ROW_BLOCK
    qi = lax.broadcasted_iota(jnp.int32, (rb, ATTN_KEYS), 0)
    kj = lax.broadcasted_iota(jnp.int32, (rb, ATTN_KEYS), 1)
    dists = [jnp.abs(WINDOW + qi + off - kj).astype(F32) for off in range(0, CHUNK, rb)]
    for c in range(n_chunks):
        valid = kj >= first_valids[c]
        for g in range(N_KV_HEADS):
            for half in range(2):
                ch = chain(c, g, half)
                for b in range(ATTN_PAIR // rb):
                    row0 = b * rb
                    head = GQA_GROUP * g + half + (2 if row0 >= CHUNK else 0)
                    s = s_scr[ch, row0:row0 + rb, :]
                    s = s * (HEAD_DIM ** -0.5) - slopes[head] * dists[(row0 % CHUNK) // rb]
                    s = jnp.where(valid, s, NEG)
                    sink = sink_ref[0, head]
                    m = jnp.maximum(jnp.max(s, axis=1, keepdims=True), sink)
                    e = jnp.exp(s - m)
                    denom = jnp.sum(e, axis=1, keepdims=True) + jnp.exp(sink - m)
                    p_scr[ch, row0:row0 + rb, :] = (e / denom).astype(BF16)

    outs = []
    for c in range(n_chunks):
        win = windows[c]()
        o_tiles = [None] * (N_HEADS // 2)
        for g in range(N_KV_HEADS):
            t = g // 2
            v_pad = _padded_heads(win[:, half_kv + t * LANES:half_kv + (t + 1) * LANES], g)
            o_stack = (jnp.dot(p_scr[chain(c, g, 0)], v_pad[0], preferred_element_type=F32)
                       + jnp.dot(p_scr[chain(c, g, 1)], v_pad[1], preferred_element_type=F32))
            o_tiles[2 * g] = o_stack[:CHUNK]
            o_tiles[2 * g + 1] = o_stack[CHUNK:]
        outs.append(jnp.concatenate(o_tiles, axis=1))
    return outs[0] if n_chunks == 1 else jnp.concatenate(outs, axis=0)


def _attn_scratch(n_chunks):
    n_chains = n_chunks * N_KV_HEADS * 2
    return [pltpu.VMEM((n_chains, ATTN_PAIR, ATTN_KEYS), F32),
            pltpu.VMEM((n_chains, ATTN_PAIR, ATTN_KEYS), BF16)]


def _attn_prompt_kernel(sink_ref, q_ref, kvp_ref, kvc_ref, h_ref, wo_ref, out_ref,
                        kv_s, s_scr, p_scr, *, qt):
    t = pl.program_id(1)
    kv_s[0:WINDOW, :] = kvp_ref[...]
    kv_s[WINDOW:WINDOW + qt, :] = kvc_ref[...]
    n_chunks = qt // CHUNK
    windows = [functools.partial(lambda r0: kv_s[r0:r0 + ATTN_KEYS, :], c * CHUNK)
               for c in range(n_chunks)]
    first_valids = [WINDOW - CHUNK * (t * n_chunks + c) for c in range(n_chunks)]
    o = _attn_chunks(q_ref, windows, first_valids, sink_ref, s_scr, p_scr).astype(BF16)
    out_ref[...] = h_ref[...] + jnp.dot(o, wo_ref[...], preferred_element_type=F32)


def _attn_prompt(sinks, q, kv, h, wo, batch, seq, qt):
    t = h.shape[0]
    tiles = seq // qt
    cur = lambda b, i: (b * tiles + i, 0)
    prev = lambda b, i: (jnp.maximum((b * tiles + i) * (qt // WINDOW) - 1, 0), 0)
    const2 = lambda b, i: (0, 0)
    return pl.pallas_call(
        functools.partial(_attn_prompt_kernel, qt=qt),
        grid=(batch, tiles),
        in_specs=[
            pl.BlockSpec((1, N_HEADS), const2, memory_space=pltpu.SMEM),
            pl.BlockSpec((qt, D_MODEL), cur),
            pl.BlockSpec((WINDOW, KV_DIM), prev),
            pl.BlockSpec((qt, KV_DIM), cur),
            pl.BlockSpec((qt, D_MODEL), cur),
            pl.BlockSpec((D_MODEL, D_MODEL), const2),
        ],
        out_specs=pl.BlockSpec((qt, D_MODEL), cur),
        out_shape=jax.ShapeDtypeStruct((t, D_MODEL), F32),
        scratch_shapes=[pltpu.VMEM((WINDOW + qt, KV_DIM), F32)] + _attn_scratch(qt // CHUNK),
        compiler_params=_cparams(2),
        name="attn_prompt",
    )(sinks, q, kv, kv, h, wo)


def _attn_sample_kernel(sink_ref, q_ref, ck_ref, cv_ref, kvn_ref, h_ref, wo_ref, out_ref,
                        s_scr, p_scr, *, nb):
    half_kv = KV_DIM // 2

    def window(b):
        kvn = kvn_ref[b * CHUNK:(b + 1) * CHUNK, :]
        k = jnp.concatenate([ck_ref[b * WINDOW:(b + 1) * WINDOW, :], kvn[:, :half_kv]], axis=0)
        v = jnp.concatenate([cv_ref[b * WINDOW:(b + 1) * WINDOW, :], kvn[:, half_kv:]], axis=0)
        return jnp.concatenate([k, v], axis=1)

    windows = [functools.partial(window, b) for b in range(nb)]
    o = _attn_chunks(q_ref, windows, [0] * nb, sink_ref, s_scr, p_scr).astype(BF16)
    out_ref[...] = h_ref[...] + jnp.dot(o, wo_ref[...], preferred_element_type=F32)


def _attn_sample(sinks, q, ck, cv, kv, h, wo, batch, nb):
    t = h.shape[0]
    row = lambda b: (b, 0)
    const2 = lambda b: (0, 0)
    return pl.pallas_call(
        functools.partial(_attn_sample_kernel, nb=nb),
        grid=(batch // nb,),
        in_specs=[
            pl.BlockSpec((1, N_HEADS), const2, memory_space=pltpu.SMEM),
            pl.BlockSpec((nb * CHUNK, D_MODEL), row),
            pl.BlockSpec((nb * WINDOW, KV_DIM // 2), row),
            pl.BlockSpec((nb * WINDOW, KV_DIM // 2), row),
            pl.BlockSpec((nb * CHUNK, KV_DIM), row),
            pl.BlockSpec((nb * CHUNK, D_MODEL), row),
            pl.BlockSpec((D_MODEL, D_MODEL), const2),
        ],
        scratch_shapes=_attn_scratch(nb),
        out_specs=pl.BlockSpec((nb * CHUNK, D_MODEL), row),
        out_shape=jax.ShapeDtypeStruct((t, D_MODEL), F32),
        compiler_params=_cparams(1),
        name="attn_sample",
    )(sinks, q, ck, cv, kv, h, wo)


def kernel(x_prompt, x_sample, cache_k_win, cache_v_win, norm_mix, norm_ffn, gmlp_w_in, gmlp_norm_v, gmlp_w_s, gmlp_b_s, gmlp_w_out, norm_kv, w_kv, attn_w_q, attn_sinks, attn_w_o, peer_w_q, peer_sub_keys, peer_u, peer_v, norm_final):
    batch, seq, _ = x_prompt.shape
    dec_batch, dec_seq, _ = x_sample.shape
    w_cache = cache_k_win.shape[1]
    assert dec_seq == CHUNK and w_cache == WINDOW and seq % MLP_CHUNK == 0

    row2 = lambda a: a.reshape(1, -1)
    win = gmlp_w_in[0].astype(BF16)
    wout = gmlp_w_out[0].astype(BF16)
    wkv = w_kv.astype(BF16)
    wq_attn = attn_w_q[0].astype(BF16)
    wo = attn_w_o[0].astype(BF16)
    wq_peer = peer_w_q.astype(BF16)
    sub_keys = peer_sub_keys.reshape(2, 2 * PEER_HEADS, N_KEYS, D_HALF).astype(BF16)
    tables = [_pack_experts(peer_u[l], peer_v[l]) for l in range(2)]

    pos = jnp.arange(MLP_CHUNK)
    blk_i = pos[:, None] // CHUNK
    blk_j = pos[None, :] // CHUNK
    ws = gmlp_w_s[0]
    bs = gmlp_b_s[0]
    wmix_p = jnp.where((blk_j <= blk_i)[None], ws, 0.0).astype(BF16)
    ws64 = ws[:, :CHUNK, :CHUNK]
    wmix_s = jnp.where((blk_j == blk_i)[None], jnp.tile(ws64, (1, 2, 2)), 0.0).astype(BF16)
    bias_p = jnp.repeat(bs.T, LANES, axis=1)
    bias_s = jnp.repeat(jnp.tile(bs[:, :CHUNK], (1, 2)).T, LANES, axis=1)

    def trunk(x, prompt):
        t = x.shape[0]
        tm = 256
        tm_route = 512
        tt = 256
        h, v_rows = _gmlp(x, row2(norm_mix[0]), win, row2(gmlp_norm_v[0]),
                          wmix_p if prompt else wmix_s, bias_p if prompt else bias_s, wout, tm)
        idx, gate = _route(h, row2(norm_ffn[0]), wq_peer[0], sub_keys[0], tm_route)
        h = _gather(idx, gate, h, row2(norm_ffn[0]), row2(norm_final), tables[0], tt, False)
        kv, q = _proj(h, row2(norm_kv), row2(norm_mix[1]), wkv, wq_attn, tm)
        if prompt:
            h = _attn_prompt(attn_sinks, q, kv, h, wo, batch, seq, 256)
        else:
            ck = cache_k_win.reshape(dec_batch * w_cache, KV_DIM // 2)
            cv = cache_v_win.reshape(dec_batch * w_cache, KV_DIM // 2)
            h = _attn_sample(attn_sinks, q, ck, cv, kv, h, wo, dec_batch, 4)
        idx, gate = _route(h, row2(norm_ffn[1]), wq_peer[1], sub_keys[1], tm_route)
        y = _gather(idx, gate, h, row2(norm_ffn[1]), row2(norm_final), tables[1], tt, True)
        return y, kv, v_rows

    y_p, kv_p, _ = trunk(x_prompt.reshape(batch * seq, D_MODEL), True)
    y_s, kv_s, v_s = trunk(x_sample.reshape(dec_batch * dec_seq, D_MODEL), False)

    half = KV_DIM // 2
    kv_p = kv_p.reshape(batch, seq, KV_DIM)[:, seq - WINDOW:]
    prompt_k_win = kv_p[..., :half].reshape(batch, WINDOW, N_KV_HEADS, HEAD_DIM)
    prompt_v_win = kv_p[..., half:].reshape(batch, WINDOW, N_KV_HEADS, HEAD_DIM)
    kv_s = kv_s.reshape(dec_batch, dec_seq, KV_DIM)
    k_s = kv_s[..., :half].reshape(dec_batch, dec_seq, N_KV_HEADS, HEAD_DIM)
    v_s_new = kv_s[..., half:].reshape(dec_batch, dec_seq, N_KV_HEADS, HEAD_DIM)
    sample_k_win = jnp.concatenate([cache_k_win, k_s], axis=1)[:, -w_cache:]
    sample_v_win = jnp.concatenate([cache_v_win, v_s_new], axis=1)[:, -w_cache:]
    sample_gmlp_v = v_s.reshape(1, dec_batch, dec_seq, D_GATE)

    return (y_p.reshape(batch, seq, D_MODEL), y_s.reshape(dec_batch, dec_seq, D_MODEL),
            prompt_k_win, prompt_v_win, sample_k_win, sample_v_win, sample_gmlp_v)
```

```python
import functools
import math

import jax
import jax.numpy as jnp
from jax import lax
from jax.experimental import pallas as pl
from jax.experimental.pallas import tpu as pltpu

F32 = jnp.float32
BF16 = jnp.bfloat16

D_MODEL = 1024
CHUNK = 64
MLP_CHUNK = 128
GMLP_GROUPS = 8
D_GATE = D_MODEL
N_HEADS = 16
N_KV_HEADS = 4
HEAD_DIM = 64
GQA_GROUP = N_HEADS // N_KV_HEADS
WINDOW = 128
PEER_HEADS = 8
N_KEYS = 128
N_EXPERTS = N_KEYS * N_KEYS
PEER_TOPK = 16
D_HALF = 128
N_PICKS = PEER_HEADS * PEER_TOPK
EPS = 1e-6
NEG = -1e30
KV_DIM = 2 * N_KV_HEADS * HEAD_DIM

LANES = 128
SUBLANES = 8
WORD_ROWS = D_MODEL // 2 // LANES
ROWS_PER_EXPERT = 2 * WORD_ROWS
HIGH_HALF = -65536
PICK_PITCH = 12
SLOT_ROWS = N_PICKS * PICK_PITCH
N_SLOTS = SUBLANES
PREFETCH_DIST = 6

VMEM_LIMIT = 48 * 1024 * 1024


def _rms(x, g):
    return x * lax.rsqrt(jnp.mean(x * x, axis=-1, keepdims=True) + EPS) * g


def _cparams(n_axes):
    return pltpu.CompilerParams(
        dimension_semantics=("arbitrary",) * n_axes, vmem_limit_bytes=VMEM_LIMIT)


def _gmlp_kernel(x_ref, gmix_ref, win_ref, gv_ref, wmix_ref, bias_ref, wout_ref,
                 h_ref, v_ref, *, tm):
    x = x_ref[...]
    hn = _rms(x, gmix_ref[...])
    hmid = jax.nn.gelu(jnp.dot(hn.astype(BF16), win_ref[...], preferred_element_type=F32))
    u = hmid[:, :D_GATE]
    v = _rms(hmid[:, D_GATE:], gv_ref[...])
    v_ref[...] = v
    vb = v.astype(BF16)
    bias = bias_ref[...]
    outs = []
    for c in range(tm // MLP_CHUNK):
        r0 = c * MLP_CHUNK
        cols = []
        for g in range(GMLP_GROUPS):
            c0 = g * LANES
            cols.append(jnp.dot(wmix_ref[g], vb[r0:r0 + MLP_CHUNK, c0:c0 + LANES],
                                preferred_element_type=F32))
        mixed = jnp.concatenate(cols, axis=1) + bias
        outs.append(u[r0:r0 + MLP_CHUNK, :] * mixed)
    out = outs[0] if len(outs) == 1 else jnp.concatenate(outs, axis=0)
    y = jnp.dot(out.astype(BF16), wout_ref[...], preferred_element_type=F32)
    h_ref[...] = x + y


def _gmlp(x, gmix, win, gv, wmix, bias, wout, tm):
    t = x.shape[0]
    row = lambda i: (i, 0)
    const2 = lambda i: (0, 0)
    return pl.pallas_call(
        functools.partial(_gmlp_kernel, tm=tm),
        grid=(t // tm,),
        in_specs=[
            pl.BlockSpec((tm, D_MODEL), row),
            pl.BlockSpec((1, D_MODEL), const2),
            pl.BlockSpec((D_MODEL, 2 * D_GATE), const2),
            pl.BlockSpec((1, D_GATE), const2),
            pl.BlockSpec((GMLP_GROUPS, MLP_CHUNK, MLP_CHUNK), lambda i: (0, 0, 0)),
            pl.BlockSpec((MLP_CHUNK, D_GATE), const2),
            pl.BlockSpec((D_GATE, D_MODEL), const2),
        ],
        out_specs=[pl.BlockSpec((tm, D_MODEL), row), pl.BlockSpec((tm, D_GATE), row)],
        out_shape=[jax.ShapeDtypeStruct((t, D_MODEL), F32),
                   jax.ShapeDtypeStruct((t, D_GATE), F32)],
        compiler_params=_cparams(1),
        name="gmlp_mixer",
    )(x, gmix, win, gv, wmix, bias, wout)


CAND_COUNT = [PEER_TOPK // (a + 1) for a in range(PEER_TOPK)]
CAND_OFFSET = [sum(CAND_COUNT[:a]) for a in range(PEER_TOPK)]
CAND_USED = sum(CAND_COUNT)
CAND_ROWS = -(-CAND_USED // SUBLANES) * SUBLANES


def _route_kernel(h_ref, g_ref, wq_ref, sk_ref, idx_ref, gate_ref,
                  sc_s, ts_s, ti_s, cand_s, best_s, sel_s, e_s, gt_s, *, tm):
    hf = _rms(h_ref[...], g_ref[...])
    q = jnp.dot(hf.astype(BF16), wq_ref[...], preferred_element_type=F32).astype(BF16)
    n_hp = 2 * PEER_HEADS
    for hp in range(n_hp):
        sc_s[hp] = lax.dot_general(sk_ref[hp], q[:, hp * D_HALF:(hp + 1) * D_HALF],
                                   (((1,), (1,)), ((), ())), preferred_element_type=F32)

    n_rows = N_KEYS // SUBLANES
    sub_key = lax.broadcasted_iota(jnp.int32, (SUBLANES, LANES), 0).astype(F32)

    def stage1(hp, carry):
        for lg in range(tm // LANES):
            l0 = lg * LANES
            val = [sc_s[hp, v * SUBLANES:(v + 1) * SUBLANES, l0:l0 + LANES] for v in range(n_rows)]
            key = [sub_key + float(v * SUBLANES) for v in range(n_rows)]
            for rnd in range(n_rows):
                for a in range(rnd % 2, n_rows - 1, 2):
                    swap = val[a + 1] > val[a]
                    val[a], val[a + 1] = (jnp.where(swap, val[a + 1], val[a]),
                                          jnp.where(swap, val[a], val[a + 1]))
                    key[a], key[a + 1] = (jnp.where(swap, key[a + 1], key[a]),
                                          jnp.where(swap, key[a], key[a + 1]))
            for k in range(PEER_TOPK):
                m = jnp.max(val[0], axis=0, keepdims=True)
                i = jnp.min(jnp.where(val[0] == m, key[0], float(N_KEYS)), axis=0, keepdims=True)
                ts_s[hp, k:k + 1, l0:l0 + LANES] = m
                ti_s[hp, k:k + 1, l0:l0 + LANES] = i
                if k + 1 < PEER_TOPK:
                    taken = key[0] == i
                    for p in range(PEER_TOPK - 1 - k):
                        val[p] = jnp.where(taken, val[p + 1], val[p])
                        key[p] = jnp.where(taken, key[p + 1], key[p])
        return carry

    lax.fori_loop(0, n_hp, stage1, 0)

    n_cand = float(PEER_TOPK * PEER_TOPK)
    slot = lax.broadcasted_iota(jnp.int32, (CAND_ROWS, LANES), 0)
    cand_flat = slot
    for a in range(1, PEER_TOPK):
        step = (a * PEER_TOPK - CAND_OFFSET[a]) - ((a - 1) * PEER_TOPK - CAND_OFFSET[a - 1])
        cand_flat = cand_flat + jnp.where(slot >= CAND_OFFSET[a], step, 0)
    cand_flat = jnp.where(slot >= CAND_USED, PEER_TOPK * PEER_TOPK, cand_flat).astype(F32)

    def stage2(hd, carry):
        for lg in range(tm // LANES):
            l0 = lg * LANES
            s0 = ts_s[2 * hd, :, l0:l0 + LANES]
            s1 = ts_s[2 * hd + 1, :, l0:l0 + LANES]
            i0 = ti_s[2 * hd, :, l0:l0 + LANES]
            i1 = ti_s[2 * hd + 1, :, l0:l0 + LANES]
            for a in range(PEER_TOPK):
                n = CAND_COUNT[a]
                cand_s[CAND_OFFSET[a]:CAND_OFFSET[a] + n, :] = s0[a:a + 1, :] + s1[0:n, :]
            cand_s[CAND_USED:CAND_ROWS, :] = jnp.full((CAND_ROWS - CAND_USED, LANES), -jnp.inf, F32)
            cand = cand_s[...]
            for k in range(PEER_TOPK):
                m = jnp.max(cand, axis=0, keepdims=True)
                c = jnp.min(jnp.where(cand == m, cand_flat, n_cand), axis=0, keepdims=True)
                best_s[k:k + 1, :] = m
                sel_s[k:k + 1, :] = c
                cand = jnp.where(cand_flat == c, -jnp.inf, cand)
            best = best_s[...]
            sel = sel_s[...]
            sel_a = jnp.floor(sel * (1.0 / PEER_TOPK))
            sel_b = sel - sel_a * PEER_TOPK
            e0 = jnp.zeros_like(sel)
            e1 = jnp.zeros_like(sel)
            for a in range(PEER_TOPK):
                e0 = jnp.where(sel_a == float(a), i0[a:a + 1, :], e0)
                e1 = jnp.where(sel_b == float(a), i1[a:a + 1, :], e1)
            ex = jnp.exp(best - best[0:1, :])
            gate = ex / jnp.sum(ex, axis=0, keepdims=True)
            r0 = pl.multiple_of(hd * PEER_TOPK, PEER_TOPK)
            e_s[pl.ds(r0, PEER_TOPK), l0:l0 + LANES] = e0 * float(N_KEYS) + e1
            gt_s[pl.ds(r0, PEER_TOPK), l0:l0 + LANES] = gate
        return carry

    lax.fori_loop(0, PEER_HEADS, stage2, 0)

    idx_ref[...] = jnp.transpose(e_s[...]).astype(jnp.int32)
    gate_ref[...] = jnp.transpose(gt_s[...])


def _route(h, g, wq, sk, tm):
    t = h.shape[0]
    row = lambda i: (i, 0)
    const2 = lambda i: (0, 0)
    n_hp = 2 * PEER_HEADS
    return pl.pallas_call(
        functools.partial(_route_kernel, tm=tm),
        grid=(t // tm,),
        in_specs=[
            pl.BlockSpec((tm, D_MODEL), row),
            pl.BlockSpec((1, D_MODEL), const2),
            pl.BlockSpec((D_MODEL, n_hp * D_HALF), const2),
            pl.BlockSpec((n_hp, N_KEYS, D_HALF), lambda i: (0, 0, 0)),
        ],
        out_specs=[pl.BlockSpec((tm, N_PICKS), row), pl.BlockSpec((tm, N_PICKS), row)],
        out_shape=[jax.ShapeDtypeStruct((t, N_PICKS), jnp.int32),
                   jax.ShapeDtypeStruct((t, N_PICKS), F32)],
        scratch_shapes=[
            pltpu.VMEM((n_hp, N_KEYS, tm), F32),
            pltpu.VMEM((n_hp, PEER_TOPK, tm), F32),
            pltpu.VMEM((n_hp, PEER_TOPK, tm), F32),
            pltpu.VMEM((CAND_ROWS, LANES), F32),
            pltpu.VMEM((PEER_TOPK, LANES), F32),
            pltpu.VMEM((PEER_TOPK, LANES), F32),
            pltpu.VMEM((N_PICKS, tm), F32),
            pltpu.VMEM((N_PICKS, tm), F32),
        ],
        compiler_params=_cparams(1),
        name="peer_route",
    )(h, g, wq, sk)


def _pack_kernel(u_ref, v_ref, o_ref):
    n = u_ref.shape[0]
    half = D_MODEL // 2
    for src, row0 in ((u_ref, 0), (v_ref, WORD_ROWS)):
        bits = pltpu.bitcast(src[...].astype(BF16).astype(F32), jnp.uint32)
        for t in range(WORD_ROWS):
            lo = bits[:, t * LANES:(t + 1) * LANES] >> 16
            hi = bits[:, half + t * LANES:half + (t + 1) * LANES] & jnp.uint32(0xFFFF0000)
            o_ref[pl.ds(row0 + t, n, stride=ROWS_PER_EXPERT), :] = pltpu.bitcast(lo | hi, jnp.int32)


def _pack_experts(u, v, layer):
    n_blk = 256
    row = lambda i: (i, 0)
    layer_row = lambda i: (layer, i, 0)
    out = pl.pallas_call(
        _pack_kernel,
        grid=(N_EXPERTS // n_blk,),
        in_specs=[pl.BlockSpec((None, n_blk, D_MODEL), layer_row),
                  pl.BlockSpec((None, n_blk, D_MODEL), layer_row)],
        out_specs=pl.BlockSpec((n_blk * ROWS_PER_EXPERT, LANES), row),
        out_shape=jax.ShapeDtypeStruct((N_EXPERTS * ROWS_PER_EXPERT, LANES), jnp.int32),
        compiler_params=_cparams(1),
        name="pack_experts",
    )(u, v)
    return out.reshape(N_EXPERTS, ROWS_PER_EXPERT, LANES)


def _gather_kernel(idx_ref, gate_ref, h_ref, g_ref, gfin_ref, tab_ref, out_ref,
                   hf_s, buf, sem, *, tt, final_norm):
    hf_s[...] = _rms(h_ref[...], g_ref[...])
    n_sub = D_MODEL // LANES
    n_groups = tt // N_SLOTS
    n_pieces = 2 * WORD_ROWS
    picks_per_piece = N_PICKS // n_pieces

    def unpack(words):
        return (pltpu.bitcast(words << 16, F32), pltpu.bitcast(words & HIGH_HALF, F32))

    def pick_copy(e, slot, r):
        dst = buf.at[pl.ds(slot * SLOT_ROWS + r * PICK_PITCH, ROWS_PER_EXPERT), :]
        return pltpu.make_async_copy(tab_ref.at[e], dst, sem.at[slot])

    def issue(j, slot, r_lo, r_hi):
        for r in range(r_lo, r_hi):
            pick_copy(idx_ref[j, r], slot, r).start(priority=r % 2)

    def wait(slot):
        for r in range(N_PICKS):
            pick_copy(0, slot, r).wait()

    eye = (lax.broadcasted_iota(jnp.int32, (N_PICKS, N_PICKS), 0)
           == lax.broadcasted_iota(jnp.int32, (N_PICKS, N_PICKS), 1))
    sub = lax.broadcasted_iota(jnp.int32, (SUBLANES, LANES), 0)

    def token(k, x8, gate8, tiles, prefetch):
        base = k * SLOT_ROWS
        wait(k)
        piece = 0
        acc = jnp.zeros((N_PICKS, LANES), F32)
        for t in range(WORD_ROWS):
            u_lo, u_hi = unpack(buf[pl.ds(base + t, N_PICKS, stride=PICK_PITCH), :])
            acc = acc + u_lo * x8[t][k:k + 1, :] + u_hi * x8[WORD_ROWS + t][k:k + 1, :]
            prefetch(piece)
            piece += 1
        act = jax.nn.gelu(jnp.sum(acc, axis=1, keepdims=True))
        gcol = jnp.sum(jnp.where(eye, gate8[k:k + 1, :], 0.0), axis=1, keepdims=True)
        w = gcol * act
        new_tiles = list(tiles)
        for t in range(WORD_ROWS):
            v_lo, v_hi = unpack(buf[pl.ds(base + WORD_ROWS + t, N_PICKS, stride=PICK_PITCH), :])
            o_lo = jnp.sum(v_lo * w, axis=0, keepdims=True)
            o_hi = jnp.sum(v_hi * w, axis=0, keepdims=True)
            new_tiles[t] = jnp.where(sub == k, o_lo, tiles[t])
            new_tiles[WORD_ROWS + t] = jnp.where(sub == k, o_hi, tiles[WORD_ROWS + t])
            prefetch(piece)
            piece += 1
        return new_tiles

    def group(gi, last):
        r0 = pl.multiple_of(gi * N_SLOTS, N_SLOTS)
        x8 = [hf_s[pl.ds(r0, N_SLOTS), s * LANES:(s + 1) * LANES] for s in range(n_sub)]
        gate8 = gate_ref[pl.ds(r0, N_SLOTS), :]
        tiles = [jnp.zeros((SUBLANES, LANES), F32) for _ in range(n_sub)]
        for k in range(N_SLOTS):
            ahead = k + PREFETCH_DIST
            if last and ahead >= N_SLOTS:
                prefetch = lambda piece: None
            else:
                def prefetch(piece, ahead=ahead):
                    issue(r0 + ahead, ahead % N_SLOTS,
                          piece * picks_per_piece, (piece + 1) * picks_per_piece)
            tiles = token(k, x8, gate8, tiles, prefetch)
        res = [h_ref[pl.ds(r0, N_SLOTS), s * LANES:(s + 1) * LANES] + tiles[s] for s in range(n_sub)]
        if final_norm:
            ss = res[0] * res[0]
            for s in range(1, n_sub):
                ss = ss + res[s] * res[s]
            scale = lax.rsqrt(jnp.sum(ss, axis=1, keepdims=True) * (1.0 / D_MODEL) + EPS)
            res = [res[s] * scale * gfin_ref[:, s * LANES:(s + 1) * LANES] for s in range(n_sub)]
        for s in range(n_sub):
            out_ref[pl.ds(r0, N_SLOTS), s * LANES:(s + 1) * LANES] = res[s]

    for j in range(PREFETCH_DIST):
        issue(j, j, 0, N_PICKS)

    def body(gi, carry):
        group(gi, False)
        return carry

    lax.fori_loop(0, n_groups - 1, body, 0)
    group(n_groups - 1, True)


def _gather(idx, gate, h, g, gfin, tab, tt, final_norm):
    t = h.shape[0]
    row = lambda i: (i, 0)
    const2 = lambda i: (0, 0)
    return pl.pallas_call(
        functools.partial(_gather_kernel, tt=tt, final_norm=final_norm),
        grid=(t // tt,),
        in_specs=[
            pl.BlockSpec((tt, N_PICKS), row, memory_space=pltpu.SMEM),
            pl.BlockSpec((tt, N_PICKS), row),
            pl.BlockSpec((tt, D_MODEL), row),
            pl.BlockSpec((1, D_MODEL), const2),
            pl.BlockSpec((1, D_MODEL), const2),
            pl.BlockSpec(memory_space=pl.ANY),
        ],
        out_specs=pl.BlockSpec((tt, D_MODEL), row),
        out_shape=jax.ShapeDtypeStruct((t, D_MODEL), F32),
        scratch_shapes=[
            pltpu.VMEM((tt, D_MODEL), F32),
            pltpu.VMEM((N_SLOTS * SLOT_ROWS, LANES), jnp.int32),
            pltpu.SemaphoreType.DMA((N_SLOTS,)),
        ],
        compiler_params=_cparams(1),
        name="peer_experts",
    )(idx, gate, h, g, gfin, tab)


def _proj_kernel(h_ref, gkv_ref, gq_ref, wkv_ref, wq_ref, kv_ref, q_ref):
    h = h_ref[...]
    n = h * lax.rsqrt(jnp.mean(h * h, axis=-1, keepdims=True) + EPS)
    kv_ref[...] = jnp.dot((n * gkv_ref[...]).astype(BF16), wkv_ref[...], preferred_element_type=F32)
    q_ref[...] = jnp.dot((n * gq_ref[...]).astype(BF16), wq_ref[...],
                         preferred_element_type=F32).astype(BF16)


def _proj(h, gkv, gq, wkv, wq, tm):
    t = h.shape[0]
    row = lambda i: (i, 0)
    const2 = lambda i: (0, 0)
    return pl.pallas_call(
        _proj_kernel,
        grid=(t // tm,),
        in_specs=[
            pl.BlockSpec((tm, D_MODEL), row),
            pl.BlockSpec((1, D_MODEL), const2),
            pl.BlockSpec((1, D_MODEL), const2),
            pl.BlockSpec((D_MODEL, KV_DIM), const2),
            pl.BlockSpec((D_MODEL, D_MODEL), const2),
        ],
        out_specs=[pl.BlockSpec((tm, KV_DIM), row), pl.BlockSpec((tm, D_MODEL), row)],
        out_shape=[jax.ShapeDtypeStruct((t, KV_DIM), F32),
                   jax.ShapeDtypeStruct((t, D_MODEL), BF16)],
        compiler_params=_cparams(1),
        name="kv_q_proj",
    )(h, gkv, gq, wkv, wq)


def _alibi_slopes():
    return [2.0 ** (-8.0 * (h + 1) / N_HEADS) for h in range(N_HEADS)]


ATTN_KEYS = WINDOW + CHUNK
ATTN_PAIR = 2 * CHUNK
ATTN_ROW_BLOCK = 32


def _padded_heads(tile, g):
    low = lax.broadcasted_iota(jnp.int32, tile.shape, 1) < HEAD_DIM
    rolled = pltpu.roll(tile, HEAD_DIM, axis=1)
    if g % 2 == 0:
        pads = [jnp.where(low, tile, 0.0), jnp.where(low, 0.0, rolled)]
    else:
        pads = [jnp.where(low, rolled, 0.0), jnp.where(low, 0.0, tile)]
    return [p.astype(BF16) for p in pads]


def _attn_chunks(q_ref, windows, first_valids, sink_ref, s_scr, p_scr):
    half_kv = KV_DIM // 2
    slopes = _alibi_slopes()
    n_chunks = len(windows)

    def chain(c, g, half):
        return (c * N_KV_HEADS + g) * 2 + half

    for c in range(n_chunks):
        win = windows[c]()
        r0 = c * CHUNK
        for g in range(N_KV_HEADS):
            t = g // 2
            k_pad = _padded_heads(win[:, t * LANES:(t + 1) * LANES], g)
            qstack = jnp.concatenate(
                [q_ref[r0:r0 + CHUNK, (2 * g) * LANES:(2 * g + 1) * LANES],
                 q_ref[r0:r0 + CHUNK, (2 * g + 1) * LANES:(2 * g + 2) * LANES]], axis=0)
            for half in range(2):
                s_scr[chain(c, g, half)] = lax.dot_general(
                    qstack, k_pad[half], (((1,), (1,)), ((), ())), preferred_element_type=F32)

    rb = ATTN_ROW_BLOCK
    qi = lax.broadcasted_iota(jnp.int32, (rb, ATTN_KEYS), 0)
    kj = lax.broadcasted_iota(jnp.int32, (rb, ATTN_KEYS), 1)
    dists = [jnp.abs(WINDOW + qi + off - kj).astype(F32) for off in range(0, CHUNK, rb)]
    for c in range(n_chunks):
        valid = kj >= first_valids[c]
        for g in range(N_KV_HEADS):
            for half in range(2):
                ch = chain(c, g, half)
                for b in range(ATTN_PAIR // rb):
                    row0 = b * rb
                    head = GQA_GROUP * g + half + (2 if row0 >= CHUNK else 0)
                    s = s_scr[ch, row0:row0 + rb, :]
                    s = s * (HEAD_DIM ** -0.5) - slopes[head] * dists[(row0 % CHUNK) // rb]
                    s = jnp.where(valid, s, NEG)
                    sink = sink_ref[0, head]
                    m = jnp.maximum(jnp.max(s, axis=1, keepdims=True), sink)
                    e = jnp.exp(s - m)
                    denom = jnp.sum(e, axis=1, keepdims=True) + jnp.exp(sink - m)
                    p_scr[ch, row0:row0 + rb, :] = (e / denom).astype(BF16)

    outs = []
    for c in range(n_chunks):
        win = windows[c]()
        o_tiles = [None] * (N_HEADS // 2)
        for g in range(N_KV_HEADS):
            t = g // 2
            v_pad = _padded_heads(win[:, half_kv + t * LANES:half_kv + (t + 1) * LANES], g)
            o_stack = (jnp.dot(p_scr[chain(c, g, 0)], v_pad[0], preferred_element_type=F32)
                       + jnp.dot(p_scr[chain(c, g, 1)], v_pad[1], preferred_element_type=F32))
            o_tiles[2 * g] = o_stack[:CHUNK]
            o_tiles[2 * g + 1] = o_stack[CHUNK:]
        outs.append(jnp.concatenate(o_tiles, axis=1))
    return outs[0] if n_chunks == 1 else jnp.concatenate(outs, axis=0)


def _attn_scratch(n_chunks):
    n_chains = n_chunks * N_KV_HEADS * 2
    return [pltpu.VMEM((n_chains, ATTN_PAIR, ATTN_KEYS), F32),
            pltpu.VMEM((n_chains, ATTN_PAIR, ATTN_KEYS), BF16)]


def _attn_prompt_kernel(sink_ref, q_ref, kvp_ref, kvc_ref, h_ref, wo_ref, out_ref,
                        kv_s, s_scr, p_scr, *, qt):
    t = pl.program_id(1)
    kv_s[0:WINDOW, :] = kvp_ref[...]
    kv_s[WINDOW:WINDOW + qt, :] = kvc_ref[...]
    n_chunks = qt // CHUNK
    windows = [functools.partial(lambda r0: kv_s[r0:r0 + ATTN_KEYS, :], c * CHUNK)
               for c in range(n_chunks)]
    first_valids = [WINDOW - CHUNK * (t * n_chunks + c) for c in range(n_chunks)]
    o = _attn_chunks(q_ref, windows, first_valids, sink_ref, s_scr, p_scr).astype(BF16)
    out_ref[...] = h_ref[...] + jnp.dot(o, wo_ref[...], preferred_element_type=F32)


def _attn_prompt(sinks, q, kv, h, wo, batch, seq, qt):
    t = h.shape[0]
    tiles = seq // qt
    cur = lambda b, i: (b * tiles + i, 0)
    prev = lambda b, i: (jnp.maximum((b * tiles + i) * (qt // WINDOW) - 1, 0), 0)
    const2 = lambda b, i: (0, 0)
    return pl.pallas_call(
        functools.partial(_attn_prompt_kernel, qt=qt),
        grid=(batch, tiles),
        in_specs=[
            pl.BlockSpec((1, N_HEADS), const2, memory_space=pltpu.SMEM),
            pl.BlockSpec((qt, D_MODEL), cur),
            pl.BlockSpec((WINDOW, KV_DIM), prev),
            pl.BlockSpec((qt, KV_DIM), cur),
            pl.BlockSpec((qt, D_MODEL), cur),
            pl.BlockSpec((D_MODEL, D_MODEL), const2),
        ],
        out_specs=pl.BlockSpec((qt, D_MODEL), cur),
        out_shape=jax.ShapeDtypeStruct((t, D_MODEL), F32),
        scratch_shapes=[pltpu.VMEM((WINDOW + qt, KV_DIM), F32)] + _attn_scratch(qt // CHUNK),
        compiler_params=_cparams(2),
        name="attn_prompt",
    )(sinks, q, kv, kv, h, wo)


def _attn_sample_kernel(sink_ref, q_ref, ck_ref, cv_ref, kvn_ref, h_ref, wo_ref, out_ref,
                        s_scr, p_scr, *, nb):
    half_kv = KV_DIM // 2

    def window(b):
        kvn = kvn_ref[b * CHUNK:(b + 1) * CHUNK, :]
        k = jnp.concatenate([ck_ref[b * WINDOW:(b + 1) * WINDOW, :], kvn[:, :half_kv]], axis=0)
        v = jnp.concatenate([cv_ref[b * WINDOW:(b + 1) * WINDOW, :], kvn[:, half_kv:]], axis=0)
        return jnp.concatenate([k, v], axis=1)

    windows = [functools.partial(window, b) for b in range(nb)]
    o = _attn_chunks(q_ref, windows, [0] * nb, sink_ref, s_scr, p_scr).astype(BF16)
    out_ref[...] = h_ref[...] + jnp.dot(o, wo_ref[...], preferred_element_type=F32)


def _attn_sample(sinks, q, ck, cv, kv, h, wo, batch, nb):
    t = h.shape[0]
    row = lambda b: (b, 0)
    const2 = lambda b: (0, 0)
    return pl.pallas_call(
        functools.partial(_attn_sample_kernel, nb=nb),
        grid=(batch // nb,),
        in_specs=[
            pl.BlockSpec((1, N_HEADS), const2, memory_space=pltpu.SMEM),
            pl.BlockSpec((nb * CHUNK, D_MODEL), row),
            pl.BlockSpec((nb * WINDOW, KV_DIM // 2), row),
            pl.BlockSpec((nb * WINDOW, KV_DIM // 2), row),
            pl.BlockSpec((nb * CHUNK, KV_DIM), row),
            pl.BlockSpec((nb * CHUNK, D_MODEL), row),
            pl.BlockSpec((D_MODEL, D_MODEL), const2),
        ],
        scratch_shapes=_attn_scratch(nb),
        out_specs=pl.BlockSpec((nb * CHUNK, D_MODEL), row),
        out_shape=jax.ShapeDtypeStruct((t, D_MODEL), F32),
        compiler_params=_cparams(1),
        name="attn_sample",
    )(sinks, q, ck, cv, kv, h, wo)


def kernel(x_prompt, x_sample, cache_k_win, cache_v_win, norm_mix, norm_ffn, gmlp_w_in, gmlp_norm_v, gmlp_w_s, gmlp_b_s, gmlp_w_out, norm_kv, w_kv, attn_w_q, attn_sinks, attn_w_o, peer_w_q, peer_sub_keys, peer_u, peer_v, norm_final):
    batch, seq, _ = x_prompt.shape
    dec_batch, dec_seq, _ = x_sample.shape
    w_cache = cache_k_win.shape[1]
    assert dec_seq == CHUNK and w_cache == WINDOW and seq % MLP_CHUNK == 0

    row2 = lambda a: a.reshape(1, -1)
    win = gmlp_w_in[0].astype(BF16)
    wout = gmlp_w_out[0].astype(BF16)
    wkv = w_kv.astype(BF16)
    wq_attn = attn_w_q[0].astype(BF16)
    wo = attn_w_o[0].astype(BF16)
    wq_peer = peer_w_q.astype(BF16)
    sub_keys = peer_sub_keys.reshape(2, 2 * PEER_HEADS, N_KEYS, D_HALF).astype(BF16)
    tables = [_pack_experts(peer_u, peer_v, l) for l in range(2)]

    pos = jnp.arange(MLP_CHUNK)
    blk_i = pos[:, None] // CHUNK
    blk_j = pos[None, :] // CHUNK
    ws = gmlp_w_s[0]
    bs = gmlp_b_s[0]
    wmix_p = jnp.where((blk_j <= blk_i)[None], ws, 0.0).astype(BF16)
    ws64 = ws[:, :CHUNK, :CHUNK]
    wmix_s = jnp.where((blk_j == blk_i)[None], jnp.tile(ws64, (1, 2, 2)), 0.0).astype(BF16)
    bias_p = jnp.repeat(bs.T, LANES, axis=1)
    bias_s = jnp.repeat(jnp.tile(bs[:, :CHUNK], (1, 2)).T, LANES, axis=1)

    def trunk(x, prompt):
        t = x.shape[0]
        tm = 256
        tm_route = 512
        tt = 256
        h, v_rows = _gmlp(x, row2(norm_mix[0]), win, row2(gmlp_norm_v[0]),
                          wmix_p if prompt else wmix_s, bias_p if prompt else bias_s, wout, tm)
        idx, gate = _route(h, row2(norm_ffn[0]), wq_peer[0], sub_keys[0], tm_route)
        h = _gather(idx, gate, h, row2(norm_ffn[0]), row2(norm_final), tables[0], tt, False)
        kv, q = _proj(h, row2(norm_kv), row2(norm_mix[1]), wkv, wq_attn, tm)
        if prompt:
            h = _attn_prompt(attn_sinks, q, kv, h, wo, batch, seq, 256)
        else:
            ck = cache_k_win.reshape(dec_batch * w_cache, KV_DIM // 2)
            cv = cache_v_win.reshape(dec_batch * w_cache, KV_DIM // 2)
            h = _attn_sample(attn_sinks, q, ck, cv, kv, h, wo, dec_batch, 4)
        idx, gate = _route(h, row2(norm_ffn[1]), wq_peer[1], sub_keys[1], tm_route)
        y = _gather(idx, gate, h, row2(norm_ffn[1]), row2(norm_final), tables[1], tt, True)
        return y, kv, v_rows

    y_p, kv_p, _ = trunk(x_prompt.reshape(batch * seq, D_MODEL), True)
    y_s, kv_s, v_s = trunk(x_sample.reshape(dec_batch * dec_seq, D_MODEL), False)

    half = KV_DIM // 2
    kv_p = kv_p.reshape(batch, seq, KV_DIM)[:, seq - WINDOW:]
    prompt_k_win = kv_p[..., :half].reshape(batch, WINDOW, N_KV_HEADS, HEAD_DIM)
    prompt_v_win = kv_p[..., half:].reshape(batch, WINDOW, N_KV_HEADS, HEAD_DIM)
    kv_s = kv_s.reshape(dec_batch, dec_seq, KV_DIM)
    k_s = kv_s[..., :half].reshape(dec_batch, dec_seq, N_KV_HEADS, HEAD_DIM)
    v_s_new = kv_s[..., half:].reshape(dec_batch, dec_seq, N_KV_HEADS, HEAD_DIM)
    sample_k_win = jnp.concatenate([cache_k_win, k_s], axis=1)[:, -w_cache:]
    sample_v_win = jnp.concatenate([cache_v_win, v_s_new], axis=1)[:, -w_cache:]
    sample_gmlp_v = v_s.reshape(1, dec_batch, dec_seq, D_GATE)

    return (y_p.reshape(batch, seq, D_MODEL), y_s.reshape(dec_batch, dec_seq, D_MODEL),
            prompt_k_win, prompt_v_win, sample_k_win, sample_v_win, sample_gmlp_v)
```

```python
import functools
import math

import jax
import jax.numpy as jnp
from jax import lax
from jax.experimental import pallas as pl
from jax.experimental.pallas import tpu as pltpu

F32 = jnp.float32
BF16 = jnp.bfloat16

D_MODEL = 1024
CHUNK = 64
MLP_CHUNK = 128
GMLP_GROUPS = 8
D_GATE = D_MODEL
N_HEADS = 16
N_KV_HEADS = 4
HEAD_DIM = 64
GQA_GROUP = N_HEADS // N_KV_HEADS
WINDOW = 128
PEER_HEADS = 8
N_KEYS = 128
N_EXPERTS = N_KEYS * N_KEYS
PEER_TOPK = 16
D_HALF = 128
N_PICKS = PEER_HEADS * PEER_TOPK
EPS = 1e-6
NEG = -1e30
KV_DIM = 2 * N_KV_HEADS * HEAD_DIM

LANES = 128
SUBLANES = 8
WORD_ROWS = D_MODEL // 2 // LANES
ROWS_PER_EXPERT = 2 * WORD_ROWS
HIGH_HALF = -65536
PICK_PITCH = 12
SLOT_ROWS = N_PICKS * PICK_PITCH
N_SLOTS = SUBLANES
PREFETCH_DIST = 6

VMEM_LIMIT = 48 * 1024 * 1024


def _rms(x, g):
    return x * lax.rsqrt(jnp.mean(x * x, axis=-1, keepdims=True) + EPS) * g


def _cparams(n_axes):
    return pltpu.CompilerParams(
        dimension_semantics=("arbitrary",) * n_axes, vmem_limit_bytes=VMEM_LIMIT)


def _gmlp_kernel(x_ref, gmix_ref, win_ref, gv_ref, wmix_ref, bias_ref, wout_ref,
                 h_ref, v_ref, *, tm):
    x = x_ref[...]
    hn = _rms(x, gmix_ref[...])
    hmid = jax.nn.gelu(jnp.dot(hn.astype(BF16), win_ref[...], preferred_element_type=F32))
    u = hmid[:, :D_GATE]
    v = _rms(hmid[:, D_GATE:], gv_ref[...])
    v_ref[...] = v
    vb = v.astype(BF16)
    bias = bias_ref[...]
    outs = []
    for c in range(tm // MLP_CHUNK):
        r0 = c * MLP_CHUNK
        cols = []
        for g in range(GMLP_GROUPS):
            c0 = g * LANES
            cols.append(jnp.dot(wmix_ref[g], vb[r0:r0 + MLP_CHUNK, c0:c0 + LANES],
                                preferred_element_type=F32))
        mixed = jnp.concatenate(cols, axis=1) + bias
        outs.append(u[r0:r0 + MLP_CHUNK, :] * mixed)
    out = outs[0] if len(outs) == 1 else jnp.concatenate(outs, axis=0)
    y = jnp.dot(out.astype(BF16), wout_ref[...], preferred_element_type=F32)
    h_ref[...] = x + y


def _gmlp(x, gmix, win, gv, wmix, bias, wout, tm):
    t = x.shape[0]
    row = lambda i: (i, 0)
    const2 = lambda i: (0, 0)
    return pl.pallas_call(
        functools.partial(_gmlp_kernel, tm=tm),
        grid=(t // tm,),
        in_specs=[
            pl.BlockSpec((tm, D_MODEL), row),
            pl.BlockSpec((1, D_MODEL), const2),
            pl.BlockSpec((D_MODEL, 2 * D_GATE), const2),
            pl.BlockSpec((1, D_GATE), const2),
            pl.BlockSpec((GMLP_GROUPS, MLP_CHUNK, MLP_CHUNK), lambda i: (0, 0, 0)),
            pl.BlockSpec((MLP_CHUNK, D_GATE), const2),
            pl.BlockSpec((D_GATE, D_MODEL), const2),
        ],
        out_specs=[pl.BlockSpec((tm, D_MODEL), row), pl.BlockSpec((tm, D_GATE), row)],
        out_shape=[jax.ShapeDtypeStruct((t, D_MODEL), F32),
                   jax.ShapeDtypeStruct((t, D_GATE), F32)],
        compiler_params=_cparams(1),
        name="gmlp_mixer",
    )(x, gmix, win, gv, wmix, bias, wout)


CAND_COUNT = [PEER_TOPK // (a + 1) for a in range(PEER_TOPK)]
CAND_OFFSET = [sum(CAND_COUNT[:a]) for a in range(PEER_TOPK)]
CAND_USED = sum(CAND_COUNT)
CAND_ROWS = -(-CAND_USED // SUBLANES) * SUBLANES


def _route_kernel(h_ref, g_ref, wq_ref, sk_ref, idx_ref, gate_ref,
                  sc_s, ts_s, ti_s, cand_s, best_s, sel_s, e_s, gt_s, *, tm):
    hf = _rms(h_ref[...], g_ref[...])
    q = jnp.dot(hf.astype(BF16), wq_ref[...], preferred_element_type=F32).astype(BF16)
    n_hp = 2 * PEER_HEADS
    for hp in range(n_hp):
        sc_s[hp] = lax.dot_general(sk_ref[hp], q[:, hp * D_HALF:(hp + 1) * D_HALF],
                                   (((1,), (1,)), ((), ())), preferred_element_type=F32)

    n_rows = N_KEYS // SUBLANES
    sub_key = lax.broadcasted_iota(jnp.int32, (SUBLANES, LANES), 0).astype(F32)

    def stage1(hp, carry):
        for lg in range(tm // LANES):
            l0 = lg * LANES
            val = [sc_s[hp, v * SUBLANES:(v + 1) * SUBLANES, l0:l0 + LANES] for v in range(n_rows)]
            key = [sub_key + float(v * SUBLANES) for v in range(n_rows)]
            for rnd in range(n_rows):
                for a in range(rnd % 2, n_rows - 1, 2):
                    swap = val[a + 1] > val[a]
                    val[a], val[a + 1] = (jnp.where(swap, val[a + 1], val[a]),
                                          jnp.where(swap, val[a], val[a + 1]))
                    key[a], key[a + 1] = (jnp.where(swap, key[a + 1], key[a]),
                                          jnp.where(swap, key[a], key[a + 1]))
            for k in range(PEER_TOPK):
                m = jnp.max(val[0], axis=0, keepdims=True)
                i = jnp.min(jnp.where(val[0] == m, key[0], float(N_KEYS)), axis=0, keepdims=True)
                ts_s[hp, k:k + 1, l0:l0 + LANES] = m
                ti_s[hp, k:k + 1, l0:l0 + LANES] = i
                if k + 1 < PEER_TOPK:
                    taken = key[0] == i
                    for p in range(PEER_TOPK - 1 - k):
                        val[p] = jnp.where(taken, val[p + 1], val[p])
                        key[p] = jnp.where(taken, key[p + 1], key[p])
        return carry

    lax.fori_loop(0, n_hp, stage1, 0)

    n_cand = float(PEER_TOPK * PEER_TOPK)
    slot = lax.broadcasted_iota(jnp.int32, (CAND_ROWS, LANES), 0)
    cand_flat = slot
    for a in range(1, PEER_TOPK):
        step = (a * PEER_TOPK - CAND_OFFSET[a]) - ((a - 1) * PEER_TOPK - CAND_OFFSET[a - 1])
        cand_flat = cand_flat + jnp.where(slot >= CAND_OFFSET[a], step, 0)
    cand_flat = jnp.where(slot >= CAND_USED, PEER_TOPK * PEER_TOPK, cand_flat).astype(F32)

    def stage2(hd, carry):
        for lg in range(tm // LANES):
            l0 = lg * LANES
            s0 = ts_s[2 * hd, :, l0:l0 + LANES]
            s1 = ts_s[2 * hd + 1, :, l0:l0 + LANES]
            i0 = ti_s[2 * hd, :, l0:l0 + LANES]
            i1 = ti_s[2 * hd + 1, :, l0:l0 + LANES]
            for a in range(PEER_TOPK):
                n = CAND_COUNT[a]
                cand_s[CAND_OFFSET[a]:CAND_OFFSET[a] + n, :] = s0[a:a + 1, :] + s1[0:n, :]
            cand_s[CAND_USED:CAND_ROWS, :] = jnp.full((CAND_ROWS - CAND_USED, LANES), -jnp.inf, F32)
            cand = cand_s[...]
            for k in range(PEER_TOPK):
                m = jnp.max(cand, axis=0, keepdims=True)
                c = jnp.min(jnp.where(cand == m, cand_flat, n_cand), axis=0, keepdims=True)
                best_s[k:k + 1, :] = m
                sel_s[k:k + 1, :] = c
                cand = jnp.where(cand_flat == c, -jnp.inf, cand)
            best = best_s[...]
            sel = sel_s[...]
            sel_a = jnp.floor(sel * (1.0 / PEER_TOPK))
            sel_b = sel - sel_a * PEER_TOPK
            e0 = jnp.zeros_like(sel)
            e1 = jnp.zeros_like(sel)
            for a in range(PEER_TOPK):
                e0 = jnp.where(sel_a == float(a), i0[a:a + 1, :], e0)
                e1 = jnp.where(sel_b == float(a), i1[a:a + 1, :], e1)
            ex = jnp.exp(best - best[0:1, :])
            gate = ex / jnp.sum(ex, axis=0, keepdims=True)
            r0 = pl.multiple_of(hd * PEER_TOPK, PEER_TOPK)
            e_s[pl.ds(r0, PEER_TOPK), l0:l0 + LANES] = e0 * float(N_KEYS) + e1
            gt_s[pl.ds(r0, PEER_TOPK), l0:l0 + LANES] = gate
        return carry

    lax.fori_loop(0, PEER_HEADS, stage2, 0)

    ids = jnp.clip(e_s[...], 0.0, float(N_EXPERTS - 1))
    idx_ref[...] = jnp.transpose(ids).astype(jnp.int32)
    gate_ref[...] = jnp.transpose(gt_s[...])


def _route(h, g, wq, sk, tm):
    t = h.shape[0]
    row = lambda i: (i, 0)
    const2 = lambda i: (0, 0)
    n_hp = 2 * PEER_HEADS
    return pl.pallas_call(
        functools.partial(_route_kernel, tm=tm),
        grid=(t // tm,),
        in_specs=[
            pl.BlockSpec((tm, D_MODEL), row),
            pl.BlockSpec((1, D_MODEL), const2),
            pl.BlockSpec((D_MODEL, n_hp * D_HALF), const2),
            pl.BlockSpec((n_hp, N_KEYS, D_HALF), lambda i: (0, 0, 0)),
        ],
        out_specs=[pl.BlockSpec((tm, N_PICKS), row), pl.BlockSpec((tm, N_PICKS), row)],
        out_shape=[jax.ShapeDtypeStruct((t, N_PICKS), jnp.int32),
                   jax.ShapeDtypeStruct((t, N_PICKS), F32)],
        scratch_shapes=[
            pltpu.VMEM((n_hp, N_KEYS, tm), F32),
            pltpu.VMEM((n_hp, PEER_TOPK, tm), F32),
            pltpu.VMEM((n_hp, PEER_TOPK, tm), F32),
            pltpu.VMEM((CAND_ROWS, LANES), F32),
            pltpu.VMEM((PEER_TOPK, LANES), F32),
            pltpu.VMEM((PEER_TOPK, LANES), F32),
            pltpu.VMEM((N_PICKS, tm), F32),
            pltpu.VMEM((N_PICKS, tm), F32),
        ],
        compiler_params=_cparams(1),
        name="peer_route",
    )(h, g, wq, sk)


def _pack_kernel(u_ref, v_ref, o_ref):
    n = u_ref.shape[0]
    half = D_MODEL // 2
    for src, row0 in ((u_ref, 0), (v_ref, WORD_ROWS)):
        bits = pltpu.bitcast(src[...].astype(BF16).astype(F32), jnp.uint32)
        for t in range(WORD_ROWS):
            lo = bits[:, t * LANES:(t + 1) * LANES] >> 16
            hi = bits[:, half + t * LANES:half + (t + 1) * LANES] & jnp.uint32(0xFFFF0000)
            o_ref[pl.ds(row0 + t, n, stride=ROWS_PER_EXPERT), :] = pltpu.bitcast(lo | hi, jnp.int32)


def _pack_experts(u, v, layer):
    n_blk = 256
    row = lambda i: (i, 0)
    layer_row = lambda i: (layer, i, 0)
    out = pl.pallas_call(
        _pack_kernel,
        grid=(N_EXPERTS // n_blk,),
        in_specs=[pl.BlockSpec((None, n_blk, D_MODEL), layer_row),
                  pl.BlockSpec((None, n_blk, D_MODEL), layer_row)],
        out_specs=pl.BlockSpec((n_blk * ROWS_PER_EXPERT, LANES), row),
        out_shape=jax.ShapeDtypeStruct((N_EXPERTS * ROWS_PER_EXPERT, LANES), jnp.int32),
        compiler_params=_cparams(1),
        name="pack_experts",
    )(u, v)
    return out.reshape(N_EXPERTS, ROWS_PER_EXPERT, LANES)


def _gather_kernel(idx_ref, gate_ref, h_ref, g_ref, gfin_ref, tab_ref, out_ref,
                   hf_s, buf, sem, *, tt, final_norm):
    hf_s[...] = _rms(h_ref[...], g_ref[...])
    n_sub = D_MODEL // LANES
    n_groups = tt // N_SLOTS
    n_pieces = 2 * WORD_ROWS
    picks_per_piece = N_PICKS // n_pieces

    def unpack(words):
        return (pltpu.bitcast(words << 16, F32), pltpu.bitcast(words & HIGH_HALF, F32))

    def pick_copy(e, slot, r):
        dst = buf.at[pl.ds(slot * SLOT_ROWS + r * PICK_PITCH, ROWS_PER_EXPERT), :]
        return pltpu.make_async_copy(tab_ref.at[e], dst, sem.at[slot])

    def issue(j, slot, r_lo, r_hi):
        for r in range(r_lo, r_hi):
            pick_copy(idx_ref[j, r], slot, r).start(priority=r % 2)

    def wait(slot):
        for r in range(N_PICKS):
            pick_copy(0, slot, r).wait()

    eye = (lax.broadcasted_iota(jnp.int32, (N_PICKS, N_PICKS), 0)
           == lax.broadcasted_iota(jnp.int32, (N_PICKS, N_PICKS), 1))
    sub = lax.broadcasted_iota(jnp.int32, (SUBLANES, LANES), 0)

    def token(k, x8, gate8, tiles, prefetch):
        base = k * SLOT_ROWS
        wait(k)
        piece = 0
        acc = jnp.zeros((N_PICKS, LANES), F32)
        for t in range(WORD_ROWS):
            u_lo, u_hi = unpack(buf[pl.ds(base + t, N_PICKS, stride=PICK_PITCH), :])
            acc = acc + u_lo * x8[t][k:k + 1, :] + u_hi * x8[WORD_ROWS + t][k:k + 1, :]
            prefetch(piece)
            piece += 1
        act = jax.nn.gelu(jnp.sum(acc, axis=1, keepdims=True))
        gcol = jnp.sum(jnp.where(eye, gate8[k:k + 1, :], 0.0), axis=1, keepdims=True)
        w = gcol * act
        new_tiles = list(tiles)
        for t in range(WORD_ROWS):
            v_lo, v_hi = unpack(buf[pl.ds(base + WORD_ROWS + t, N_PICKS, stride=PICK_PITCH), :])
            o_lo = jnp.sum(v_lo * w, axis=0, keepdims=True)
            o_hi = jnp.sum(v_hi * w, axis=0, keepdims=True)
            new_tiles[t] = jnp.where(sub == k, o_lo, tiles[t])
            new_tiles[WORD_ROWS + t] = jnp.where(sub == k, o_hi, tiles[WORD_ROWS + t])
            prefetch(piece)
            piece += 1
        return new_tiles

    def group(gi, last):
        r0 = pl.multiple_of(gi * N_SLOTS, N_SLOTS)
        x8 = [hf_s[pl.ds(r0, N_SLOTS), s * LANES:(s + 1) * LANES] for s in range(n_sub)]
        gate8 = gate_ref[pl.ds(r0, N_SLOTS), :]
        tiles = [jnp.zeros((SUBLANES, LANES), F32) for _ in range(n_sub)]
        for k in range(N_SLOTS):
            ahead = k + PREFETCH_DIST
            if last and ahead >= N_SLOTS:
                prefetch = lambda piece: None
            else:
                def prefetch(piece, ahead=ahead):
                    issue(r0 + ahead, ahead % N_SLOTS,
                          piece * picks_per_piece, (piece + 1) * picks_per_piece)
            tiles = token(k, x8, gate8, tiles, prefetch)
        res = [h_ref[pl.ds(r0, N_SLOTS), s * LANES:(s + 1) * LANES] + tiles[s] for s in range(n_sub)]
        if final_norm:
            ss = res[0] * res[0]
            for s in range(1, n_sub):
                ss = ss + res[s] * res[s]
            scale = lax.rsqrt(jnp.sum(ss, axis=1, keepdims=True) * (1.0 / D_MODEL) + EPS)
            res = [res[s] * scale * gfin_ref[:, s * LANES:(s + 1) * LANES] for s in range(n_sub)]
        for s in range(n_sub):
            out_ref[pl.ds(r0, N_SLOTS), s * LANES:(s + 1) * LANES] = res[s]

    for j in range(PREFETCH_DIST):
        issue(j, j, 0, N_PICKS)

    def body(gi, carry):
        group(gi, False)
        return carry

    lax.fori_loop(0, n_groups - 1, body, 0)
    group(n_groups - 1, True)


def _gather(idx, gate, h, g, gfin, tab, tt, final_norm):
    t = h.shape[0]
    row = lambda i: (i, 0)
    const2 = lambda i: (0, 0)
    return pl.pallas_call(
        functools.partial(_gather_kernel, tt=tt, final_norm=final_norm),
        grid=(t // tt,),
        in_specs=[
            pl.BlockSpec((tt, N_PICKS), row, memory_space=pltpu.SMEM),
            pl.BlockSpec((tt, N_PICKS), row),
            pl.BlockSpec((tt, D_MODEL), row),
            pl.BlockSpec((1, D_MODEL), const2),
            pl.BlockSpec((1, D_MODEL), const2),
            pl.BlockSpec(memory_space=pl.ANY),
        ],
        out_specs=pl.BlockSpec((tt, D_MODEL), row),
        out_shape=jax.ShapeDtypeStruct((t, D_MODEL), F32),
        scratch_shapes=[
            pltpu.VMEM((tt, D_MODEL), F32),
            pltpu.VMEM((N_SLOTS * SLOT_ROWS, LANES), jnp.int32),
            pltpu.SemaphoreType.DMA((N_SLOTS,)),
        ],
        compiler_params=_cparams(1),
        name="peer_experts",
    )(idx, gate, h, g, gfin, tab)


def _proj_kernel(h_ref, gkv_ref, gq_ref, wkv_ref, wq_ref, kv_ref, q_ref):
    h = h_ref[...]
    n = h * lax.rsqrt(jnp.mean(h * h, axis=-1, keepdims=True) + EPS)
    kv_ref[...] = jnp.dot((n * gkv_ref[...]).astype(BF16), wkv_ref[...], preferred_element_type=F32)
    q_ref[...] = jnp.dot((n * gq_ref[...]).astype(BF16), wq_ref[...],
                         preferred_element_type=F32).astype(BF16)


def _proj(h, gkv, gq, wkv, wq, tm):
    t = h.shape[0]
    row = lambda i: (i, 0)
    const2 = lambda i: (0, 0)
    return pl.pallas_call(
        _proj_kernel,
        grid=(t // tm,),
        in_specs=[
            pl.BlockSpec((tm, D_MODEL), row),
            pl.BlockSpec((1, D_MODEL), const2),
            pl.BlockSpec((1, D_MODEL), const2),
            pl.BlockSpec((D_MODEL, KV_DIM), const2),
            pl.BlockSpec((D_MODEL, D_MODEL), const2),
        ],
        out_specs=[pl.BlockSpec((tm, KV_DIM), row), pl.BlockSpec((tm, D_MODEL), row)],
        out_shape=[jax.ShapeDtypeStruct((t, KV_DIM), F32),
                   jax.ShapeDtypeStruct((t, D_MODEL), BF16)],
        compiler_params=_cparams(1),
        name="kv_q_proj",
    )(h, gkv, gq, wkv, wq)


def _alibi_slopes():
    return [2.0 ** (-8.0 * (h + 1) / N_HEADS) for h in range(N_HEADS)]


ATTN_KEYS = WINDOW + CHUNK
ATTN_PAIR = 2 * CHUNK
ATTN_ROW_BLOCK = 32


def _padded_heads(tile, g):
    low = lax.broadcasted_iota(jnp.int32, tile.shape, 1) < HEAD_DIM
    rolled = pltpu.roll(tile, HEAD_DIM, axis=1)
    if g % 2 == 0:
        pads = [jnp.where(low, tile, 0.0), jnp.where(low, 0.0, rolled)]
    else:
        pads = [jnp.where(low, rolled, 0.0), jnp.where(low, 0.0, tile)]
    return [p.astype(BF16) for p in pads]


def _attn_chunks(q_ref, windows, first_valids, sink_ref, s_scr, p_scr):
    half_kv = KV_DIM // 2
    slopes = _alibi_slopes()
    n_chunks = len(windows)

    def chain(c, g, half):
        return (c * N_KV_HEADS + g) * 2 + half

    for c in range(n_chunks):
        win = windows[c]()
        r0 = c * CHUNK
        for g in range(N_KV_HEADS):
            t = g // 2
            k_pad = _padded_heads(win[:, t * LANES:(t + 1) * LANES], g)
            qstack = jnp.concatenate(
                [q_ref[r0:r0 + CHUNK, (2 * g) * LANES:(2 * g + 1) * LANES],
                 q_ref[r0:r0 + CHUNK, (2 * g + 1) * LANES:(2 * g + 2) * LANES]], axis=0)
            for half in range(2):
                s_scr[chain(c, g, half)] = lax.dot_general(
                    qstack, k_pad[half], (((1,), (1,)), ((), ())), preferred_element_type=F32)

    rb = ATTN_ROW_BLOCK
    qi = lax.broadcasted_iota(jnp.int32, (rb, ATTN_KEYS), 0)
    kj = lax.broadcasted_iota(jnp.int32, (rb, ATTN_KEYS), 1)
    dists = [jnp.abs(WINDOW + qi + off - kj).astype(F32) for off in range(0, CHUNK, rb)]
    for c in range(n_chunks):
        valid = kj >= first_valids[c]
        for g in range(N_KV_HEADS):
            for half in range(2):
                ch = chain(c, g, half)
                for b in range(ATTN_PAIR // rb):
                    row0 = b * rb
                    head = GQA_GROUP * g + half + (2 if row0 >= CHUNK else 0)
                    s = s_scr[ch, row0:row0 + rb, :]
                    s = s * (HEAD_DIM ** -0.5) - slopes[head] * dists[(row0 % CHUNK) // rb]
                    s = jnp.where(valid, s, NEG)
                    sink = sink_ref[0, head]
                    m = jnp.maximum(jnp.max(s, axis=1, keepdims=True), sink)
                    e = jnp.exp(s - m)
                    denom = jnp.sum(e, axis=1, keepdims=True) + jnp.exp(sink - m)
                    p_scr[ch, row0:row0 + rb, :] = (e / denom).astype(BF16)

    outs = []
    for c in range(n_chunks):
        win = windows[c]()
        o_tiles = [None] * (N_HEADS // 2)
        for g in range(N_KV_HEADS):
            t = g // 2
            v_pad = _padded_heads(win[:, half_kv + t * LANES:half_kv + (t + 1) * LANES], g)
            o_stack = (jnp.dot(p_scr[chain(c, g, 0)], v_pad[0], preferred_element_type=F32)
                       + jnp.dot(p_scr[chain(c, g, 1)], v_pad[1], preferred_element_type=F32))
            o_tiles[2 * g] = o_stack[:CHUNK]
            o_tiles[2 * g + 1] = o_stack[CHUNK:]
        outs.append(jnp.concatenate(o_tiles, axis=1))
    return outs[0] if n_chunks == 1 else jnp.concatenate(outs, axis=0)


def _attn_scratch(n_chunks):
    n_chains = n_chunks * N_KV_HEADS * 2
    return [pltpu.VMEM((n_chains, ATTN_PAIR, ATTN_KEYS), F32),
            pltpu.VMEM((n_chains, ATTN_PAIR, ATTN_KEYS), BF16)]


def _attn_prompt_kernel(sink_ref, q_ref, kvp_ref, kvc_ref, h_ref, wo_ref, out_ref,
                        kv_s, s_scr, p_scr, *, qt):
    t = pl.program_id(1)
    kv_s[0:WINDOW, :] = kvp_ref[...]
    kv_s[WINDOW:WINDOW + qt, :] = kvc_ref[...]
    n_chunks = qt // CHUNK
    windows = [functools.partial(lambda r0: kv_s[r0:r0 + ATTN_KEYS, :], c * CHUNK)
               for c in range(n_chunks)]
    first_valids = [WINDOW - CHUNK * (t * n_chunks + c) for c in range(n_chunks)]
    o = _attn_chunks(q_ref, windows, first_valids, sink_ref, s_scr, p_scr).astype(BF16)
    out_ref[...] = h_ref[...] + jnp.dot(o, wo_ref[...], preferred_element_type=F32)


def _attn_prompt(sinks, q, kv, h, wo, batch, seq, qt):
    t = h.shape[0]
    tiles = seq // qt
    cur = lambda b, i: (b * tiles + i, 0)
    prev = lambda b, i: (jnp.maximum((b * tiles + i) * (qt // WINDOW) - 1, 0), 0)
    const2 = lambda b, i: (0, 0)
    return pl.pallas_call(
        functools.partial(_attn_prompt_kernel, qt=qt),
        grid=(batch, tiles),
        in_specs=[
            pl.BlockSpec((1, N_HEADS), const2, memory_space=pltpu.SMEM),
            pl.BlockSpec((qt, D_MODEL), cur),
            pl.BlockSpec((WINDOW, KV_DIM), prev),
            pl.BlockSpec((qt, KV_DIM), cur),
            pl.BlockSpec((qt, D_MODEL), cur),
            pl.BlockSpec((D_MODEL, D_MODEL), const2),
        ],
        out_specs=pl.BlockSpec((qt, D_MODEL), cur),
        out_shape=jax.ShapeDtypeStruct((t, D_MODEL), F32),
        scratch_shapes=[pltpu.VMEM((WINDOW + qt, KV_DIM), F32)] + _attn_scratch(qt // CHUNK),
        compiler_params=_cparams(2),
        name="attn_prompt",
    )(sinks, q, kv, kv, h, wo)


def _attn_sample_kernel(sink_ref, q_ref, ck_ref, cv_ref, kvn_ref, h_ref, wo_ref, out_ref,
                        s_scr, p_scr, *, nb):
    half_kv = KV_DIM // 2

    def window(b):
        kvn = kvn_ref[b * CHUNK:(b + 1) * CHUNK, :]
        k = jnp.concatenate([ck_ref[b * WINDOW:(b + 1) * WINDOW, :], kvn[:, :half_kv]], axis=0)
        v = jnp.concatenate([cv_ref[b * WINDOW:(b + 1) * WINDOW, :], kvn[:, half_kv:]], axis=0)
        return jnp.concatenate([k, v], axis=1)

    windows = [functools.partial(window, b) for b in range(nb)]
    o = _attn_chunks(q_ref, windows, [0] * nb, sink_ref, s_scr, p_scr).astype(BF16)
    out_ref[...] = h_ref[...] + jnp.dot(o, wo_ref[...], preferred_element_type=F32)


def _attn_sample(sinks, q, ck, cv, kv, h, wo, batch, nb):
    t = h.shape[0]
    row = lambda b: (b, 0)
    const2 = lambda b: (0, 0)
    return pl.pallas_call(
        functools.partial(_attn_sample_kernel, nb=nb),
        grid=(batch // nb,),
        in_specs=[
            pl.BlockSpec((1, N_HEADS), const2, memory_space=pltpu.SMEM),
            pl.BlockSpec((nb * CHUNK, D_MODEL), row),
            pl.BlockSpec((nb * WINDOW, KV_DIM // 2), row),
            pl.BlockSpec((nb * WINDOW, KV_DIM // 2), row),
            pl.BlockSpec((nb * CHUNK, KV_DIM), row),
            pl.BlockSpec((nb * CHUNK, D_MODEL), row),
            pl.BlockSpec((D_MODEL, D_MODEL), const2),
        ],
        scratch_shapes=_attn_scratch(nb),
        out_specs=pl.BlockSpec((nb * CHUNK, D_MODEL), row),
        out_shape=jax.ShapeDtypeStruct((t, D_MODEL), F32),
        compiler_params=_cparams(1),
        name="attn_sample",
    )(sinks, q, ck, cv, kv, h, wo)


def kernel(x_prompt, x_sample, cache_k_win, cache_v_win, norm_mix, norm_ffn, gmlp_w_in, gmlp_norm_v, gmlp_w_s, gmlp_b_s, gmlp_w_out, norm_kv, w_kv, attn_w_q, attn_sinks, attn_w_o, peer_w_q, peer_sub_keys, peer_u, peer_v, norm_final):
    batch, seq, _ = x_prompt.shape
    dec_batch, dec_seq, _ = x_sample.shape
    w_cache = cache_k_win.shape[1]
    assert dec_seq == CHUNK and w_cache == WINDOW and seq % MLP_CHUNK == 0

    row2 = lambda a: a.reshape(1, -1)
    win = gmlp_w_in[0].astype(BF16)
    wout = gmlp_w_out[0].astype(BF16)
    wkv = w_kv.astype(BF16)
    wq_attn = attn_w_q[0].astype(BF16)
    wo = attn_w_o[0].astype(BF16)
    wq_peer = peer_w_q.astype(BF16)
    sub_keys = peer_sub_keys.reshape(2, 2 * PEER_HEADS, N_KEYS, D_HALF).astype(BF16)
    tables = [_pack_experts(peer_u, peer_v, l) for l in range(2)]

    pos = jnp.arange(MLP_CHUNK)
    blk_i = pos[:, None] // CHUNK
    blk_j = pos[None, :] // CHUNK
    ws = gmlp_w_s[0]
    bs = gmlp_b_s[0]
    wmix_p = jnp.where((blk_j <= blk_i)[None], ws, 0.0).astype(BF16)
    ws64 = ws[:, :CHUNK, :CHUNK]
    wmix_s = jnp.where((blk_j == blk_i)[None], jnp.tile(ws64, (1, 2, 2)), 0.0).astype(BF16)
    bias_p = jnp.repeat(bs.T, LANES, axis=1)
    bias_s = jnp.repeat(jnp.tile(bs[:, :CHUNK], (1, 2)).T, LANES, axis=1)

    def trunk(x, prompt):
        t = x.shape[0]
        tm = 256
        tm_route = 512
        tt = 256
        h, v_rows = _gmlp(x, row2(norm_mix[0]), win, row2(gmlp_norm_v[0]),
                          wmix_p if prompt else wmix_s, bias_p if prompt else bias_s, wout, tm)
        idx, gate = _route(h, row2(norm_ffn[0]), wq_peer[0], sub_keys[0], tm_route)
        h = _gather(idx, gate, h, row2(norm_ffn[0]), row2(norm_final), tables[0], tt, False)
        kv, q = _proj(h, row2(norm_kv), row2(norm_mix[1]), wkv, wq_attn, tm)
        if prompt:
            h = _attn_prompt(attn_sinks, q, kv, h, wo, batch, seq, 256)
        else:
            ck = cache_k_win.reshape(dec_batch * w_cache, KV_DIM // 2)
            cv = cache_v_win.reshape(dec_batch * w_cache, KV_DIM // 2)
            h = _attn_sample(attn_sinks, q, ck, cv, kv, h, wo, dec_batch, 4)
        idx, gate = _route(h, row2(norm_ffn[1]), wq_peer[1], sub_keys[1], tm_route)
        y = _gather(idx, gate, h, row2(norm_ffn[1]), row2(norm_final), tables[1], tt, True)
        return y, kv, v_rows

    y_p, kv_p, _ = trunk(x_prompt.reshape(batch * seq, D_MODEL), True)
    y_s, kv_s, v_s = trunk(x_sample.reshape(dec_batch * dec_seq, D_MODEL), False)

    half = KV_DIM // 2
    kv_p = kv_p.reshape(batch, seq, KV_DIM)[:, seq - WINDOW:]
    prompt_k_win = kv_p[..., :half].reshape(batch, WINDOW, N_KV_HEADS, HEAD_DIM)
    prompt_v_win = kv_p[..., half:].reshape(batch, WINDOW, N_KV_HEADS, HEAD_DIM)
    kv_s = kv_s.reshape(dec_batch, dec_seq, KV_DIM)
    k_s = kv_s[..., :half].reshape(dec_batch, dec_seq, N_KV_HEADS, HEAD_DIM)
    v_s_new = kv_s[..., half:].reshape(dec_batch, dec_seq, N_KV_HEADS, HEAD_DIM)
    sample_k_win = jnp.concatenate([cache_k_win, k_s], axis=1)[:, -w_cache:]
    sample_v_win = jnp.concatenate([cache_v_win, v_s_new], axis=1)[:, -w_cache:]
    sample_gmlp_v = v_s.reshape(1, dec_batch, dec_seq, D_GATE)

    return (y_p.reshape(batch, seq, D_MODEL), y_s.reshape(dec_batch, dec_seq, D_MODEL),
            prompt_k_win, prompt_v_win, sample_k_win, sample_v_win, sample_gmlp_v)
```

```python
import functools
import math

import jax
import jax.numpy as jnp
from jax import lax
from jax.experimental import pallas as pl
from jax.experimental.pallas import tpu as pltpu

F32 = jnp.float32
BF16 = jnp.bfloat16

D_MODEL = 1024
CHUNK = 64
MLP_CHUNK = 128
GMLP_GROUPS = 8
D_GATE = D_MODEL
N_HEADS = 16
N_KV_HEADS = 4
HEAD_DIM = 64
GQA_GROUP = N_HEADS // N_KV_HEADS
WINDOW = 128
PEER_HEADS = 8
N_KEYS = 128
N_EXPERTS = N_KEYS * N_KEYS
PEER_TOPK = 16
D_HALF = 128
N_PICKS = PEER_HEADS * PEER_TOPK
EPS = 1e-6
NEG = -1e30
KV_DIM = 2 * N_KV_HEADS * HEAD_DIM

LANES = 128
SUBLANES = 8
WORD_ROWS = D_MODEL // 2 // LANES
ROWS_PER_EXPERT = 2 * WORD_ROWS
HIGH_HALF = -65536
PICK_PITCH = 12
SLOT_ROWS = N_PICKS * PICK_PITCH
N_SLOTS = SUBLANES
PREFETCH_DIST = 6

VMEM_LIMIT = 48 * 1024 * 1024


def _rms(x, g):
    return x * lax.rsqrt(jnp.mean(x * x, axis=-1, keepdims=True) + EPS) * g


def _cparams(n_axes):
    return pltpu.CompilerParams(
        dimension_semantics=("arbitrary",) * n_axes, vmem_limit_bytes=VMEM_LIMIT)


def _gmlp_kernel(x_ref, gmix_ref, win_ref, gv_ref, wmix_ref, bias_ref, wout_ref,
                 h_ref, v_ref, *, tm):
    x = x_ref[...]
    hn = _rms(x, gmix_ref[...])
    hmid = jax.nn.gelu(jnp.dot(hn.astype(BF16), win_ref[...], preferred_element_type=F32))
    u = hmid[:, :D_GATE]
    v = _rms(hmid[:, D_GATE:], gv_ref[...])
    v_ref[...] = v
    vb = v.astype(BF16)
    bias = bias_ref[...]
    outs = []
    for c in range(tm // MLP_CHUNK):
        r0 = c * MLP_CHUNK
        cols = []
        for g in range(GMLP_GROUPS):
            c0 = g * LANES
            cols.append(jnp.dot(wmix_ref[g], vb[r0:r0 + MLP_CHUNK, c0:c0 + LANES],
                                preferred_element_type=F32))
        mixed = jnp.concatenate(cols, axis=1) + bias
        outs.append(u[r0:r0 + MLP_CHUNK, :] * mixed)
    out = outs[0] if len(outs) == 1 else jnp.concatenate(outs, axis=0)
    y = jnp.dot(out.astype(BF16), wout_ref[...], preferred_element_type=F32)
    h_ref[...] = x + y


def _gmlp(x, gmix, win, gv, wmix, bias, wout, tm):
    t = x.shape[0]
    row = lambda i: (i, 0)
    const2 = lambda i: (0, 0)
    return pl.pallas_call(
        functools.partial(_gmlp_kernel, tm=tm),
        grid=(t // tm,),
        in_specs=[
            pl.BlockSpec((tm, D_MODEL), row),
            pl.BlockSpec((1, D_MODEL), const2),
            pl.BlockSpec((D_MODEL, 2 * D_GATE), const2),
            pl.BlockSpec((1, D_GATE), const2),
            pl.BlockSpec((GMLP_GROUPS, MLP_CHUNK, MLP_CHUNK), lambda i: (0, 0, 0)),
            pl.BlockSpec((MLP_CHUNK, D_GATE), const2),
            pl.BlockSpec((D_GATE, D_MODEL), const2),
        ],
        out_specs=[pl.BlockSpec((tm, D_MODEL), row), pl.BlockSpec((tm, D_GATE), row)],
        out_shape=[jax.ShapeDtypeStruct((t, D_MODEL), F32),
                   jax.ShapeDtypeStruct((t, D_GATE), F32)],
        compiler_params=_cparams(1),
        name="gmlp_mixer",
    )(x, gmix, win, gv, wmix, bias, wout)


CAND_COUNT = [PEER_TOPK // (a + 1) for a in range(PEER_TOPK)]
CAND_OFFSET = [sum(CAND_COUNT[:a]) for a in range(PEER_TOPK)]
CAND_USED = sum(CAND_COUNT)
CAND_ROWS = -(-CAND_USED // SUBLANES) * SUBLANES


def _route_kernel(h_ref, g_ref, wq_ref, sk_ref, idx_ref, gate_ref,
                  sc_s, ts_s, ti_s, cand_s, best_s, sel_s, e_s, gt_s, *, tm):
    hf = _rms(h_ref[...], g_ref[...])
    q = jnp.dot(hf.astype(BF16), wq_ref[...], preferred_element_type=F32).astype(BF16)
    n_hp = 2 * PEER_HEADS
    for hp in range(n_hp):
        sc_s[hp] = lax.dot_general(sk_ref[hp], q[:, hp * D_HALF:(hp + 1) * D_HALF],
                                   (((1,), (1,)), ((), ())), preferred_element_type=F32)

    n_rows = N_KEYS // SUBLANES
    sub_key = lax.broadcasted_iota(jnp.int32, (SUBLANES, LANES), 0).astype(F32)

    def stage1(hp, carry):
        for lg in range(tm // LANES):
            l0 = lg * LANES
            val = [sc_s[hp, v * SUBLANES:(v + 1) * SUBLANES, l0:l0 + LANES] for v in range(n_rows)]
            key = [sub_key + float(v * SUBLANES) for v in range(n_rows)]
            for rnd in range(n_rows):
                for a in range(rnd % 2, n_rows - 1, 2):
                    swap = val[a + 1] > val[a]
                    val[a], val[a + 1] = (jnp.where(swap, val[a + 1], val[a]),
                                          jnp.where(swap, val[a], val[a + 1]))
                    key[a], key[a + 1] = (jnp.where(swap, key[a + 1], key[a]),
                                          jnp.where(swap, key[a], key[a + 1]))
            for k in range(PEER_TOPK):
                m = jnp.max(val[0], axis=0, keepdims=True)
                i = jnp.min(jnp.where(val[0] == m, key[0], float(N_KEYS)), axis=0, keepdims=True)
                ts_s[hp, k:k + 1, l0:l0 + LANES] = m
                ti_s[hp, k:k + 1, l0:l0 + LANES] = i
                if k + 1 < PEER_TOPK:
                    taken = key[0] == i
                    for p in range(PEER_TOPK - 1 - k):
                        val[p] = jnp.where(taken, val[p + 1], val[p])
                        key[p] = jnp.where(taken, key[p + 1], key[p])
        return carry

    lax.fori_loop(0, n_hp, stage1, 0)

    n_cand = float(PEER_TOPK * PEER_TOPK)
    slot = lax.broadcasted_iota(jnp.int32, (CAND_ROWS, LANES), 0)
    cand_flat = slot
    for a in range(1, PEER_TOPK):
        step = (a * PEER_TOPK - CAND_OFFSET[a]) - ((a - 1) * PEER_TOPK - CAND_OFFSET[a - 1])
        cand_flat = cand_flat + jnp.where(slot >= CAND_OFFSET[a], step, 0)
    cand_flat = jnp.where(slot >= CAND_USED, PEER_TOPK * PEER_TOPK, cand_flat).astype(F32)

    def stage2(hd, carry):
        for lg in range(tm // LANES):
            l0 = lg * LANES
            s0 = ts_s[2 * hd, :, l0:l0 + LANES]
            s1 = ts_s[2 * hd + 1, :, l0:l0 + LANES]
            i0 = ti_s[2 * hd, :, l0:l0 + LANES]
            i1 = ti_s[2 * hd + 1, :, l0:l0 + LANES]
            for a in range(PEER_TOPK):
                n = CAND_COUNT[a]
                cand_s[CAND_OFFSET[a]:CAND_OFFSET[a] + n, :] = s0[a:a + 1, :] + s1[0:n, :]
            cand_s[CAND_USED:CAND_ROWS, :] = jnp.full((CAND_ROWS - CAND_USED, LANES), -jnp.inf, F32)
            cand = cand_s[...]
            for k in range(PEER_TOPK):
                m = jnp.max(cand, axis=0, keepdims=True)
                c = jnp.min(jnp.where(cand == m, cand_flat, n_cand), axis=0, keepdims=True)
                best_s[k:k + 1, :] = m
                sel_s[k:k + 1, :] = c
                cand = jnp.where(cand_flat == c, -jnp.inf, cand)
            best = best_s[...]
            sel = sel_s[...]
            sel_a = jnp.floor(sel * (1.0 / PEER_TOPK))
            sel_b = sel - sel_a * PEER_TOPK
            e0 = jnp.zeros_like(sel)
            e1 = jnp.zeros_like(sel)
            for a in range(PEER_TOPK):
                e0 = jnp.where(sel_a == float(a), i0[a:a + 1, :], e0)
                e1 = jnp.where(sel_b == float(a), i1[a:a + 1, :], e1)
            ex = jnp.exp(best - best[0:1, :])
            gate = ex / jnp.sum(ex, axis=0, keepdims=True)
            r0 = pl.multiple_of(hd * PEER_TOPK, PEER_TOPK)
            e_s[pl.ds(r0, PEER_TOPK), l0:l0 + LANES] = e0 * float(N_KEYS) + e1
            gt_s[pl.ds(r0, PEER_TOPK), l0:l0 + LANES] = gate
        return carry

    lax.fori_loop(0, PEER_HEADS, stage2, 0)

    ids = jnp.clip(e_s[...], 0.0, float(N_EXPERTS - 1))
    idx_ref[...] = jnp.transpose(ids).astype(jnp.int32)
    gate_ref[...] = jnp.transpose(gt_s[...])


def _route(h, g, wq, sk, tm):
    t = h.shape[0]
    row = lambda i: (i, 0)
    const2 = lambda i: (0, 0)
    n_hp = 2 * PEER_HEADS
    return pl.pallas_call(
        functools.partial(_route_kernel, tm=tm),
        grid=(t // tm,),
        in_specs=[
            pl.BlockSpec((tm, D_MODEL), row),
            pl.BlockSpec((1, D_MODEL), const2),
            pl.BlockSpec((D_MODEL, n_hp * D_HALF), const2),
            pl.BlockSpec((n_hp, N_KEYS, D_HALF), lambda i: (0, 0, 0)),
        ],
        out_specs=[pl.BlockSpec((tm, N_PICKS), row), pl.BlockSpec((tm, N_PICKS), row)],
        out_shape=[jax.ShapeDtypeStruct((t, N_PICKS), jnp.int32),
                   jax.ShapeDtypeStruct((t, N_PICKS), F32)],
        scratch_shapes=[
            pltpu.VMEM((n_hp, N_KEYS, tm), F32),
            pltpu.VMEM((n_hp, PEER_TOPK, tm), F32),
            pltpu.VMEM((n_hp, PEER_TOPK, tm), F32),
            pltpu.VMEM((CAND_ROWS, LANES), F32),
            pltpu.VMEM((PEER_TOPK, LANES), F32),
            pltpu.VMEM((PEER_TOPK, LANES), F32),
            pltpu.VMEM((N_PICKS, tm), F32),
            pltpu.VMEM((N_PICKS, tm), F32),
        ],
        compiler_params=_cparams(1),
        name="peer_route",
    )(h, g, wq, sk)


def _pack_kernel(u_ref, v_ref, o_ref):
    n = u_ref.shape[0]
    half = D_MODEL // 2
    for src, row0 in ((u_ref, 0), (v_ref, WORD_ROWS)):
        bits = pltpu.bitcast(src[...].astype(BF16).astype(F32), jnp.uint32)
        for t in range(WORD_ROWS):
            lo = bits[:, t * LANES:(t + 1) * LANES] >> 16
            hi = bits[:, half + t * LANES:half + (t + 1) * LANES] & jnp.uint32(0xFFFF0000)
            o_ref[pl.ds(row0 + t, n, stride=ROWS_PER_EXPERT), :] = pltpu.bitcast(lo | hi, jnp.int32)


def _pack_experts(u, v, layer):
    n_blk = 256
    row = lambda i: (i, 0)
    layer_row = lambda i: (layer, i, 0)
    out = pl.pallas_call(
        _pack_kernel,
        grid=(N_EXPERTS // n_blk,),
        in_specs=[pl.BlockSpec((None, n_blk, D_MODEL), layer_row),
                  pl.BlockSpec((None, n_blk, D_MODEL), layer_row)],
        out_specs=pl.BlockSpec((n_blk * ROWS_PER_EXPERT, LANES), row),
        out_shape=jax.ShapeDtypeStruct((N_EXPERTS * ROWS_PER_EXPERT, LANES), jnp.int32),
        compiler_params=_cparams(1),
        name="pack_experts",
    )(u, v)
    return out.reshape(N_EXPERTS, ROWS_PER_EXPERT, LANES)


def _gather_kernel(idx_ref, gate_ref, h_ref, g_ref, gfin_ref, tab_ref, out_ref,
                   hf_s, buf, sem, *, tt, final_norm):
    hf_s[...] = _rms(h_ref[...], g_ref[...])
    n_sub = D_MODEL // LANES
    n_groups = tt // N_SLOTS
    n_pieces = 2 * WORD_ROWS
    picks_per_piece = N_PICKS // n_pieces

    def unpack(words):
        return (pltpu.bitcast(words << 16, F32), pltpu.bitcast(words & HIGH_HALF, F32))

    def pick_copy(e, slot, r):
        dst = buf.at[pl.ds(slot * SLOT_ROWS + r * PICK_PITCH, ROWS_PER_EXPERT), :]
        return pltpu.make_async_copy(tab_ref.at[e], dst, sem.at[slot])

    def issue(j, slot, r_lo, r_hi):
        for r in range(r_lo, r_hi):
            pick_copy(idx_ref[j, r], slot, r).start(priority=r % 2)

    def wait(slot):
        for r in range(N_PICKS):
            pick_copy(0, slot, r).wait()

    eye = (lax.broadcasted_iota(jnp.int32, (N_PICKS, N_PICKS), 0)
           == lax.broadcasted_iota(jnp.int32, (N_PICKS, N_PICKS), 1))
    sub = lax.broadcasted_iota(jnp.int32, (SUBLANES, LANES), 0)

    def token(k, x8, gate8, tiles, prefetch):
        base = k * SLOT_ROWS
        wait(k)
        piece = 0
        acc = jnp.zeros((N_PICKS, LANES), F32)
        for t in range(WORD_ROWS):
            u_lo, u_hi = unpack(buf[pl.ds(base + t, N_PICKS, stride=PICK_PITCH), :])
            acc = acc + u_lo * x8[t][k:k + 1, :] + u_hi * x8[WORD_ROWS + t][k:k + 1, :]
            prefetch(piece)
            piece += 1
        act = jax.nn.gelu(jnp.sum(acc, axis=1, keepdims=True))
        gcol = jnp.sum(jnp.where(eye, gate8[k:k + 1, :], 0.0), axis=1, keepdims=True)
        w = gcol * act
        new_tiles = list(tiles)
        for t in range(WORD_ROWS):
            v_lo, v_hi = unpack(buf[pl.ds(base + WORD_ROWS + t, N_PICKS, stride=PICK_PITCH), :])
            o_lo = jnp.sum(v_lo * w, axis=0, keepdims=True)
            o_hi = jnp.sum(v_hi * w, axis=0, keepdims=True)
            new_tiles[t] = jnp.where(sub == k, o_lo, tiles[t])
            new_tiles[WORD_ROWS + t] = jnp.where(sub == k, o_hi, tiles[WORD_ROWS + t])
            prefetch(piece)
            piece += 1
        return new_tiles

    def group(gi, last):
        r0 = pl.multiple_of(gi * N_SLOTS, N_SLOTS)
        x8 = [hf_s[pl.ds(r0, N_SLOTS), s * LANES:(s + 1) * LANES] for s in range(n_sub)]
        gate8 = gate_ref[pl.ds(r0, N_SLOTS), :]
        tiles = [jnp.zeros((SUBLANES, LANES), F32) for _ in range(n_sub)]
        for k in range(N_SLOTS):
            ahead = k + PREFETCH_DIST
            if last and ahead >= N_SLOTS:
                prefetch = lambda piece: None
            else:
                def prefetch(piece, ahead=ahead):
                    issue(r0 + ahead, ahead % N_SLOTS,
                          piece * picks_per_piece, (piece + 1) * picks_per_piece)
            tiles = token(k, x8, gate8, tiles, prefetch)
        res = [h_ref[pl.ds(r0, N_SLOTS), s * LANES:(s + 1) * LANES] + tiles[s] for s in range(n_sub)]
        if final_norm:
            ss = res[0] * res[0]
            for s in range(1, n_sub):
                ss = ss + res[s] * res[s]
            scale = lax.rsqrt(jnp.sum(ss, axis=1, keepdims=True) * (1.0 / D_MODEL) + EPS)
            res = [res[s] * scale * gfin_ref[:, s * LANES:(s + 1) * LANES] for s in range(n_sub)]
        for s in range(n_sub):
            out_ref[pl.ds(r0, N_SLOTS), s * LANES:(s + 1) * LANES] = res[s]

    for j in range(PREFETCH_DIST):
        issue(j, j, 0, N_PICKS)

    def body(gi, carry):
        group(gi, False)
        return carry

    lax.fori_loop(0, n_groups - 1, body, 0)
    group(n_groups - 1, True)


def _gather(idx, gate, h, g, gfin, tab, tt, final_norm):
    t = h.shape[0]
    row = lambda i: (i, 0)
    const2 = lambda i: (0, 0)
    return pl.pallas_call(
        functools.partial(_gather_kernel, tt=tt, final_norm=final_norm),
        grid=(t // tt,),
        in_specs=[
            pl.BlockSpec((tt, N_PICKS), row, memory_space=pltpu.SMEM),
            pl.BlockSpec((tt, N_PICKS), row),
            pl.BlockSpec((tt, D_MODEL), row),
            pl.BlockSpec((1, D_MODEL), const2),
            pl.BlockSpec((1, D_MODEL), const2),
            pl.BlockSpec(memory_space=pl.ANY),
        ],
        out_specs=pl.BlockSpec((tt, D_MODEL), row),
        out_shape=jax.ShapeDtypeStruct((t, D_MODEL), F32),
        scratch_shapes=[
            pltpu.VMEM((tt, D_MODEL), F32),
            pltpu.VMEM((N_SLOTS * SLOT_ROWS, LANES), jnp.int32),
            pltpu.SemaphoreType.DMA((N_SLOTS,)),
        ],
        compiler_params=_cparams(1),
        name="peer_experts",
    )(idx, gate, h, g, gfin, tab)


def _proj_kernel(h_ref, gkv_ref, gq_ref, wkv_ref, wq_ref, kv_ref, q_ref):
    h = h_ref[...]
    n = h * lax.rsqrt(jnp.mean(h * h, axis=-1, keepdims=True) + EPS)
    kv_ref[...] = jnp.dot((n * gkv_ref[...]).astype(BF16), wkv_ref[...], preferred_element_type=F32)
    q_ref[...] = jnp.dot((n * gq_ref[...]).astype(BF16), wq_ref[...],
                         preferred_element_type=F32).astype(BF16)


def _proj(h, gkv, gq, wkv, wq, tm):
    t = h.shape[0]
    row = lambda i: (i, 0)
    const2 = lambda i: (0, 0)
    return pl.pallas_call(
        _proj_kernel,
        grid=(t // tm,),
        in_specs=[
            pl.BlockSpec((tm, D_MODEL), row),
            pl.BlockSpec((1, D_MODEL), const2),
            pl.BlockSpec((1, D_MODEL), const2),
            pl.BlockSpec((D_MODEL, KV_DIM), const2),
            pl.BlockSpec((D_MODEL, D_MODEL), const2),
        ],
        out_specs=[pl.BlockSpec((tm, KV_DIM), row), pl.BlockSpec((tm, D_MODEL), row)],
        out_shape=[jax.ShapeDtypeStruct((t, KV_DIM), F32),
                   jax.ShapeDtypeStruct((t, D_MODEL), BF16)],
        compiler_params=_cparams(1),
        name="kv_q_proj",
    )(h, gkv, gq, wkv, wq)


def _alibi_slopes():
    return [2.0 ** (-8.0 * (h + 1) / N_HEADS) for h in range(N_HEADS)]


ATTN_KEYS = WINDOW + CHUNK
ATTN_PAIR = 2 * CHUNK
ATTN_ROW_BLOCK = 32


def _padded_heads(tile, g):
    low = lax.broadcasted_iota(jnp.int32, tile.shape, 1) < HEAD_DIM
    rolled = pltpu.roll(tile, HEAD_DIM, axis=1)
    if g % 2 == 0:
        pads = [jnp.where(low, tile, 0.0), jnp.where(low, 0.0, rolled)]
    else:
        pads = [jnp.where(low, rolled, 0.0), jnp.where(low, 0.0, tile)]
    return [p.astype(BF16) for p in pads]


def _attn_chunks(q_ref, windows, first_valids, sink_ref, s_scr, p_scr):
    half_kv = KV_DIM // 2
    slopes = _alibi_slopes()
    n_chunks = len(windows)

    def chain(c, g, half):
        return (c * N_KV_HEADS + g) * 2 + half

    for c in range(n_chunks):
        win = windows[c]()
        r0 = c * CHUNK
        for g in range(N_KV_HEADS):
            t = g // 2
            k_pad = _padded_heads(win[:, t * LANES:(t + 1) * LANES], g)
            qstack = jnp.concatenate(
                [q_ref[r0:r0 + CHUNK, (2 * g) * LANES:(2 * g + 1) * LANES],
                 q_ref[r0:r0 + CHUNK, (2 * g + 1) * LANES:(2 * g + 2) * LANES]], axis=0)
            for half in range(2):
                s_scr[chain(c, g, half)] = lax.dot_general(
                    qstack, k_pad[half], (((1,), (1,)), ((), ())), preferred_element_type=F32)

    rb = ATTN_ROW_BLOCK
    qi = lax.broadcasted_iota(jnp.int32, (rb, ATTN_KEYS), 0)
    kj = lax.broadcasted_iota(jnp.int32, (rb, ATTN_KEYS), 1)
    dists = [jnp.abs(WINDOW + qi + off - kj).astype(F32) for off in range(0, CHUNK, rb)]
    for c in range(n_chunks):
        valid = kj >= first_valids[c]
        for g in range(N_KV_HEADS):
            for half in range(2):
                ch = chain(c, g, half)
                for b in range(ATTN_PAIR // rb):
                    row0 = b * rb
                    head = GQA_GROUP * g + half + (2 if row0 >= CHUNK else 0)
                    s = s_scr[ch, row0:row0 + rb, :]
                    s = s * (HEAD_DIM ** -0.5) - slopes[head] * dists[(row0 % CHUNK) // rb]
                    s = jnp.where(valid, s, NEG)
                    sink = sink_ref[0, head]
                    m = jnp.maximum(jnp.max(s, axis=1, keepdims=True), sink)
                    e = jnp.exp(s - m)
                    denom = jnp.sum(e, axis=1, keepdims=True) + jnp.exp(sink - m)
                    p_scr[ch, row0:row0 + rb, :] = (e / denom).astype(BF16)

    outs = []
    for c in range(n_chunks):
        win = windows[c]()
        o_tiles = [None] * (N_HEADS // 2)
        for g in range(N_KV_HEADS):
            t = g // 2
            v_pad = _padded_heads(win[:, half_kv + t * LANES:half_kv + (t + 1) * LANES], g)
            o_stack = (jnp.dot(p_scr[chain(c, g, 0)], v_pad[0], preferred_element_type=F32)
                       + jnp.dot(p_scr[chain(c, g, 1)], v_pad[1], preferred_element_type=F32))
            o_tiles[2 * g] = o_stack[:CHUNK]
            o_tiles[2 * g + 1] = o_stack[CHUNK:]
        outs.append(jnp.concatenate(o_tiles, axis=1))
    return outs[0] if n_chunks == 1 else jnp.concatenate(outs, axis=0)


def _attn_scratch(n_chunks):
    n_chains = n_chunks * N_KV_HEADS * 2
    return [pltpu.VMEM((n_chains, ATTN_PAIR, ATTN_KEYS), F32),
            pltpu.VMEM((n_chains, ATTN_PAIR, ATTN_KEYS), BF16)]


def _attn_prompt_kernel(sink_ref, q_ref, kvp_ref, kvc_ref, h_ref, wo_ref, out_ref,
                        kv_s, s_scr, p_scr, *, qt):
    t = pl.program_id(1)
    kv_s[0:WINDOW, :] = kvp_ref[...]
    kv_s[WINDOW:WINDOW + qt, :] = kvc_ref[...]
    n_chunks = qt // CHUNK
    windows = [functools.partial(lambda r0: kv_s[r0:r0 + ATTN_KEYS, :], c * CHUNK)
               for c in range(n_chunks)]
    first_valids = [WINDOW - CHUNK * (t * n_chunks + c) for c in range(n_chunks)]
    o = _attn_chunks(q_ref, windows, first_valids, sink_ref, s_scr, p_scr).astype(BF16)
    out_ref[...] = h_ref[...] + jnp.dot(o, wo_ref[...], preferred_element_type=F32)


def _attn_prompt(sinks, q, kv, h, wo, batch, seq, qt):
    t = h.shape[0]
    tiles = seq // qt
    cur = lambda b, i: (b * tiles + i, 0)
    prev = lambda b, i: (jnp.maximum((b * tiles + i) * (qt // WINDOW) - 1, 0), 0)
    const2 = lambda b, i: (0, 0)
    return pl.pallas_call(
        functools.partial(_attn_prompt_kernel, qt=qt),
        grid=(batch, tiles),
        in_specs=[
            pl.BlockSpec((1, N_HEADS), const2, memory_space=pltpu.SMEM),
            pl.BlockSpec((qt, D_MODEL), cur),
            pl.BlockSpec((WINDOW, KV_DIM), prev),
            pl.BlockSpec((qt, KV_DIM), cur),
            pl.BlockSpec((qt, D_MODEL), cur),
            pl.BlockSpec((D_MODEL, D_MODEL), const2),
        ],
        out_specs=pl.BlockSpec((qt, D_MODEL), cur),
        out_shape=jax.ShapeDtypeStruct((t, D_MODEL), F32),
        scratch_shapes=[pltpu.VMEM((WINDOW + qt, KV_DIM), F32)] + _attn_scratch(qt // CHUNK),
        compiler_params=_cparams(2),
        name="attn_prompt",
    )(sinks, q, kv, kv, h, wo)


def _attn_sample_kernel(sink_ref, q_ref, ck_ref, cv_ref, kvn_ref, h_ref, wo_ref, out_ref,
                        s_scr, p_scr, *, nb):
    half_kv = KV_DIM // 2

    def window(b):
        kvn = kvn_ref[b * CHUNK:(b + 1) * CHUNK, :]
        k = jnp.concatenate([ck_ref[b * WINDOW:(b + 1) * WINDOW, :], kvn[:, :half_kv]], axis=0)
        v = jnp.concatenate([cv_ref[b * WINDOW:(b + 1) * WINDOW, :], kvn[:, half_kv:]], axis=0)
        return jnp.concatenate([k, v], axis=1)

    windows = [functools.partial(window, b) for b in range(nb)]
    o = _attn_chunks(q_ref, windows, [0] * nb, sink_ref, s_scr, p_scr).astype(BF16)
    out_ref[...] = h_ref[...] + jnp.dot(o, wo_ref[...], preferred_element_type=F32)


def _attn_sample(sinks, q, ck, cv, kv, h, wo, batch, nb):
    t = h.shape[0]
    row = lambda b: (b, 0)
    const2 = lambda b: (0, 0)
    return pl.pallas_call(
        functools.partial(_attn_sample_kernel, nb=nb),
        grid=(batch // nb,),
        in_specs=[
            pl.BlockSpec((1, N_HEADS), const2, memory_space=pltpu.SMEM),
            pl.BlockSpec((nb * CHUNK, D_MODEL), row),
            pl.BlockSpec((nb * WINDOW, KV_DIM // 2), row),
            pl.BlockSpec((nb * WINDOW, KV_DIM // 2), row),
            pl.BlockSpec((nb * CHUNK, KV_DIM), row),
            pl.BlockSpec((nb * CHUNK, D_MODEL), row),
            pl.BlockSpec((D_MODEL, D_MODEL), const2),
        ],
        scratch_shapes=_attn_scratch(nb),
        out_specs=pl.BlockSpec((nb * CHUNK, D_MODEL), row),
        out_shape=jax.ShapeDtypeStruct((t, D_MODEL), F32),
        compiler_params=_cparams(1),
        name="attn_sample",
    )(sinks, q, ck, cv, kv, h, wo)


def kernel(x_prompt, x_sample, cache_k_win, cache_v_win, norm_mix, norm_ffn, gmlp_w_in, gmlp_norm_v, gmlp_w_s, gmlp_b_s, gmlp_w_out, norm_kv, w_kv, attn_w_q, attn_sinks, attn_w_o, peer_w_q, peer_sub_keys, peer_u, peer_v, norm_final):
    batch, seq, _ = x_prompt.shape
    dec_batch, dec_seq, _ = x_sample.shape
    w_cache = cache_k_win.shape[1]
    assert dec_seq == CHUNK and w_cache == WINDOW and seq % MLP_CHUNK == 0

    row2 = lambda a: a.reshape(1, -1)
    win = gmlp_w_in[0].astype(BF16)
    wout = gmlp_w_out[0].astype(BF16)
    wkv = w_kv.astype(BF16)
    wq_attn = attn_w_q[0].astype(BF16)
    wo = attn_w_o[0].astype(BF16)
    wq_peer = peer_w_q.astype(BF16)
    sub_keys = peer_sub_keys.reshape(2, 2 * PEER_HEADS, N_KEYS, D_HALF).astype(BF16)
    tables = [_pack_experts(peer_u, peer_v, l) for l in range(2)]

    pos = jnp.arange(MLP_CHUNK)
    blk_i = pos[:, None] // CHUNK
    blk_j = pos[None, :] // CHUNK
    ws = gmlp_w_s[0]
    bs = gmlp_b_s[0]
    wmix_p = jnp.where((blk_j <= blk_i)[None], ws, 0.0).astype(BF16)
    ws64 = ws[:, :CHUNK, :CHUNK]
    wmix_s = jnp.where((blk_j == blk_i)[None], jnp.tile(ws64, (1, 2, 2)), 0.0).astype(BF16)
    bias_p = jnp.repeat(bs.T, LANES, axis=1)
    bias_s = jnp.repeat(jnp.tile(bs[:, :CHUNK], (1, 2)).T, LANES, axis=1)

    def trunk(x, prompt):
        t = x.shape[0]
        tm = 512
        tm_route = 1024
        tt = 256
        h, v_rows = _gmlp(x, row2(norm_mix[0]), win, row2(gmlp_norm_v[0]),
                          wmix_p if prompt else wmix_s, bias_p if prompt else bias_s, wout, tm)
        idx, gate = _route(h, row2(norm_ffn[0]), wq_peer[0], sub_keys[0], tm_route)
        h = _gather(idx, gate, h, row2(norm_ffn[0]), row2(norm_final), tables[0], tt, False)
        kv, q = _proj(h, row2(norm_kv), row2(norm_mix[1]), wkv, wq_attn, tm)
        if prompt:
            h = _attn_prompt(attn_sinks, q, kv, h, wo, batch, seq, 256)
        else:
            ck = cache_k_win.reshape(dec_batch * w_cache, KV_DIM // 2)
            cv = cache_v_win.reshape(dec_batch * w_cache, KV_DIM // 2)
            h = _attn_sample(attn_sinks, q, ck, cv, kv, h, wo, dec_batch, 4)
        idx, gate = _route(h, row2(norm_ffn[1]), wq_peer[1], sub_keys[1], tm_route)
        y = _gather(idx, gate, h, row2(norm_ffn[1]), row2(norm_final), tables[1], tt, True)
        return y, kv, v_rows

    y_p, kv_p, _ = trunk(x_prompt.reshape(batch * seq, D_MODEL), True)
    y_s, kv_s, v_s = trunk(x_sample.reshape(dec_batch * dec_seq, D_MODEL), False)

    half = KV_DIM // 2
    kv_p = kv_p.reshape(batch, seq, KV_DIM)[:, seq - WINDOW:]
    prompt_k_win = kv_p[..., :half].reshape(batch, WINDOW, N_KV_HEADS, HEAD_DIM)
    prompt_v_win = kv_p[..., half:].reshape(batch, WINDOW, N_KV_HEADS, HEAD_DIM)
    kv_s = kv_s.reshape(dec_batch, dec_seq, KV_DIM)
    k_s = kv_s[..., :half].reshape(dec_batch, dec_seq, N_KV_HEADS, HEAD_DIM)
    v_s_new = kv_s[..., half:].reshape(dec_batch, dec_seq, N_KV_HEADS, HEAD_DIM)
    sample_k_win = jnp.concatenate([cache_k_win, k_s], axis=1)[:, -w_cache:]
    sample_v_win = jnp.concatenate([cache_v_win, v_s_new], axis=1)[:, -w_cache:]
    sample_gmlp_v = v_s.reshape(1, dec_batch, dec_seq, D_GATE)

    return (y_p.reshape(batch, seq, D_MODEL), y_s.reshape(dec_batch, dec_seq, D_MODEL),
            prompt_k_win, prompt_v_win, sample_k_win, sample_v_win, sample_gmlp_v)
```

```python
import functools
import math

import jax
import jax.numpy as jnp
from jax import lax
from jax.experimental import pallas as pl
from jax.experimental.pallas import tpu as pltpu

F32 = jnp.float32
BF16 = jnp.bfloat16

D_MODEL = 1024
CHUNK = 64
MLP_CHUNK = 128
GMLP_GROUPS = 8
D_GATE = D_MODEL
N_HEADS = 16
N_KV_HEADS = 4
HEAD_DIM = 64
GQA_GROUP = N_HEADS // N_KV_HEADS
WINDOW = 128
PEER_HEADS = 8
N_KEYS = 128
N_EXPERTS = N_KEYS * N_KEYS
PEER_TOPK = 16
D_HALF = 128
N_PICKS = PEER_HEADS * PEER_TOPK
EPS = 1e-6
NEG = -1e30
KV_DIM = 2 * N_KV_HEADS * HEAD_DIM

LANES = 128
SUBLANES = 8
WORD_ROWS = D_MODEL // 2 // LANES
ROWS_PER_EXPERT = 2 * WORD_ROWS
HIGH_HALF = -65536
PICK_PITCH = 12
SLOT_ROWS = N_PICKS * PICK_PITCH
N_SLOTS = SUBLANES
PREFETCH_DIST = 6

VMEM_LIMIT = 48 * 1024 * 1024


def _rms(x, g):
    return x * lax.rsqrt(jnp.mean(x * x, axis=-1, keepdims=True) + EPS) * g


def _cparams(n_axes):
    return pltpu.CompilerParams(
        dimension_semantics=("arbitrary",) * n_axes, vmem_limit_bytes=VMEM_LIMIT)


def _gmlp_kernel(x_ref, gmix_ref, win_ref, gv_ref, wmix_ref, bias_ref, wout_ref,
                 h_ref, v_ref, *, tm):
    x = x_ref[...]
    hn = _rms(x, gmix_ref[...])
    hmid = jax.nn.gelu(jnp.dot(hn.astype(BF16), win_ref[...], preferred_element_type=F32))
    u = hmid[:, :D_GATE]
    v = _rms(hmid[:, D_GATE:], gv_ref[...])
    v_ref[...] = v
    vb = v.astype(BF16)
    bias = bias_ref[...]
    outs = []
    for c in range(tm // MLP_CHUNK):
        r0 = c * MLP_CHUNK
        cols = []
        for g in range(GMLP_GROUPS):
            c0 = g * LANES
            cols.append(jnp.dot(wmix_ref[g], vb[r0:r0 + MLP_CHUNK, c0:c0 + LANES],
                                preferred_element_type=F32))
        mixed = jnp.concatenate(cols, axis=1) + bias
        outs.append(u[r0:r0 + MLP_CHUNK, :] * mixed)
    out = outs[0] if len(outs) == 1 else jnp.concatenate(outs, axis=0)
    y = jnp.dot(out.astype(BF16), wout_ref[...], preferred_element_type=F32)
    h_ref[...] = x + y


def _gmlp(x, gmix, win, gv, wmix, bias, wout, tm):
    t = x.shape[0]
    row = lambda i: (i, 0)
    const2 = lambda i: (0, 0)
    return pl.pallas_call(
        functools.partial(_gmlp_kernel, tm=tm),
        grid=(t // tm,),
        in_specs=[
            pl.BlockSpec((tm, D_MODEL), row),
            pl.BlockSpec((1, D_MODEL), const2),
            pl.BlockSpec((D_MODEL, 2 * D_GATE), const2),
            pl.BlockSpec((1, D_GATE), const2),
            pl.BlockSpec((GMLP_GROUPS, MLP_CHUNK, MLP_CHUNK), lambda i: (0, 0, 0)),
            pl.BlockSpec((MLP_CHUNK, D_GATE), const2),
            pl.BlockSpec((D_GATE, D_MODEL), const2),
        ],
        out_specs=[pl.BlockSpec((tm, D_MODEL), row), pl.BlockSpec((tm, D_GATE), row)],
        out_shape=[jax.ShapeDtypeStruct((t, D_MODEL), F32),
                   jax.ShapeDtypeStruct((t, D_GATE), F32)],
        compiler_params=_cparams(1),
        name="gmlp_mixer",
    )(x, gmix, win, gv, wmix, bias, wout)


CAND_COUNT = [PEER_TOPK // (a + 1) for a in range(PEER_TOPK)]
CAND_OFFSET = [sum(CAND_COUNT[:a]) for a in range(PEER_TOPK)]
CAND_USED = sum(CAND_COUNT)
CAND_ROWS = -(-CAND_USED // SUBLANES) * SUBLANES


def _route_kernel(h_ref, g_ref, wq_ref, sk_ref, idx_ref, gate_ref,
                  sc_s, ts_s, ti_s, cand_s, best_s, sel_s, e_s, gt_s, *, tm):
    hf = _rms(h_ref[...], g_ref[...])
    q = jnp.dot(hf.astype(BF16), wq_ref[...], preferred_element_type=F32).astype(BF16)
    n_hp = 2 * PEER_HEADS
    for hp in range(n_hp):
        sc_s[hp] = lax.dot_general(sk_ref[hp], q[:, hp * D_HALF:(hp + 1) * D_HALF],
                                   (((1,), (1,)), ((), ())), preferred_element_type=F32)

    n_rows = N_KEYS // SUBLANES
    sub_key = lax.broadcasted_iota(jnp.int32, (SUBLANES, LANES), 0).astype(F32)

    def stage1(hp, carry):
        for lg in range(tm // LANES):
            l0 = lg * LANES
            val = [sc_s[hp, v * SUBLANES:(v + 1) * SUBLANES, l0:l0 + LANES] for v in range(n_rows)]
            key = [sub_key + float(v * SUBLANES) for v in range(n_rows)]
            for rnd in range(n_rows):
                for a in range(rnd % 2, n_rows - 1, 2):
                    swap = val[a + 1] > val[a]
                    val[a], val[a + 1] = (jnp.where(swap, val[a + 1], val[a]),
                                          jnp.where(swap, val[a], val[a + 1]))
                    key[a], key[a + 1] = (jnp.where(swap, key[a + 1], key[a]),
                                          jnp.where(swap, key[a], key[a + 1]))
            for k in range(PEER_TOPK):
                m = jnp.max(val[0], axis=0, keepdims=True)
                i = jnp.min(jnp.where(val[0] == m, key[0], float(N_KEYS)), axis=0, keepdims=True)
                ts_s[hp, k:k + 1, l0:l0 + LANES] = m
                ti_s[hp, k:k + 1, l0:l0 + LANES] = i
                if k + 1 < PEER_TOPK:
                    taken = key[0] == i
                    for p in range(PEER_TOPK - 1 - k):
                        val[p] = jnp.where(taken, val[p + 1], val[p])
                        key[p] = jnp.where(taken, key[p + 1], key[p])
        return carry

    lax.fori_loop(0, n_hp, stage1, 0)

    n_cand = float(PEER_TOPK * PEER_TOPK)
    slot = lax.broadcasted_iota(jnp.int32, (CAND_ROWS, LANES), 0)
    cand_flat = slot
    for a in range(1, PEER_TOPK):
        step = (a * PEER_TOPK - CAND_OFFSET[a]) - ((a - 1) * PEER_TOPK - CAND_OFFSET[a - 1])
        cand_flat = cand_flat + jnp.where(slot >= CAND_OFFSET[a], step, 0)
    cand_flat = jnp.where(slot >= CAND_USED, PEER_TOPK * PEER_TOPK, cand_flat).astype(F32)

    def stage2(hd, carry):
        for lg in range(tm // LANES):
            l0 = lg * LANES
            s0 = ts_s[2 * hd, :, l0:l0 + LANES]
            s1 = ts_s[2 * hd + 1, :, l0:l0 + LANES]
            i0 = ti_s[2 * hd, :, l0:l0 + LANES]
            i1 = ti_s[2 * hd + 1, :, l0:l0 + LANES]
            for a in range(PEER_TOPK):
                n = CAND_COUNT[a]
                cand_s[CAND_OFFSET[a]:CAND_OFFSET[a] + n, :] = s0[a:a + 1, :] + s1[0:n, :]
            cand_s[CAND_USED:CAND_ROWS, :] = jnp.full((CAND_ROWS - CAND_USED, LANES), -jnp.inf, F32)
            cand = cand_s[...]
            for k in range(PEER_TOPK):
                m = jnp.max(cand, axis=0, keepdims=True)
                c = jnp.min(jnp.where(cand == m, cand_flat, n_cand), axis=0, keepdims=True)
                best_s[k:k + 1, :] = m
                sel_s[k:k + 1, :] = c
                cand = jnp.where(cand_flat == c, -jnp.inf, cand)
            best = best_s[...]
            sel = sel_s[...]
            sel_a = jnp.floor(sel * (1.0 / PEER_TOPK))
            sel_b = sel - sel_a * PEER_TOPK
            e0 = jnp.zeros_like(sel)
            e1 = jnp.zeros_like(sel)
            for a in range(PEER_TOPK):
                e0 = jnp.where(sel_a == float(a), i0[a:a + 1, :], e0)
                e1 = jnp.where(sel_b == float(a), i1[a:a + 1, :], e1)
            ex = jnp.exp(best - best[0:1, :])
            gate = ex / jnp.sum(ex, axis=0, keepdims=True)
            r0 = pl.multiple_of(hd * PEER_TOPK, PEER_TOPK)
            e_s[pl.ds(r0, PEER_TOPK), l0:l0 + LANES] = e0 * float(N_KEYS) + e1
            gt_s[pl.ds(r0, PEER_TOPK), l0:l0 + LANES] = gate
        return carry

    lax.fori_loop(0, PEER_HEADS, stage2, 0)

    ids = jnp.clip(e_s[...], 0.0, float(N_EXPERTS - 1))
    idx_ref[...] = jnp.transpose(ids).astype(jnp.int32)
    gate_ref[...] = jnp.transpose(gt_s[...])


def _route(h, g, wq, sk, tm):
    t = h.shape[0]
    row = lambda i: (i, 0)
    const2 = lambda i: (0, 0)
    n_hp = 2 * PEER_HEADS
    return pl.pallas_call(
        functools.partial(_route_kernel, tm=tm),
        grid=(t // tm,),
        in_specs=[
            pl.BlockSpec((tm, D_MODEL), row),
            pl.BlockSpec((1, D_MODEL), const2),
            pl.BlockSpec((D_MODEL, n_hp * D_HALF), const2),
            pl.BlockSpec((n_hp, N_KEYS, D_HALF), lambda i: (0, 0, 0)),
        ],
        out_specs=[pl.BlockSpec((tm, N_PICKS), row), pl.BlockSpec((tm, N_PICKS), row)],
        out_shape=[jax.ShapeDtypeStruct((t, N_PICKS), jnp.int32),
                   jax.ShapeDtypeStruct((t, N_PICKS), F32)],
        scratch_shapes=[
            pltpu.VMEM((n_hp, N_KEYS, tm), F32),
            pltpu.VMEM((n_hp, PEER_TOPK, tm), F32),
            pltpu.VMEM((n_hp, PEER_TOPK, tm), F32),
            pltpu.VMEM((CAND_ROWS, LANES), F32),
            pltpu.VMEM((PEER_TOPK, LANES), F32),
            pltpu.VMEM((PEER_TOPK, LANES), F32),
            pltpu.VMEM((N_PICKS, tm), F32),
            pltpu.VMEM((N_PICKS, tm), F32),
        ],
        compiler_params=_cparams(1),
        name="peer_route",
    )(h, g, wq, sk)


def _pack_kernel(u_ref, v_ref, o_ref):
    n = u_ref.shape[0]
    half = D_MODEL // 2
    for src, row0 in ((u_ref, 0), (v_ref, WORD_ROWS)):
        bits = pltpu.bitcast(src[...].astype(BF16).astype(F32), jnp.uint32)
        for t in range(WORD_ROWS):
            lo = bits[:, t * LANES:(t + 1) * LANES] >> 16
            hi = bits[:, half + t * LANES:half + (t + 1) * LANES] & jnp.uint32(0xFFFF0000)
            o_ref[pl.ds(row0 + t, n, stride=ROWS_PER_EXPERT), :] = pltpu.bitcast(lo | hi, jnp.int32)


def _pack_experts(u, v, layer):
    n_blk = 256
    row = lambda i: (i, 0)
    layer_row = lambda i: (layer, i, 0)
    out = pl.pallas_call(
        _pack_kernel,
        grid=(N_EXPERTS // n_blk,),
        in_specs=[pl.BlockSpec((None, n_blk, D_MODEL), layer_row),
                  pl.BlockSpec((None, n_blk, D_MODEL), layer_row)],
        out_specs=pl.BlockSpec((n_blk * ROWS_PER_EXPERT, LANES), row),
        out_shape=jax.ShapeDtypeStruct((N_EXPERTS * ROWS_PER_EXPERT, LANES), jnp.int32),
        compiler_params=_cparams(1),
        name="pack_experts",
    )(u, v)
    return out.reshape(N_EXPERTS, ROWS_PER_EXPERT, LANES)


def _gather_kernel(idx_ref, gate_ref, h_ref, g_ref, gfin_ref, tab_ref, out_ref,
                   hf_s, buf, sem, *, tt, final_norm):
    hf_s[...] = _rms(h_ref[...], g_ref[...])
    n_sub = D_MODEL // LANES
    n_groups = tt // N_SLOTS
    n_pieces = 2 * WORD_ROWS
    picks_per_piece = N_PICKS // n_pieces

    def unpack(words):
        return (pltpu.bitcast(words << 16, F32), pltpu.bitcast(words & HIGH_HALF, F32))

    def pick_copy(e, slot, r):
        dst = buf.at[pl.ds(slot * SLOT_ROWS + r * PICK_PITCH, ROWS_PER_EXPERT), :]
        return pltpu.make_async_copy(tab_ref.at[e], dst, sem.at[slot])

    def issue(j, slot, r_lo, r_hi):
        for r in range(r_lo, r_hi):
            pick_copy(idx_ref[j, r], slot, r).start(priority=r % 2)

    def wait(slot):
        for r in range(N_PICKS):
            pick_copy(0, slot, r).wait()

    eye = (lax.broadcasted_iota(jnp.int32, (N_PICKS, N_PICKS), 0)
           == lax.broadcasted_iota(jnp.int32, (N_PICKS, N_PICKS), 1))
    sub = lax.broadcasted_iota(jnp.int32, (SUBLANES, LANES), 0)

    def token_pair(k, x8, gate8, tiles, prefetch):
        wait(k)
        wait(k + 1)
        pair = (k, k + 1)
        piece = 0
        acc = [jnp.zeros((N_PICKS, LANES), F32) for _ in pair]
        for t in range(WORD_ROWS):
            for i, kk in enumerate(pair):
                u_lo, u_hi = unpack(buf[pl.ds(kk * SLOT_ROWS + t, N_PICKS, stride=PICK_PITCH), :])
                acc[i] = (acc[i] + u_lo * x8[t][kk:kk + 1, :]
                          + u_hi * x8[WORD_ROWS + t][kk:kk + 1, :])
                prefetch(piece)
                piece += 1
        w = []
        for i, kk in enumerate(pair):
            act = jax.nn.gelu(jnp.sum(acc[i], axis=1, keepdims=True))
            gcol = jnp.sum(jnp.where(eye, gate8[kk:kk + 1, :], 0.0), axis=1, keepdims=True)
            w.append(gcol * act)
        new_tiles = list(tiles)
        for t in range(WORD_ROWS):
            for i, kk in enumerate(pair):
                v_lo, v_hi = unpack(buf[pl.ds(kk * SLOT_ROWS + WORD_ROWS + t, N_PICKS,
                                              stride=PICK_PITCH), :])
                o_lo = jnp.sum(v_lo * w[i], axis=0, keepdims=True)
                o_hi = jnp.sum(v_hi * w[i], axis=0, keepdims=True)
                new_tiles[t] = jnp.where(sub == kk, o_lo, new_tiles[t])
                new_tiles[WORD_ROWS + t] = jnp.where(sub == kk, o_hi, new_tiles[WORD_ROWS + t])
                prefetch(piece)
                piece += 1
        return new_tiles

    def group(gi, last):
        r0 = pl.multiple_of(gi * N_SLOTS, N_SLOTS)
        x8 = [hf_s[pl.ds(r0, N_SLOTS), s * LANES:(s + 1) * LANES] for s in range(n_sub)]
        gate8 = gate_ref[pl.ds(r0, N_SLOTS), :]
        tiles = [jnp.zeros((SUBLANES, LANES), F32) for _ in range(n_sub)]
        for k in range(0, N_SLOTS, 2):
            def prefetch(piece, k=k):
                ahead = k + PREFETCH_DIST + piece // n_pieces
                if last and ahead >= N_SLOTS:
                    return
                sub_piece = piece % n_pieces
                issue(r0 + ahead, ahead % N_SLOTS,
                      sub_piece * picks_per_piece, (sub_piece + 1) * picks_per_piece)
            tiles = token_pair(k, x8, gate8, tiles, prefetch)
        res = [h_ref[pl.ds(r0, N_SLOTS), s * LANES:(s + 1) * LANES] + tiles[s] for s in range(n_sub)]
        if final_norm:
            ss = res[0] * res[0]
            for s in range(1, n_sub):
                ss = ss + res[s] * res[s]
            scale = lax.rsqrt(jnp.sum(ss, axis=1, keepdims=True) * (1.0 / D_MODEL) + EPS)
            res = [res[s] * scale * gfin_ref[:, s * LANES:(s + 1) * LANES] for s in range(n_sub)]
        for s in range(n_sub):
            out_ref[pl.ds(r0, N_SLOTS), s * LANES:(s + 1) * LANES] = res[s]

    for j in range(PREFETCH_DIST):
        issue(j, j, 0, N_PICKS)

    def body(gi, carry):
        group(gi, False)
        return carry

    lax.fori_loop(0, n_groups - 1, body, 0)
    group(n_groups - 1, True)


def _gather(idx, gate, h, g, gfin, tab, tt, final_norm):
    t = h.shape[0]
    row = lambda i: (i, 0)
    const2 = lambda i: (0, 0)
    return pl.pallas_call(
        functools.partial(_gather_kernel, tt=tt, final_norm=final_norm),
        grid=(t // tt,),
        in_specs=[
            pl.BlockSpec((tt, N_PICKS), row, memory_space=pltpu.SMEM),
            pl.BlockSpec((tt, N_PICKS), row),
            pl.BlockSpec((tt, D_MODEL), row),
            pl.BlockSpec((1, D_MODEL), const2),
            pl.BlockSpec((1, D_MODEL), const2),
            pl.BlockSpec(memory_space=pl.ANY),
        ],
        out_specs=pl.BlockSpec((tt, D_MODEL), row),
        out_shape=jax.ShapeDtypeStruct((t, D_MODEL), F32),
        scratch_shapes=[
            pltpu.VMEM((tt, D_MODEL), F32),
            pltpu.VMEM((N_SLOTS * SLOT_ROWS, LANES), jnp.int32),
            pltpu.SemaphoreType.DMA((N_SLOTS,)),
        ],
        compiler_params=_cparams(1),
        name="peer_experts",
    )(idx, gate, h, g, gfin, tab)


def _proj_kernel(h_ref, gkv_ref, gq_ref, wkv_ref, wq_ref, kv_ref, q_ref):
    h = h_ref[...]
    n = h * lax.rsqrt(jnp.mean(h * h, axis=-1, keepdims=True) + EPS)
    kv_ref[...] = jnp.dot((n * gkv_ref[...]).astype(BF16), wkv_ref[...], preferred_element_type=F32)
    q_ref[...] = jnp.dot((n * gq_ref[...]).astype(BF16), wq_ref[...],
                         preferred_element_type=F32).astype(BF16)


def _proj(h, gkv, gq, wkv, wq, tm):
    t = h.shape[0]
    row = lambda i: (i, 0)
    const2 = lambda i: (0, 0)
    return pl.pallas_call(
        _proj_kernel,
        grid=(t // tm,),
        in_specs=[
            pl.BlockSpec((tm, D_MODEL), row),
            pl.BlockSpec((1, D_MODEL), const2),
            pl.BlockSpec((1, D_MODEL), const2),
            pl.BlockSpec((D_MODEL, KV_DIM), const2),
            pl.BlockSpec((D_MODEL, D_MODEL), const2),
        ],
        out_specs=[pl.BlockSpec((tm, KV_DIM), row), pl.BlockSpec((tm, D_MODEL), row)],
        out_shape=[jax.ShapeDtypeStruct((t, KV_DIM), F32),
                   jax.ShapeDtypeStruct((t, D_MODEL), BF16)],
        compiler_params=_cparams(1),
        name="kv_q_proj",
    )(h, gkv, gq, wkv, wq)


def _alibi_slopes():
    return [2.0 ** (-8.0 * (h + 1) / N_HEADS) for h in range(N_HEADS)]


ATTN_KEYS = WINDOW + CHUNK
ATTN_PAIR = 2 * CHUNK
ATTN_ROW_BLOCK = 32


def _padded_heads(tile, g):
    low = lax.broadcasted_iota(jnp.int32, tile.shape, 1) < HEAD_DIM
    rolled = pltpu.roll(tile, HEAD_DIM, axis=1)
    if g % 2 == 0:
        pads = [jnp.where(low, tile, 0.0), jnp.where(low, 0.0, rolled)]
    else:
        pads = [jnp.where(low, rolled, 0.0), jnp.where(low, 0.0, tile)]
    return [p.astype(BF16) for p in pads]


def _attn_chunks(q_ref, windows, first_valids, sink_ref, s_scr, p_scr):
    half_kv = KV_DIM // 2
    slopes = _alibi_slopes()
    n_chunks = len(windows)

    def chain(c, g, half):
        return (c * N_KV_HEADS + g) * 2 + half

    for c in range(n_chunks):
        win = windows[c]()
        r0 = c * CHUNK
        for g in range(N_KV_HEADS):
            t = g // 2
            k_pad = _padded_heads(win[:, t * LANES:(t + 1) * LANES], g)
            qstack = jnp.concatenate(
                [q_ref[r0:r0 + CHUNK, (2 * g) * LANES:(2 * g + 1) * LANES],
                 q_ref[r0:r0 + CHUNK, (2 * g + 1) * LANES:(2 * g + 2) * LANES]], axis=0)
            for half in range(2):
                s_scr[chain(c, g, half)] = lax.dot_general(
                    qstack, k_pad[half], (((1,), (1,)), ((), ())), preferred_element_type=F32)

    rb = ATTN_ROW_BLOCK
    qi = lax.broadcasted_iota(jnp.int32, (rb, ATTN_KEYS), 0)
    kj = lax.broadcasted_iota(jnp.int32, (rb, ATTN_KEYS), 1)
    dists = [jnp.abs(WINDOW + qi + off - kj).astype(F32) for off in range(0, CHUNK, rb)]
    for c in range(n_chunks):
        valid = kj >= first_valids[c]
        for g in range(N_KV_HEADS):
            for half in range(2):
                ch = chain(c, g, half)
                for b in range(ATTN_PAIR // rb):
                    row0 = b * rb
                    head = GQA_GROUP * g + half + (2 if row0 >= CHUNK else 0)
                    s = s_scr[ch, row0:row0 + rb, :]
                    s = s * (HEAD_DIM ** -0.5) - slopes[head] * dists[(row0 % CHUNK) // rb]
                    s = jnp.where(valid, s, NEG)
                    sink = sink_ref[0, head]
                    m = jnp.maximum(jnp.max(s, axis=1, keepdims=True), sink)
                    e = jnp.exp(s - m)
                    denom = jnp.sum(e, axis=1, keepdims=True) + jnp.exp(sink - m)
                    p_scr[ch, row0:row0 + rb, :] = (e / denom).astype(BF16)

    outs = []
    for c in range(n_chunks):
        win = windows[c]()
        o_tiles = [None] * (N_HEADS // 2)
        for g in range(N_KV_HEADS):
            t = g // 2
            v_pad = _padded_heads(win[:, half_kv + t * LANES:half_kv + (t + 1) * LANES], g)
            o_stack = (jnp.dot(p_scr[chain(c, g, 0)], v_pad[0], preferred_element_type=F32)
                       + jnp.dot(p_scr[chain(c, g, 1)], v_pad[1], preferred_element_type=F32))
            o_tiles[2 * g] = o_stack[:CHUNK]
            o_tiles[2 * g + 1] = o_stack[CHUNK:]
        outs.append(jnp.concatenate(o_tiles, axis=1))
    return outs[0] if n_chunks == 1 else jnp.concatenate(outs, axis=0)


def _attn_scratch(n_chunks):
    n_chains = n_chunks * N_KV_HEADS * 2
    return [pltpu.VMEM((n_chains, ATTN_PAIR, ATTN_KEYS), F32),
            pltpu.VMEM((n_chains, ATTN_PAIR, ATTN_KEYS), BF16)]


def _attn_prompt_kernel(sink_ref, q_ref, kvp_ref, kvc_ref, h_ref, wo_ref, out_ref,
                        kv_s, s_scr, p_scr, *, qt):
    t = pl.program_id(1)
    kv_s[0:WINDOW, :] = kvp_ref[...]
    kv_s[WINDOW:WINDOW + qt, :] = kvc_ref[...]
    n_chunks = qt // CHUNK
    windows = [functools.partial(lambda r0: kv_s[r0:r0 + ATTN_KEYS, :], c * CHUNK)
               for c in range(n_chunks)]
    first_valids = [WINDOW - CHUNK * (t * n_chunks + c) for c in range(n_chunks)]
    o = _attn_chunks(q_ref, windows, first_valids, sink_ref, s_scr, p_scr).astype(BF16)
    out_ref[...] = h_ref[...] + jnp.dot(o, wo_ref[...], preferred_element_type=F32)


def _attn_prompt(sinks, q, kv, h, wo, batch, seq, qt):
    t = h.shape[0]
    tiles = seq // qt
    cur = lambda b, i: (b * tiles + i, 0)
    prev = lambda b, i: (jnp.maximum((b * tiles + i) * (qt // WINDOW) - 1, 0), 0)
    const2 = lambda b, i: (0, 0)
    return pl.pallas_call(
        functools.partial(_attn_prompt_kernel, qt=qt),
        grid=(batch, tiles),
        in_specs=[
            pl.BlockSpec((1, N_HEADS), const2, memory_space=pltpu.SMEM),
            pl.BlockSpec((qt, D_MODEL), cur),
            pl.BlockSpec((WINDOW, KV_DIM), prev),
            pl.BlockSpec((qt, KV_DIM), cur),
            pl.BlockSpec((qt, D_MODEL), cur),
            pl.BlockSpec((D_MODEL, D_MODEL), const2),
        ],
        out_specs=pl.BlockSpec((qt, D_MODEL), cur),
        out_shape=jax.ShapeDtypeStruct((t, D_MODEL), F32),
        scratch_shapes=[pltpu.VMEM((WINDOW + qt, KV_DIM), F32)] + _attn_scratch(qt // CHUNK),
        compiler_params=_cparams(2),
        name="attn_prompt",
    )(sinks, q, kv, kv, h, wo)


def _attn_sample_kernel(sink_ref, q_ref, ck_ref, cv_ref, kvn_ref, h_ref, wo_ref, out_ref,
                        s_scr, p_scr, *, nb):
    half_kv = KV_DIM // 2

    def window(b):
        kvn = kvn_ref[b * CHUNK:(b + 1) * CHUNK, :]
        k = jnp.concatenate([ck_ref[b * WINDOW:(b + 1) * WINDOW, :], kvn[:, :half_kv]], axis=0)
        v = jnp.concatenate([cv_ref[b * WINDOW:(b + 1) * WINDOW, :], kvn[:, half_kv:]], axis=0)
        return jnp.concatenate([k, v], axis=1)

    windows = [functools.partial(window, b) for b in range(nb)]
    o = _attn_chunks(q_ref, windows, [0] * nb, sink_ref, s_scr, p_scr).astype(BF16)
    out_ref[...] = h_ref[...] + jnp.dot(o, wo_ref[...], preferred_element_type=F32)


def _attn_sample(sinks, q, ck, cv, kv, h, wo, batch, nb):
    t = h.shape[0]
    row = lambda b: (b, 0)
    const2 = lambda b: (0, 0)
    return pl.pallas_call(
        functools.partial(_attn_sample_kernel, nb=nb),
        grid=(batch // nb,),
        in_specs=[
            pl.BlockSpec((1, N_HEADS), const2, memory_space=pltpu.SMEM),
            pl.BlockSpec((nb * CHUNK, D_MODEL), row),
            pl.BlockSpec((nb * WINDOW, KV_DIM // 2), row),
            pl.BlockSpec((nb * WINDOW, KV_DIM // 2), row),
            pl.BlockSpec((nb * CHUNK, KV_DIM), row),
            pl.BlockSpec((nb * CHUNK, D_MODEL), row),
            pl.BlockSpec((D_MODEL, D_MODEL), const2),
        ],
        scratch_shapes=_attn_scratch(nb),
        out_specs=pl.BlockSpec((nb * CHUNK, D_MODEL), row),
        out_shape=jax.ShapeDtypeStruct((t, D_MODEL), F32),
        compiler_params=_cparams(1),
        name="attn_sample",
    )(sinks, q, ck, cv, kv, h, wo)


def kernel(x_prompt, x_sample, cache_k_win, cache_v_win, norm_mix, norm_ffn, gmlp_w_in, gmlp_norm_v, gmlp_w_s, gmlp_b_s, gmlp_w_out, norm_kv, w_kv, attn_w_q, attn_sinks, attn_w_o, peer_w_q, peer_sub_keys, peer_u, peer_v, norm_final):
    batch, seq, _ = x_prompt.shape
    dec_batch, dec_seq, _ = x_sample.shape
    w_cache = cache_k_win.shape[1]
    assert dec_seq == CHUNK and w_cache == WINDOW and seq % MLP_CHUNK == 0

    row2 = lambda a: a.reshape(1, -1)
    win = gmlp_w_in[0].astype(BF16)
    wout = gmlp_w_out[0].astype(BF16)
    wkv = w_kv.astype(BF16)
    wq_attn = attn_w_q[0].astype(BF16)
    wo = attn_w_o[0].astype(BF16)
    wq_peer = peer_w_q.astype(BF16)
    sub_keys = peer_sub_keys.reshape(2, 2 * PEER_HEADS, N_KEYS, D_HALF).astype(BF16)
    tables = [_pack_experts(peer_u, peer_v, l) for l in range(2)]

    pos = jnp.arange(MLP_CHUNK)
    blk_i = pos[:, None] // CHUNK
    blk_j = pos[None, :] // CHUNK
    ws = gmlp_w_s[0]
    bs = gmlp_b_s[0]
    wmix_p = jnp.where((blk_j <= blk_i)[None], ws, 0.0).astype(BF16)
    ws64 = ws[:, :CHUNK, :CHUNK]
    wmix_s = jnp.where((blk_j == blk_i)[None], jnp.tile(ws64, (1, 2, 2)), 0.0).astype(BF16)
    bias_p = jnp.repeat(bs.T, LANES, axis=1)
    bias_s = jnp.repeat(jnp.tile(bs[:, :CHUNK], (1, 2)).T, LANES, axis=1)

    def trunk(x, prompt):
        t = x.shape[0]
        tm = 512
        tm_route = 1024
        tt = 256
        h, v_rows = _gmlp(x, row2(norm_mix[0]), win, row2(gmlp_norm_v[0]),
                          wmix_p if prompt else wmix_s, bias_p if prompt else bias_s, wout, tm)
        idx, gate = _route(h, row2(norm_ffn[0]), wq_peer[0], sub_keys[0], tm_route)
        h = _gather(idx, gate, h, row2(norm_ffn[0]), row2(norm_final), tables[0], tt, False)
        kv, q = _proj(h, row2(norm_kv), row2(norm_mix[1]), wkv, wq_attn, tm)
        if prompt:
            h = _attn_prompt(attn_sinks, q, kv, h, wo, batch, seq, 256)
        else:
            ck = cache_k_win.reshape(dec_batch * w_cache, KV_DIM // 2)
            cv = cache_v_win.reshape(dec_batch * w_cache, KV_DIM // 2)
            h = _attn_sample(attn_sinks, q, ck, cv, kv, h, wo, dec_batch, 4)
        idx, gate = _route(h, row2(norm_ffn[1]), wq_peer[1], sub_keys[1], tm_route)
        y = _gather(idx, gate, h, row2(norm_ffn[1]), row2(norm_final), tables[1], tt, True)
        return y, kv, v_rows

    y_p, kv_p, _ = trunk(x_prompt.reshape(batch * seq, D_MODEL), True)
    y_s, kv_s, v_s = trunk(x_sample.reshape(dec_batch * dec_seq, D_MODEL), False)

    half = KV_DIM // 2
    kv_p = kv_p.reshape(batch, seq, KV_DIM)[:, seq - WINDOW:]
    prompt_k_win = kv_p[..., :half].reshape(batch, WINDOW, N_KV_HEADS, HEAD_DIM)
    prompt_v_win = kv_p[..., half:].reshape(batch, WINDOW, N_KV_HEADS, HEAD_DIM)
    kv_s = kv_s.reshape(dec_batch, dec_seq, KV_DIM)
    k_s = kv_s[..., :half].reshape(dec_batch, dec_seq, N_KV_HEADS, HEAD_DIM)
    v_s_new = kv_s[..., half:].reshape(dec_batch, dec_seq, N_KV_HEADS, HEAD_DIM)
    sample_k_win = jnp.concatenate([cache_k_win, k_s], axis=1)[:, -w_cache:]
    sample_v_win = jnp.concatenate([cache_v_win, v_s_new], axis=1)[:, -w_cache:]
    sample_gmlp_v = v_s.reshape(1, dec_batch, dec_seq, D_GATE)

    return (y_p.reshape(batch, seq, D_MODEL), y_s.reshape(dec_batch, dec_seq, D_MODEL),
            prompt_k_win, prompt_v_win, sample_k_win, sample_v_win, sample_gmlp_v)
```

```python
import functools
import math

import jax
import jax.numpy as jnp
from jax import lax
from jax.experimental import pallas as pl
from jax.experimental.pallas import tpu as pltpu

F32 = jnp.float32
BF16 = jnp.bfloat16

D_MODEL = 1024
CHUNK = 64
MLP_CHUNK = 128
GMLP_GROUPS = 8
D_GATE = D_MODEL
N_HEADS = 16
N_KV_HEADS = 4
HEAD_DIM = 64
GQA_GROUP = N_HEADS // N_KV_HEADS
WINDOW = 128
PEER_HEADS = 8
N_KEYS = 128
N_EXPERTS = N_KEYS * N_KEYS
PEER_TOPK = 16
D_HALF = 128
N_PICKS = PEER_HEADS * PEER_TOPK
EPS = 1e-6
NEG = -1e30
KV_DIM = 2 * N_KV_HEADS * HEAD_DIM

LANES = 128
SUBLANES = 8
WORD_ROWS = D_MODEL // 2 // LANES
ROWS_PER_EXPERT = 2 * WORD_ROWS
HIGH_HALF = -65536
PICK_PITCH = 12
SLOT_ROWS = N_PICKS * PICK_PITCH
N_SLOTS = SUBLANES
PREFETCH_DIST = 6

VMEM_LIMIT = 48 * 1024 * 1024


def _rms(x, g):
    return x * lax.rsqrt(jnp.mean(x * x, axis=-1, keepdims=True) + EPS) * g


def _cparams(n_axes):
    return pltpu.CompilerParams(
        dimension_semantics=("arbitrary",) * n_axes, vmem_limit_bytes=VMEM_LIMIT)


def _gmlp_kernel(x_ref, gmix_ref, win_ref, gv_ref, wmix_ref, bias_ref, wout_ref,
                 h_ref, v_ref, *, tm):
    x = x_ref[...]
    hn = _rms(x, gmix_ref[...])
    hmid = jax.nn.gelu(jnp.dot(hn.astype(BF16), win_ref[...], preferred_element_type=F32))
    u = hmid[:, :D_GATE]
    v = _rms(hmid[:, D_GATE:], gv_ref[...])
    v_ref[...] = v
    vb = v.astype(BF16)
    bias = bias_ref[...]
    outs = []
    for c in range(tm // MLP_CHUNK):
        r0 = c * MLP_CHUNK
        cols = []
        for g in range(GMLP_GROUPS):
            c0 = g * LANES
            cols.append(jnp.dot(wmix_ref[g], vb[r0:r0 + MLP_CHUNK, c0:c0 + LANES],
                                preferred_element_type=F32))
        mixed = jnp.concatenate(cols, axis=1) + bias
        outs.append(u[r0:r0 + MLP_CHUNK, :] * mixed)
    out = outs[0] if len(outs) == 1 else jnp.concatenate(outs, axis=0)
    y = jnp.dot(out.astype(BF16), wout_ref[...], preferred_element_type=F32)
    h_ref[...] = x + y


def _gmlp(x, gmix, win, gv, wmix, bias, wout, tm):
    t = x.shape[0]
    row = lambda i: (i, 0)
    const2 = lambda i: (0, 0)
    return pl.pallas_call(
        functools.partial(_gmlp_kernel, tm=tm),
        grid=(t // tm,),
        in_specs=[
            pl.BlockSpec((tm, D_MODEL), row),
            pl.BlockSpec((1, D_MODEL), const2),
            pl.BlockSpec((D_MODEL, 2 * D_GATE), const2),
            pl.BlockSpec((1, D_GATE), const2),
            pl.BlockSpec((GMLP_GROUPS, MLP_CHUNK, MLP_CHUNK), lambda i: (0, 0, 0)),
            pl.BlockSpec((MLP_CHUNK, D_GATE), const2),
            pl.BlockSpec((D_GATE, D_MODEL), const2),
        ],
        out_specs=[pl.BlockSpec((tm, D_MODEL), row), pl.BlockSpec((tm, D_GATE), row)],
        out_shape=[jax.ShapeDtypeStruct((t, D_MODEL), F32),
                   jax.ShapeDtypeStruct((t, D_GATE), F32)],
        compiler_params=_cparams(1),
        name="gmlp_mixer",
    )(x, gmix, win, gv, wmix, bias, wout)


def _merge_exchange_network(lo, hi, r=None):
    def merge(lo, hi, r):
        step = r * 2
        if step < hi - lo:
            yield from merge(lo, hi, step)
            yield from merge(lo + r, hi, step)
            yield from [(i, i + r) for i in range(lo + r, hi - r, step)]
        else:
            yield (lo, lo + r)

    if hi - lo >= 1:
        mid = lo + (hi - lo) // 2
        yield from _merge_exchange_network(lo, mid)
        yield from _merge_exchange_network(mid + 1, hi)
        yield from merge(lo, hi, 1)


SORT16_NETWORK = list(_merge_exchange_network(0, N_KEYS // SUBLANES - 1))

CAND_COUNT = [PEER_TOPK // (a + 1) for a in range(PEER_TOPK)]
CAND_OFFSET = [sum(CAND_COUNT[:a]) for a in range(PEER_TOPK)]
CAND_USED = sum(CAND_COUNT)
CAND_ROWS = -(-CAND_USED // SUBLANES) * SUBLANES


def _route_kernel(h_ref, g_ref, wq_ref, sk_ref, idx_ref, gate_ref,
                  sc_s, ts_s, ti_s, cand_s, best_s, sel_s, e_s, gt_s, *, tm):
    hf = _rms(h_ref[...], g_ref[...])
    q = jnp.dot(hf.astype(BF16), wq_ref[...], preferred_element_type=F32).astype(BF16)
    n_hp = 2 * PEER_HEADS
    for hp in range(n_hp):
        sc_s[hp] = lax.dot_general(sk_ref[hp], q[:, hp * D_HALF:(hp + 1) * D_HALF],
                                   (((1,), (1,)), ((), ())), preferred_element_type=F32)

    n_rows = N_KEYS // SUBLANES
    sub_key = lax.broadcasted_iota(jnp.int32, (SUBLANES, LANES), 0).astype(F32)

    def stage1(hp, carry):
        for lg in range(tm // LANES):
            l0 = lg * LANES
            val = [sc_s[hp, v * SUBLANES:(v + 1) * SUBLANES, l0:l0 + LANES] for v in range(n_rows)]
            key = [sub_key + float(v * SUBLANES) for v in range(n_rows)]
            for a, b in SORT16_NETWORK:
                swap = (val[b] > val[a]) | ((val[b] == val[a]) & (key[b] < key[a]))
                val[a], val[b] = jnp.where(swap, val[b], val[a]), jnp.where(swap, val[a], val[b])
                key[a], key[b] = jnp.where(swap, key[b], key[a]), jnp.where(swap, key[a], key[b])
            for k in range(PEER_TOPK):
                m = jnp.max(val[0], axis=0, keepdims=True)
                i = jnp.min(jnp.where(val[0] == m, key[0], float(N_KEYS)), axis=0, keepdims=True)
                ts_s[hp, k:k + 1, l0:l0 + LANES] = m
                ti_s[hp, k:k + 1, l0:l0 + LANES] = i
                if k + 1 < PEER_TOPK:
                    taken = key[0] == i
                    for p in range(PEER_TOPK - 1 - k):
                        val[p] = jnp.where(taken, val[p + 1], val[p])
                        key[p] = jnp.where(taken, key[p + 1], key[p])
        return carry

    lax.fori_loop(0, n_hp, stage1, 0)

    n_cand = float(PEER_TOPK * PEER_TOPK)
    slot = lax.broadcasted_iota(jnp.int32, (CAND_ROWS, LANES), 0)
    cand_flat = slot
    for a in range(1, PEER_TOPK):
        step = (a * PEER_TOPK - CAND_OFFSET[a]) - ((a - 1) * PEER_TOPK - CAND_OFFSET[a - 1])
        cand_flat = cand_flat + jnp.where(slot >= CAND_OFFSET[a], step, 0)
    cand_flat = jnp.where(slot >= CAND_USED, PEER_TOPK * PEER_TOPK, cand_flat).astype(F32)

    def stage2(hd, carry):
        for lg in range(tm // LANES):
            l0 = lg * LANES
            s0 = ts_s[2 * hd, :, l0:l0 + LANES]
            s1 = ts_s[2 * hd + 1, :, l0:l0 + LANES]
            i0 = ti_s[2 * hd, :, l0:l0 + LANES]
            i1 = ti_s[2 * hd + 1, :, l0:l0 + LANES]
            for a in range(PEER_TOPK):
                n = CAND_COUNT[a]
                cand_s[CAND_OFFSET[a]:CAND_OFFSET[a] + n, :] = s0[a:a + 1, :] + s1[0:n, :]
            cand_s[CAND_USED:CAND_ROWS, :] = jnp.full((CAND_ROWS - CAND_USED, LANES), -jnp.inf, F32)
            cand = cand_s[...]
            for k in range(PEER_TOPK):
                m = jnp.max(cand, axis=0, keepdims=True)
                c = jnp.min(jnp.where(cand == m, cand_flat, n_cand), axis=0, keepdims=True)
                best_s[k:k + 1, :] = m
                sel_s[k:k + 1, :] = c
                cand = jnp.where(cand_flat == c, -jnp.inf, cand)
            best = best_s[...]
            sel = sel_s[...]
            sel_a = jnp.floor(sel * (1.0 / PEER_TOPK))
            sel_b = sel - sel_a * PEER_TOPK
            e0 = jnp.zeros_like(sel)
            e1 = jnp.zeros_like(sel)
            for a in range(PEER_TOPK):
                e0 = jnp.where(sel_a == float(a), i0[a:a + 1, :], e0)
                e1 = jnp.where(sel_b == float(a), i1[a:a + 1, :], e1)
            ex = jnp.exp(best - best[0:1, :])
            gate = ex / jnp.sum(ex, axis=0, keepdims=True)
            r0 = pl.multiple_of(hd * PEER_TOPK, PEER_TOPK)
            e_s[pl.ds(r0, PEER_TOPK), l0:l0 + LANES] = e0 * float(N_KEYS) + e1
            gt_s[pl.ds(r0, PEER_TOPK), l0:l0 + LANES] = gate
        return carry

    lax.fori_loop(0, PEER_HEADS, stage2, 0)

    ids = jnp.clip(e_s[...], 0.0, float(N_EXPERTS - 1))
    idx_ref[...] = jnp.transpose(ids).astype(jnp.int32)
    gate_ref[...] = jnp.transpose(gt_s[...])


def _route(h, g, wq, sk, tm):
    t = h.shape[0]
    row = lambda i: (i, 0)
    const2 = lambda i: (0, 0)
    n_hp = 2 * PEER_HEADS
    return pl.pallas_call(
        functools.partial(_route_kernel, tm=tm),
        grid=(t // tm,),
        in_specs=[
            pl.BlockSpec((tm, D_MODEL), row),
            pl.BlockSpec((1, D_MODEL), const2),
            pl.BlockSpec((D_MODEL, n_hp * D_HALF), const2),
            pl.BlockSpec((n_hp, N_KEYS, D_HALF), lambda i: (0, 0, 0)),
        ],
        out_specs=[pl.BlockSpec((tm, N_PICKS), row), pl.BlockSpec((tm, N_PICKS), row)],
        out_shape=[jax.ShapeDtypeStruct((t, N_PICKS), jnp.int32),
                   jax.ShapeDtypeStruct((t, N_PICKS), F32)],
        scratch_shapes=[
            pltpu.VMEM((n_hp, N_KEYS, tm), F32),
            pltpu.VMEM((n_hp, PEER_TOPK, tm), F32),
            pltpu.VMEM((n_hp, PEER_TOPK, tm), F32),
            pltpu.VMEM((CAND_ROWS, LANES), F32),
            pltpu.VMEM((PEER_TOPK, LANES), F32),
            pltpu.VMEM((PEER_TOPK, LANES), F32),
            pltpu.VMEM((N_PICKS, tm), F32),
            pltpu.VMEM((N_PICKS, tm), F32),
        ],
        compiler_params=_cparams(1),
        name="peer_route",
    )(h, g, wq, sk)


def _pack_kernel(u_ref, v_ref, o_ref):
    n = u_ref.shape[0]
    half = D_MODEL // 2
    for src, row0 in ((u_ref, 0), (v_ref, WORD_ROWS)):
        bits = pltpu.bitcast(src[...].astype(BF16).astype(F32), jnp.uint32)
        for t in range(WORD_ROWS):
            lo = bits[:, t * LANES:(t + 1) * LANES] >> 16
            hi = bits[:, half + t * LANES:half + (t + 1) * LANES] & jnp.uint32(0xFFFF0000)
            o_ref[pl.ds(row0 + t, n, stride=ROWS_PER_EXPERT), :] = pltpu.bitcast(lo | hi, jnp.int32)


def _pack_experts(u, v, layer):
    n_blk = 256
    row = lambda i: (i, 0)
    layer_row = lambda i: (layer, i, 0)
    out = pl.pallas_call(
        _pack_kernel,
        grid=(N_EXPERTS // n_blk,),
        in_specs=[pl.BlockSpec((None, n_blk, D_MODEL), layer_row),
                  pl.BlockSpec((None, n_blk, D_MODEL), layer_row)],
        out_specs=pl.BlockSpec((n_blk * ROWS_PER_EXPERT, LANES), row),
        out_shape=jax.ShapeDtypeStruct((N_EXPERTS * ROWS_PER_EXPERT, LANES), jnp.int32),
        compiler_params=_cparams(1),
        name="pack_experts",
    )(u, v)
    return out.reshape(N_EXPERTS, ROWS_PER_EXPERT, LANES)


def _gather_kernel(idx_ref, gate_ref, h_ref, g_ref, gfin_ref, tab_ref, out_ref,
                   hf_s, buf, sem, *, tt, final_norm):
    hf_s[...] = _rms(h_ref[...], g_ref[...])
    n_sub = D_MODEL // LANES
    n_groups = tt // N_SLOTS
    n_pieces = 2 * WORD_ROWS
    picks_per_piece = N_PICKS // n_pieces

    def unpack(words):
        return (pltpu.bitcast(words << 16, F32), pltpu.bitcast(words & HIGH_HALF, F32))

    def pick_copy(e, slot, r):
        dst = buf.at[pl.ds(slot * SLOT_ROWS + r * PICK_PITCH, ROWS_PER_EXPERT), :]
        return pltpu.make_async_copy(tab_ref.at[e], dst, sem.at[slot])

    def issue(j, slot, r_lo, r_hi):
        for r in range(r_lo, r_hi):
            pick_copy(idx_ref[j, r], slot, r).start(priority=r % 2)

    def wait(slot):
        for r in range(N_PICKS):
            pick_copy(0, slot, r).wait()

    eye = (lax.broadcasted_iota(jnp.int32, (N_PICKS, N_PICKS), 0)
           == lax.broadcasted_iota(jnp.int32, (N_PICKS, N_PICKS), 1))
    sub = lax.broadcasted_iota(jnp.int32, (SUBLANES, LANES), 0)

    def token(k, x8, gate8, tiles, prefetch):
        base = k * SLOT_ROWS
        wait(k)
        piece = 0
        acc = jnp.zeros((N_PICKS, LANES), F32)
        for t in range(WORD_ROWS):
            u_lo, u_hi = unpack(buf[pl.ds(base + t, N_PICKS, stride=PICK_PITCH), :])
            acc = acc + u_lo * x8[t][k:k + 1, :] + u_hi * x8[WORD_ROWS + t][k:k + 1, :]
            prefetch(piece)
            piece += 1
        act = jax.nn.gelu(jnp.sum(acc, axis=1, keepdims=True))
        gcol = jnp.sum(jnp.where(eye, gate8[k:k + 1, :], 0.0), axis=1, keepdims=True)
        w = gcol * act
        new_tiles = list(tiles)
        for t in range(WORD_ROWS):
            v_lo, v_hi = unpack(buf[pl.ds(base + WORD_ROWS + t, N_PICKS, stride=PICK_PITCH), :])
            o_lo = jnp.sum(v_lo * w, axis=0, keepdims=True)
            o_hi = jnp.sum(v_hi * w, axis=0, keepdims=True)
            new_tiles[t] = jnp.where(sub == k, o_lo, tiles[t])
            new_tiles[WORD_ROWS + t] = jnp.where(sub == k, o_hi, tiles[WORD_ROWS + t])
            prefetch(piece)
            piece += 1
        return new_tiles

    def group(gi, last):
        r0 = pl.multiple_of(gi * N_SLOTS, N_SLOTS)
        x8 = [hf_s[pl.ds(r0, N_SLOTS), s * LANES:(s + 1) * LANES] for s in range(n_sub)]
        gate8 = gate_ref[pl.ds(r0, N_SLOTS), :]
        tiles = [jnp.zeros((SUBLANES, LANES), F32) for _ in range(n_sub)]
        for k in range(N_SLOTS):
            ahead = k + PREFETCH_DIST
            if last and ahead >= N_SLOTS:
                prefetch = lambda piece: None
            else:
                def prefetch(piece, ahead=ahead):
                    issue(r0 + ahead, ahead % N_SLOTS,
                          piece * picks_per_piece, (piece + 1) * picks_per_piece)
            tiles = token(k, x8, gate8, tiles, prefetch)
        res = [h_ref[pl.ds(r0, N_SLOTS), s * LANES:(s + 1) * LANES] + tiles[s] for s in range(n_sub)]
        if final_norm:
            ss = res[0] * res[0]
            for s in range(1, n_sub):
                ss = ss + res[s] * res[s]
            scale = lax.rsqrt(jnp.sum(ss, axis=1, keepdims=True) * (1.0 / D_MODEL) + EPS)
            res = [res[s] * scale * gfin_ref[:, s * LANES:(s + 1) * LANES] for s in range(n_sub)]
        for s in range(n_sub):
            out_ref[pl.ds(r0, N_SLOTS), s * LANES:(s + 1) * LANES] = res[s]

    for j in range(PREFETCH_DIST):
        issue(j, j, 0, N_PICKS)

    def body(gi, carry):
        group(gi, False)
        return carry

    lax.fori_loop(0, n_groups - 1, body, 0)
    group(n_groups - 1, True)


def _gather(idx, gate, h, g, gfin, tab, tt, final_norm):
    t = h.shape[0]
    row = lambda i: (i, 0)
    const2 = lambda i: (0, 0)
    return pl.pallas_call(
        functools.partial(_gather_kernel, tt=tt, final_norm=final_norm),
        grid=(t // tt,),
        in_specs=[
            pl.BlockSpec((tt, N_PICKS), row, memory_space=pltpu.SMEM),
            pl.BlockSpec((tt, N_PICKS), row),
            pl.BlockSpec((tt, D_MODEL), row),
            pl.BlockSpec((1, D_MODEL), const2),
            pl.BlockSpec((1, D_MODEL), const2),
            pl.BlockSpec(memory_space=pl.ANY),
        ],
        out_specs=pl.BlockSpec((tt, D_MODEL), row),
        out_shape=jax.ShapeDtypeStruct((t, D_MODEL), F32),
        scratch_shapes=[
            pltpu.VMEM((tt, D_MODEL), F32),
            pltpu.VMEM((N_SLOTS * SLOT_ROWS, LANES), jnp.int32),
            pltpu.SemaphoreType.DMA((N_SLOTS,)),
        ],
        compiler_params=_cparams(1),
        name="peer_experts",
    )(idx, gate, h, g, gfin, tab)


def _proj_kernel(h_ref, gkv_ref, gq_ref, wkv_ref, wq_ref, kv_ref, q_ref):
    h = h_ref[...]
    n = h * lax.rsqrt(jnp.mean(h * h, axis=-1, keepdims=True) + EPS)
    kv_ref[...] = jnp.dot((n * gkv_ref[...]).astype(BF16), wkv_ref[...], preferred_element_type=F32)
    q_ref[...] = jnp.dot((n * gq_ref[...]).astype(BF16), wq_ref[...],
                         preferred_element_type=F32).astype(BF16)


def _proj(h, gkv, gq, wkv, wq, tm):
    t = h.shape[0]
    row = lambda i: (i, 0)
    const2 = lambda i: (0, 0)
    return pl.pallas_call(
        _proj_kernel,
        grid=(t // tm,),
        in_specs=[
            pl.BlockSpec((tm, D_MODEL), row),
            pl.BlockSpec((1, D_MODEL), const2),
            pl.BlockSpec((1, D_MODEL), const2),
            pl.BlockSpec((D_MODEL, KV_DIM), const2),
            pl.BlockSpec((D_MODEL, D_MODEL), const2),
        ],
        out_specs=[pl.BlockSpec((tm, KV_DIM), row), pl.BlockSpec((tm, D_MODEL), row)],
        out_shape=[jax.ShapeDtypeStruct((t, KV_DIM), F32),
                   jax.ShapeDtypeStruct((t, D_MODEL), BF16)],
        compiler_params=_cparams(1),
        name="kv_q_proj",
    )(h, gkv, gq, wkv, wq)


def _alibi_slopes():
    return [2.0 ** (-8.0 * (h + 1) / N_HEADS) for h in range(N_HEADS)]


ATTN_KEYS = WINDOW + CHUNK
ATTN_PAIR = 2 * CHUNK
ATTN_ROW_BLOCK = 32


def _padded_heads(tile, g):
    low = lax.broadcasted_iota(jnp.int32, tile.shape, 1) < HEAD_DIM
    rolled = pltpu.roll(tile, HEAD_DIM, axis=1)
    if g % 2 == 0:
        pads = [jnp.where(low, tile, 0.0), jnp.where(low, 0.0, rolled)]
    else:
        pads = [jnp.where(low, rolled, 0.0), jnp.where(low, 0.0, tile)]
    return [p.astype(BF16) for p in pads]


def _attn_chunks(q_ref, windows, first_valids, sink_ref, s_scr, p_scr):
    half_kv = KV_DIM // 2
    slopes = _alibi_slopes()
    n_chunks = len(windows)

    def chain(c, g, half):
        return (c * N_KV_HEADS + g) * 2 + half

    for c in range(n_chunks):
        win = windows[c]()
        r0 = c * CHUNK
        for g in range(N_KV_HEADS):
            t = g // 2
            k_pad = _padded_heads(win[:, t * LANES:(t + 1) * LANES], g)
            qstack = jnp.concatenate(
                [q_ref[r0:r0 + CHUNK, (2 * g) * LANES:(2 * g + 1) * LANES],
                 q_ref[r0:r0 + CHUNK, (2 * g + 1) * LANES:(2 * g + 2) * LANES]], axis=0)
            for half in range(2):
                s_scr[chain(c, g, half)] = lax.dot_general(
                    qstack, k_pad[half], (((1,), (1,)), ((), ())), preferred_element_type=F32)

    rb = ATTN_ROW_BLOCK
    qi = lax.broadcasted_iota(jnp.int32, (rb, ATTN_KEYS), 0)
    kj = lax.broadcasted_iota(jnp.int32, (rb, ATTN_KEYS), 1)
    dists = [jnp.abs(WINDOW + qi + off - kj).astype(F32) for off in range(0, CHUNK, rb)]
    for c in range(n_chunks):
        valid = kj >= first_valids[c]
        for g in range(N_KV_HEADS):
            for half in range(2):
                ch = chain(c, g, half)
                for b in range(ATTN_PAIR // rb):
                    row0 = b * rb
                    head = GQA_GROUP * g + half + (2 if row0 >= CHUNK else 0)
                    s = s_scr[ch, row0:row0 + rb, :]
                    s = s * (HEAD_DIM ** -0.5) - slopes[head] * dists[(row0 % CHUNK) // rb]
                    s = jnp.where(valid, s, NEG)
                    sink = sink_ref[0, head]
                    m = jnp.maximum(jnp.max(s, axis=1, keepdims=True), sink)
                    e = jnp.exp(s - m)
                    denom = jnp.sum(e, axis=1, keepdims=True) + jnp.exp(sink - m)
                    p_scr[ch, row0:row0 + rb, :] = (e / denom).astype(BF16)

    outs = []
    for c in range(n_chunks):
        win = windows[c]()
        o_tiles = [None] * (N_HEADS // 2)
        for g in range(N_KV_HEADS):
            t = g // 2
            v_pad = _padded_heads(win[:, half_kv + t * LANES:half_kv + (t + 1) * LANES], g)
            o_stack = (jnp.dot(p_scr[chain(c, g, 0)], v_pad[0], preferred_element_type=F32)
                       + jnp.dot(p_scr[chain(c, g, 1)], v_pad[1], preferred_element_type=F32))
            o_tiles[2 * g] = o_stack[:CHUNK]
            o_tiles[2 * g + 1] = o_stack[CHUNK:]
        outs.append(jnp.concatenate(o_tiles, axis=1))
    return outs[0] if n_chunks == 1 else jnp.concatenate(outs, axis=0)


def _attn_scratch(n_chunks):
    n_chains = n_chunks * N_KV_HEADS * 2
    return [pltpu.VMEM((n_chains, ATTN_PAIR, ATTN_KEYS), F32),
            pltpu.VMEM((n_chains, ATTN_PAIR, ATTN_KEYS), BF16)]


def _attn_prompt_kernel(sink_ref, q_ref, kvp_ref, kvc_ref, h_ref, wo_ref, out_ref,
                        kv_s, s_scr, p_scr, *, qt):
    t = pl.program_id(1)
    kv_s[0:WINDOW, :] = kvp_ref[...]
    kv_s[WINDOW:WINDOW + qt, :] = kvc_ref[...]
    n_chunks = qt // CHUNK
    windows = [functools.partial(lambda r0: kv_s[r0:r0 + ATTN_KEYS, :], c * CHUNK)
               for c in range(n_chunks)]
    first_valids = [WINDOW - CHUNK * (t * n_chunks + c) for c in range(n_chunks)]
    o = _attn_chunks(q_ref, windows, first_valids, sink_ref, s_scr, p_scr).astype(BF16)
    out_ref[...] = h_ref[...] + jnp.dot(o, wo_ref[...], preferred_element_type=F32)


def _attn_prompt(sinks, q, kv, h, wo, batch, seq, qt):
    t = h.shape[0]
    tiles = seq // qt
    cur = lambda b, i: (b * tiles + i, 0)
    prev = lambda b, i: (jnp.maximum((b * tiles + i) * (qt // WINDOW) - 1, 0), 0)
    const2 = lambda b, i: (0, 0)
    return pl.pallas_call(
        functools.partial(_attn_prompt_kernel, qt=qt),
        grid=(batch, tiles),
        in_specs=[
            pl.BlockSpec((1, N_HEADS), const2, memory_space=pltpu.SMEM),
            pl.BlockSpec((qt, D_MODEL), cur),
            pl.BlockSpec((WINDOW, KV_DIM), prev),
            pl.BlockSpec((qt, KV_DIM), cur),
            pl.BlockSpec((qt, D_MODEL), cur),
            pl.BlockSpec((D_MODEL, D_MODEL), const2),
        ],
        out_specs=pl.BlockSpec((qt, D_MODEL), cur),
        out_shape=jax.ShapeDtypeStruct((t, D_MODEL), F32),
        scratch_shapes=[pltpu.VMEM((WINDOW + qt, KV_DIM), F32)] + _attn_scratch(qt // CHUNK),
        compiler_params=_cparams(2),
        name="attn_prompt",
    )(sinks, q, kv, kv, h, wo)


def _attn_sample_kernel(sink_ref, q_ref, ck_ref, cv_ref, kvn_ref, h_ref, wo_ref, out_ref,
                        s_scr, p_scr, *, nb):
    half_kv = KV_DIM // 2

    def window(b):
        kvn = kvn_ref[b * CHUNK:(b + 1) * CHUNK, :]
        k = jnp.concatenate([ck_ref[b * WINDOW:(b + 1) * WINDOW, :], kvn[:, :half_kv]], axis=0)
        v = jnp.concatenate([cv_ref[b * WINDOW:(b + 1) * WINDOW, :], kvn[:, half_kv:]], axis=0)
        return jnp.concatenate([k, v], axis=1)

    windows = [functools.partial(window, b) for b in range(nb)]
    o = _attn_chunks(q_ref, windows, [0] * nb, sink_ref, s_scr, p_scr).astype(BF16)
    out_ref[...] = h_ref[...] + jnp.dot(o, wo_ref[...], preferred_element_type=F32)


def _attn_sample(sinks, q, ck, cv, kv, h, wo, batch, nb):
    t = h.shape[0]
    row = lambda b: (b, 0)
    const2 = lambda b: (0, 0)
    return pl.pallas_call(
        functools.partial(_attn_sample_kernel, nb=nb),
        grid=(batch // nb,),
        in_specs=[
            pl.BlockSpec((1, N_HEADS), const2, memory_space=pltpu.SMEM),
            pl.BlockSpec((nb * CHUNK, D_MODEL), row),
            pl.BlockSpec((nb * WINDOW, KV_DIM // 2), row),
            pl.BlockSpec((nb * WINDOW, KV_DIM // 2), row),
            pl.BlockSpec((nb * CHUNK, KV_DIM), row),
            pl.BlockSpec((nb * CHUNK, D_MODEL), row),
            pl.BlockSpec((D_MODEL, D_MODEL), const2),
        ],
        scratch_shapes=_attn_scratch(nb),
        out_specs=pl.BlockSpec((nb * CHUNK, D_MODEL), row),
        out_shape=jax.ShapeDtypeStruct((t, D_MODEL), F32),
        compiler_params=_cparams(1),
        name="attn_sample",
    )(sinks, q, ck, cv, kv, h, wo)


def kernel(x_prompt, x_sample, cache_k_win, cache_v_win, norm_mix, norm_ffn, gmlp_w_in, gmlp_norm_v, gmlp_w_s, gmlp_b_s, gmlp_w_out, norm_kv, w_kv, attn_w_q, attn_sinks, attn_w_o, peer_w_q, peer_sub_keys, peer_u, peer_v, norm_final):
    batch, seq, _ = x_prompt.shape
    dec_batch, dec_seq, _ = x_sample.shape
    w_cache = cache_k_win.shape[1]
    assert dec_seq == CHUNK and w_cache == WINDOW and seq % MLP_CHUNK == 0

    row2 = lambda a: a.reshape(1, -1)
    win = gmlp_w_in[0].astype(BF16)
    wout = gmlp_w_out[0].astype(BF16)
    wkv = w_kv.astype(BF16)
    wq_attn = attn_w_q[0].astype(BF16)
    wo = attn_w_o[0].astype(BF16)
    wq_peer = peer_w_q.astype(BF16)
    sub_keys = peer_sub_keys.reshape(2, 2 * PEER_HEADS, N_KEYS, D_HALF).astype(BF16)
    tables = [_pack_experts(peer_u, peer_v, l) for l in range(2)]

    pos = jnp.arange(MLP_CHUNK)
    blk_i = pos[:, None] // CHUNK
    blk_j = pos[None, :] // CHUNK
    ws = gmlp_w_s[0]
    bs = gmlp_b_s[0]
    wmix_p = jnp.where((blk_j <= blk_i)[None], ws, 0.0).astype(BF16)
    ws64 = ws[:, :CHUNK, :CHUNK]
    wmix_s = jnp.where((blk_j == blk_i)[None], jnp.tile(ws64, (1, 2, 2)), 0.0).astype(BF16)
    bias_p = jnp.repeat(bs.T, LANES, axis=1)
    bias_s = jnp.repeat(jnp.tile(bs[:, :CHUNK], (1, 2)).T, LANES, axis=1)

    def trunk(x, prompt):
        t = x.shape[0]
        tm = 512
        tm_route = 1024
        tt = 256
        h, v_rows = _gmlp(x, row2(norm_mix[0]), win, row2(gmlp_norm_v[0]),
                          wmix_p if prompt else wmix_s, bias_p if prompt else bias_s, wout, tm)
        idx, gate = _route(h, row2(norm_ffn[0]), wq_peer[0], sub_keys[0], tm_route)
        h = _gather(idx, gate, h, row2(norm_ffn[0]), row2(norm_final), tables[0], tt, False)
        kv, q = _proj(h, row2(norm_kv), row2(norm_mix[1]), wkv, wq_attn, tm)
        if prompt:
            h = _attn_prompt(attn_sinks, q, kv, h, wo, batch, seq, 256)
        else:
            ck = cache_k_win.reshape(dec_batch * w_cache, KV_DIM // 2)
            cv = cache_v_win.reshape(dec_batch * w_cache, KV_DIM // 2)
            h = _attn_sample(attn_sinks, q, ck, cv, kv, h, wo, dec_batch, 4)
        idx, gate = _route(h, row2(norm_ffn[1]), wq_peer[1], sub_keys[1], tm_route)
        y = _gather(idx, gate, h, row2(norm_ffn[1]), row2(norm_final), tables[1], tt, True)
        return y, kv, v_rows

    y_p, kv_p, _ = trunk(x_prompt.reshape(batch * seq, D_MODEL), True)
    y_s, kv_s, v_s = trunk(x_sample.reshape(dec_batch * dec_seq, D_MODEL), False)

    half = KV_DIM // 2
    kv_p = kv_p.reshape(batch, seq, KV_DIM)[:, seq - WINDOW:]
    prompt_k_win = kv_p[..., :half].reshape(batch, WINDOW, N_KV_HEADS, HEAD_DIM)
    prompt_v_win = kv_p[..., half:].reshape(batch, WINDOW, N_KV_HEADS, HEAD_DIM)
    kv_s = kv_s.reshape(dec_batch, dec_seq, KV_DIM)
    k_s = kv_s[..., :half].reshape(dec_batch, dec_seq, N_KV_HEADS, HEAD_DIM)
    v_s_new = kv_s[..., half:].reshape(dec_batch, dec_seq, N_KV_HEADS, HEAD_DIM)
    sample_k_win = jnp.concatenate([cache_k_win, k_s], axis=1)[:, -w_cache:]
    sample_v_win = jnp.concatenate([cache_v_win, v_s_new], axis=1)[:, -w_cache:]
    sample_gmlp_v = v_s.reshape(1, dec_batch, dec_seq, D_GATE)

    return (y_p.reshape(batch, seq, D_MODEL), y_s.reshape(dec_batch, dec_seq, D_MODEL),
            prompt_k_win, prompt_v_win, sample_k_win, sample_v_win, sample_gmlp_v)
```

```python
import functools
import math

import jax
import jax.numpy as jnp
from jax import lax
from jax.experimental import pallas as pl
from jax.experimental.pallas import tpu as pltpu

F32 = jnp.float32
BF16 = jnp.bfloat16

D_MODEL = 1024
CHUNK = 64
MLP_CHUNK = 128
GMLP_GROUPS = 8
D_GATE = D_MODEL
N_HEADS = 16
N_KV_HEADS = 4
HEAD_DIM = 64
GQA_GROUP = N_HEADS // N_KV_HEADS
WINDOW = 128
PEER_HEADS = 8
N_KEYS = 128
N_EXPERTS = N_KEYS * N_KEYS
PEER_TOPK = 16
D_HALF = 128
N_PICKS = PEER_HEADS * PEER_TOPK
EPS = 1e-6
NEG = -1e30
KV_DIM = 2 * N_KV_HEADS * HEAD_DIM

LANES = 128
SUBLANES = 8
WORD_ROWS = D_MODEL // 2 // LANES
ROWS_PER_EXPERT = 2 * WORD_ROWS
HIGH_HALF = -65536
PICK_PITCH = 12
SLOT_ROWS = N_PICKS * PICK_PITCH
N_SLOTS = SUBLANES
PREFETCH_DIST = 6

VMEM_LIMIT = 48 * 1024 * 1024


def _rms(x, g):
    return x * lax.rsqrt(jnp.mean(x * x, axis=-1, keepdims=True) + EPS) * g


def _cparams(n_axes):
    return pltpu.CompilerParams(
        dimension_semantics=("arbitrary",) * n_axes, vmem_limit_bytes=VMEM_LIMIT)


def _gmlp_kernel(x_ref, gmix_ref, win_ref, gv_ref, wmix_ref, bias_ref, wout_ref,
                 h_ref, v_ref, *, tm):
    x = x_ref[...]
    hn = _rms(x, gmix_ref[...])
    hmid = jax.nn.gelu(jnp.dot(hn.astype(BF16), win_ref[...], preferred_element_type=F32))
    u = hmid[:, :D_GATE]
    v = _rms(hmid[:, D_GATE:], gv_ref[...])
    v_ref[...] = v
    vb = v.astype(BF16)
    bias = bias_ref[...]
    outs = []
    for c in range(tm // MLP_CHUNK):
        r0 = c * MLP_CHUNK
        cols = []
        for g in range(GMLP_GROUPS):
            c0 = g * LANES
            cols.append(jnp.dot(wmix_ref[g], vb[r0:r0 + MLP_CHUNK, c0:c0 + LANES],
                                preferred_element_type=F32))
        mixed = jnp.concatenate(cols, axis=1) + bias
        outs.append(u[r0:r0 + MLP_CHUNK, :] * mixed)
    out = outs[0] if len(outs) == 1 else jnp.concatenate(outs, axis=0)
    y = jnp.dot(out.astype(BF16), wout_ref[...], preferred_element_type=F32)
    h_ref[...] = x + y


def _gmlp(x, gmix, win, gv, wmix, bias, wout, tm):
    t = x.shape[0]
    row = lambda i: (i, 0)
    const2 = lambda i: (0, 0)
    return pl.pallas_call(
        functools.partial(_gmlp_kernel, tm=tm),
        grid=(t // tm,),
        in_specs=[
            pl.BlockSpec((tm, D_MODEL), row),
            pl.BlockSpec((1, D_MODEL), const2),
            pl.BlockSpec((D_MODEL, 2 * D_GATE), const2),
            pl.BlockSpec((1, D_GATE), const2),
            pl.BlockSpec((GMLP_GROUPS, MLP_CHUNK, MLP_CHUNK), lambda i: (0, 0, 0)),
            pl.BlockSpec((MLP_CHUNK, D_GATE), const2),
            pl.BlockSpec((D_GATE, D_MODEL), const2),
        ],
        out_specs=[pl.BlockSpec((tm, D_MODEL), row), pl.BlockSpec((tm, D_GATE), row)],
        out_shape=[jax.ShapeDtypeStruct((t, D_MODEL), F32),
                   jax.ShapeDtypeStruct((t, D_GATE), F32)],
        compiler_params=_cparams(1),
        name="gmlp_mixer",
    )(x, gmix, win, gv, wmix, bias, wout)


def _merge_exchange_network(lo, hi, r=None):
    def merge(lo, hi, r):
        step = r * 2
        if step < hi - lo:
            yield from merge(lo, hi, step)
            yield from merge(lo + r, hi, step)
            yield from [(i, i + r) for i in range(lo + r, hi - r, step)]
        else:
            yield (lo, lo + r)

    if hi - lo >= 1:
        mid = lo + (hi - lo) // 2
        yield from _merge_exchange_network(lo, mid)
        yield from _merge_exchange_network(mid + 1, hi)
        yield from merge(lo, hi, 1)


SORT16_NETWORK = list(_merge_exchange_network(0, N_KEYS // SUBLANES - 1))

LEVEL_COUNT = [PEER_TOPK // (b + 1) for b in range(PEER_TOPK)]


def _route_kernel(h_ref, g_ref, wq_ref, sk_ref, idx_ref, gate_ref,
                  sc_s, ts_s, ti_s, best_s, sel_s, e_s, gt_s, *, tm):
    hf = _rms(h_ref[...], g_ref[...])
    q = jnp.dot(hf.astype(BF16), wq_ref[...], preferred_element_type=F32).astype(BF16)
    n_hp = 2 * PEER_HEADS
    for hp in range(n_hp):
        sc_s[hp] = lax.dot_general(sk_ref[hp], q[:, hp * D_HALF:(hp + 1) * D_HALF],
                                   (((1,), (1,)), ((), ())), preferred_element_type=F32)

    n_rows = N_KEYS // SUBLANES
    sub_key = lax.broadcasted_iota(jnp.int32, (SUBLANES, LANES), 0).astype(F32)

    def stage1(hp, carry):
        for lg in range(tm // LANES):
            l0 = lg * LANES
            val = [sc_s[hp, v * SUBLANES:(v + 1) * SUBLANES, l0:l0 + LANES] for v in range(n_rows)]
            key = [sub_key + float(v * SUBLANES) for v in range(n_rows)]
            for a, b in SORT16_NETWORK:
                swap = (val[b] > val[a]) | ((val[b] == val[a]) & (key[b] < key[a]))
                val[a], val[b] = jnp.where(swap, val[b], val[a]), jnp.where(swap, val[a], val[b])
                key[a], key[b] = jnp.where(swap, key[b], key[a]), jnp.where(swap, key[a], key[b])
            for k in range(PEER_TOPK):
                m = jnp.max(val[0], axis=0, keepdims=True)
                i = jnp.min(jnp.where(val[0] == m, key[0], float(N_KEYS)), axis=0, keepdims=True)
                ts_s[hp, k:k + 1, l0:l0 + LANES] = m
                ti_s[hp, k:k + 1, l0:l0 + LANES] = i
                if k + 1 < PEER_TOPK:
                    taken = key[0] == i
                    for p in range(PEER_TOPK - 1 - k):
                        val[p] = jnp.where(taken, val[p + 1], val[p])
                        key[p] = jnp.where(taken, key[p + 1], key[p])
        return carry

    lax.fori_loop(0, n_hp, stage1, 0)

    n_cand = float(PEER_TOPK * PEER_TOPK)
    level_flat = [sub_key * float(PEER_TOPK) + float(b) for b in range(PEER_TOPK)]
    high_flat = (sub_key + float(SUBLANES)) * float(PEER_TOPK)

    def stage2(hd, carry):
        for lg in range(tm // LANES):
            l0 = lg * LANES
            s0 = ts_s[2 * hd, :, l0:l0 + LANES]
            s1 = ts_s[2 * hd + 1, :, l0:l0 + LANES]
            i0 = ti_s[2 * hd, :, l0:l0 + LANES]
            i1 = ti_s[2 * hd + 1, :, l0:l0 + LANES]
            val = []
            for b in range(PEER_TOPK):
                v = s0[0:SUBLANES, :] + s1[b:b + 1, :]
                if LEVEL_COUNT[b] < SUBLANES:
                    v = jnp.where(sub_key < float(LEVEL_COUNT[b]), v, -jnp.inf)
                val.append(v)
            flat = list(level_flat)
            high = s0[SUBLANES:, :] + s1[0:1, :]
            for k in range(PEER_TOPK):
                m = jnp.max(jnp.maximum(val[0], high), axis=0, keepdims=True)
                c = jnp.min(jnp.minimum(jnp.where(val[0] == m, flat[0], n_cand),
                                        jnp.where(high == m, high_flat, n_cand)),
                            axis=0, keepdims=True)
                best_s[k:k + 1, :] = m
                sel_s[k:k + 1, :] = c
                if k + 1 < PEER_TOPK:
                    taken = flat[0] == c
                    high = jnp.where(high_flat == c, -jnp.inf, high)
                    for p in range(PEER_TOPK - 1 - k):
                        val[p] = jnp.where(taken, val[p + 1], val[p])
                        flat[p] = jnp.where(taken, flat[p + 1], flat[p])
            best = best_s[...]
            sel = sel_s[...]
            sel_a = jnp.floor(sel * (1.0 / PEER_TOPK))
            sel_b = sel - sel_a * PEER_TOPK
            e0 = jnp.zeros_like(sel)
            e1 = jnp.zeros_like(sel)
            for a in range(PEER_TOPK):
                e0 = jnp.where(sel_a == float(a), i0[a:a + 1, :], e0)
                e1 = jnp.where(sel_b == float(a), i1[a:a + 1, :], e1)
            ex = jnp.exp(best - best[0:1, :])
            gate = ex / jnp.sum(ex, axis=0, keepdims=True)
            r0 = pl.multiple_of(hd * PEER_TOPK, PEER_TOPK)
            e_s[pl.ds(r0, PEER_TOPK), l0:l0 + LANES] = e0 * float(N_KEYS) + e1
            gt_s[pl.ds(r0, PEER_TOPK), l0:l0 + LANES] = gate
        return carry

    lax.fori_loop(0, PEER_HEADS, stage2, 0)

    ids = jnp.clip(e_s[...], 0.0, float(N_EXPERTS - 1))
    idx_ref[...] = jnp.transpose(ids).astype(jnp.int32)
    gate_ref[...] = jnp.transpose(gt_s[...])


def _route(h, g, wq, sk, tm):
    t = h.shape[0]
    row = lambda i: (i, 0)
    const2 = lambda i: (0, 0)
    n_hp = 2 * PEER_HEADS
    return pl.pallas_call(
        functools.partial(_route_kernel, tm=tm),
        grid=(t // tm,),
        in_specs=[
            pl.BlockSpec((tm, D_MODEL), row),
            pl.BlockSpec((1, D_MODEL), const2),
            pl.BlockSpec((D_MODEL, n_hp * D_HALF), const2),
            pl.BlockSpec((n_hp, N_KEYS, D_HALF), lambda i: (0, 0, 0)),
        ],
        out_specs=[pl.BlockSpec((tm, N_PICKS), row), pl.BlockSpec((tm, N_PICKS), row)],
        out_shape=[jax.ShapeDtypeStruct((t, N_PICKS), jnp.int32),
                   jax.ShapeDtypeStruct((t, N_PICKS), F32)],
        scratch_shapes=[
            pltpu.VMEM((n_hp, N_KEYS, tm), F32),
            pltpu.VMEM((n_hp, PEER_TOPK, tm), F32),
            pltpu.VMEM((n_hp, PEER_TOPK, tm), F32),
            pltpu.VMEM((PEER_TOPK, LANES), F32),
            pltpu.VMEM((PEER_TOPK, LANES), F32),
            pltpu.VMEM((N_PICKS, tm), F32),
            pltpu.VMEM((N_PICKS, tm), F32),
        ],
        compiler_params=_cparams(1),
        name="peer_route",
    )(h, g, wq, sk)


def _pack_kernel(u_ref, v_ref, o_ref):
    n = u_ref.shape[0]
    half = D_MODEL // 2
    for src, row0 in ((u_ref, 0), (v_ref, WORD_ROWS)):
        bits = pltpu.bitcast(src[...].astype(BF16).astype(F32), jnp.uint32)
        for t in range(WORD_ROWS):
            lo = bits[:, t * LANES:(t + 1) * LANES] >> 16
            hi = bits[:, half + t * LANES:half + (t + 1) * LANES] & jnp.uint32(0xFFFF0000)
            o_ref[pl.ds(row0 + t, n, stride=ROWS_PER_EXPERT), :] = pltpu.bitcast(lo | hi, jnp.int32)


def _pack_experts(u, v, layer):
    n_blk = 256
    row = lambda i: (i, 0)
    layer_row = lambda i: (layer, i, 0)
    out = pl.pallas_call(
        _pack_kernel,
        grid=(N_EXPERTS // n_blk,),
        in_specs=[pl.BlockSpec((None, n_blk, D_MODEL), layer_row),
                  pl.BlockSpec((None, n_blk, D_MODEL), layer_row)],
        out_specs=pl.BlockSpec((n_blk * ROWS_PER_EXPERT, LANES), row),
        out_shape=jax.ShapeDtypeStruct((N_EXPERTS * ROWS_PER_EXPERT, LANES), jnp.int32),
        compiler_params=_cparams(1),
        name="pack_experts",
    )(u, v)
    return out.reshape(N_EXPERTS, ROWS_PER_EXPERT, LANES)


def _gather_kernel(idx_ref, gate_ref, h_ref, g_ref, gfin_ref, tab_ref, out_ref,
                   hf_s, buf, sem, *, tt, final_norm):
    hf_s[...] = _rms(h_ref[...], g_ref[...])
    n_sub = D_MODEL // LANES
    n_groups = tt // N_SLOTS
    n_pieces = 2 * WORD_ROWS
    picks_per_piece = N_PICKS // n_pieces

    def unpack(words):
        return (pltpu.bitcast(words << 16, F32), pltpu.bitcast(words & HIGH_HALF, F32))

    def pick_copy(e, slot, r):
        dst = buf.at[pl.ds(slot * SLOT_ROWS + r * PICK_PITCH, ROWS_PER_EXPERT), :]
        return pltpu.make_async_copy(tab_ref.at[e], dst, sem.at[slot])

    def issue(j, slot, r_lo, r_hi):
        for r in range(r_lo, r_hi):
            pick_copy(idx_ref[j, r], slot, r).start(priority=r % 2)

    def wait(slot):
        for r in range(N_PICKS):
            pick_copy(0, slot, r).wait()

    eye = (lax.broadcasted_iota(jnp.int32, (N_PICKS, N_PICKS), 0)
           == lax.broadcasted_iota(jnp.int32, (N_PICKS, N_PICKS), 1))
    sub = lax.broadcasted_iota(jnp.int32, (SUBLANES, LANES), 0)

    def token(k, x8, gate8, tiles, prefetch):
        base = k * SLOT_ROWS
        wait(k)
        piece = 0
        acc = jnp.zeros((N_PICKS, LANES), F32)
        for t in range(WORD_ROWS):
            u_lo, u_hi = unpack(buf[pl.ds(base + t, N_PICKS, stride=PICK_PITCH), :])
            acc = acc + u_lo * x8[t][k:k + 1, :] + u_hi * x8[WORD_ROWS + t][k:k + 1, :]
            prefetch(piece)
            piece += 1
        act = jax.nn.gelu(jnp.sum(acc, axis=1, keepdims=True))
        gcol = jnp.sum(jnp.where(eye, gate8[k:k + 1, :], 0.0), axis=1, keepdims=True)
        w = gcol * act
        new_tiles = list(tiles)
        for t in range(WORD_ROWS):
            v_lo, v_hi = unpack(buf[pl.ds(base + WORD_ROWS + t, N_PICKS, stride=PICK_PITCH), :])
            o_lo = jnp.sum(v_lo * w, axis=0, keepdims=True)
            o_hi = jnp.sum(v_hi * w, axis=0, keepdims=True)
            new_tiles[t] = jnp.where(sub == k, o_lo, tiles[t])
            new_tiles[WORD_ROWS + t] = jnp.where(sub == k, o_hi, tiles[WORD_ROWS + t])
            prefetch(piece)
            piece += 1
        return new_tiles

    def group(gi, last):
        r0 = pl.multiple_of(gi * N_SLOTS, N_SLOTS)
        x8 = [hf_s[pl.ds(r0, N_SLOTS), s * LANES:(s + 1) * LANES] for s in range(n_sub)]
        gate8 = gate_ref[pl.ds(r0, N_SLOTS), :]
        tiles = [jnp.zeros((SUBLANES, LANES), F32) for _ in range(n_sub)]
        for k in range(N_SLOTS):
            ahead = k + PREFETCH_DIST
            if last and ahead >= N_SLOTS:
                prefetch = lambda piece: None
            else:
                def prefetch(piece, ahead=ahead):
                    issue(r0 + ahead, ahead % N_SLOTS,
                          piece * picks_per_piece, (piece + 1) * picks_per_piece)
            tiles = token(k, x8, gate8, tiles, prefetch)
        res = [h_ref[pl.ds(r0, N_SLOTS), s * LANES:(s + 1) * LANES] + tiles[s] for s in range(n_sub)]
        if final_norm:
            ss = res[0] * res[0]
            for s in range(1, n_sub):
                ss = ss + res[s] * res[s]
            scale = lax.rsqrt(jnp.sum(ss, axis=1, keepdims=True) * (1.0 / D_MODEL) + EPS)
            res = [res[s] * scale * gfin_ref[:, s * LANES:(s + 1) * LANES] for s in range(n_sub)]
        for s in range(n_sub):
            out_ref[pl.ds(r0, N_SLOTS), s * LANES:(s + 1) * LANES] = res[s]

    for j in range(PREFETCH_DIST):
        issue(j, j, 0, N_PICKS)

    def body(gi, carry):
        group(gi, False)
        return carry

    lax.fori_loop(0, n_groups - 1, body, 0)
    group(n_groups - 1, True)


def _gather(idx, gate, h, g, gfin, tab, tt, final_norm):
    t = h.shape[0]
    row = lambda i: (i, 0)
    const2 = lambda i: (0, 0)
    return pl.pallas_call(
        functools.partial(_gather_kernel, tt=tt, final_norm=final_norm),
        grid=(t // tt,),
        in_specs=[
            pl.BlockSpec((tt, N_PICKS), row, memory_space=pltpu.SMEM),
            pl.BlockSpec((tt, N_PICKS), row),
            pl.BlockSpec((tt, D_MODEL), row),
            pl.BlockSpec((1, D_MODEL), const2),
            pl.BlockSpec((1, D_MODEL), const2),
            pl.BlockSpec(memory_space=pl.ANY),
        ],
        out_specs=pl.BlockSpec((tt, D_MODEL), row),
        out_shape=jax.ShapeDtypeStruct((t, D_MODEL), F32),
        scratch_shapes=[
            pltpu.VMEM((tt, D_MODEL), F32),
            pltpu.VMEM((N_SLOTS * SLOT_ROWS, LANES), jnp.int32),
            pltpu.SemaphoreType.DMA((N_SLOTS,)),
        ],
        compiler_params=_cparams(1),
        name="peer_experts",
    )(idx, gate, h, g, gfin, tab)


def _proj_kernel(h_ref, gkv_ref, gq_ref, wkv_ref, wq_ref, kv_ref, q_ref):
    h = h_ref[...]
    n = h * lax.rsqrt(jnp.mean(h * h, axis=-1, keepdims=True) + EPS)
    kv_ref[...] = jnp.dot((n * gkv_ref[...]).astype(BF16), wkv_ref[...], preferred_element_type=F32)
    q_ref[...] = jnp.dot((n * gq_ref[...]).astype(BF16), wq_ref[...],
                         preferred_element_type=F32).astype(BF16)


def _proj(h, gkv, gq, wkv, wq, tm):
    t = h.shape[0]
    row = lambda i: (i, 0)
    const2 = lambda i: (0, 0)
    return pl.pallas_call(
        _proj_kernel,
        grid=(t // tm,),
        in_specs=[
            pl.BlockSpec((tm, D_MODEL), row),
            pl.BlockSpec((1, D_MODEL), const2),
            pl.BlockSpec((1, D_MODEL), const2),
            pl.BlockSpec((D_MODEL, KV_DIM), const2),
            pl.BlockSpec((D_MODEL, D_MODEL), const2),
        ],
        out_specs=[pl.BlockSpec((tm, KV_DIM), row), pl.BlockSpec((tm, D_MODEL), row)],
        out_shape=[jax.ShapeDtypeStruct((t, KV_DIM), F32),
                   jax.ShapeDtypeStruct((t, D_MODEL), BF16)],
        compiler_params=_cparams(1),
        name="kv_q_proj",
    )(h, gkv, gq, wkv, wq)


def _alibi_slopes():
    return [2.0 ** (-8.0 * (h + 1) / N_HEADS) for h in range(N_HEADS)]


ATTN_KEYS = WINDOW + CHUNK
ATTN_PAIR = 2 * CHUNK
ATTN_ROW_BLOCK = 32


def _padded_heads(tile, g):
    low = lax.broadcasted_iota(jnp.int32, tile.shape, 1) < HEAD_DIM
    rolled = pltpu.roll(tile, HEAD_DIM, axis=1)
    if g % 2 == 0:
        pads = [jnp.where(low, tile, 0.0), jnp.where(low, 0.0, rolled)]
    else:
        pads = [jnp.where(low, rolled, 0.0), jnp.where(low, 0.0, tile)]
    return [p.astype(BF16) for p in pads]


def _attn_chunks(q_ref, windows, first_valids, sink_ref, s_scr, p_scr):
    half_kv = KV_DIM // 2
    slopes = _alibi_slopes()
    n_chunks = len(windows)

    def chain(c, g, half):
        return (c * N_KV_HEADS + g) * 2 + half

    for c in range(n_chunks):
        win = windows[c]()
        r0 = c * CHUNK
        for g in range(N_KV_HEADS):
            t = g // 2
            k_pad = _padded_heads(win[:, t * LANES:(t + 1) * LANES], g)
            qstack = jnp.concatenate(
                [q_ref[r0:r0 + CHUNK, (2 * g) * LANES:(2 * g + 1) * LANES],
                 q_ref[r0:r0 + CHUNK, (2 * g + 1) * LANES:(2 * g + 2) * LANES]], axis=0)
            for half in range(2):
                s_scr[chain(c, g, half)] = lax.dot_general(
                    qstack, k_pad[half], (((1,), (1,)), ((), ())), preferred_element_type=F32)

    rb = ATTN_ROW_BLOCK
    qi = lax.broadcasted_iota(jnp.int32, (rb, ATTN_KEYS), 0)
    kj = lax.broadcasted_iota(jnp.int32, (rb, ATTN_KEYS), 1)
    dists = [jnp.abs(WINDOW + qi + off - kj).astype(F32) for off in range(0, CHUNK, rb)]
    for c in range(n_chunks):
        valid = kj >= first_valids[c]
        for g in range(N_KV_HEADS):
            for half in range(2):
                ch = chain(c, g, half)
                for b in range(ATTN_PAIR // rb):
                    row0 = b * rb
                    head = GQA_GROUP * g + half + (2 if row0 >= CHUNK else 0)
                    s = s_scr[ch, row0:row0 + rb, :]
                    s = s * (HEAD_DIM ** -0.5) - slopes[head] * dists[(row0 % CHUNK) // rb]
                    s = jnp.where(valid, s, NEG)
                    sink = sink_ref[0, head]
                    m = jnp.maximum(jnp.max(s, axis=1, keepdims=True), sink)
                    e = jnp.exp(s - m)
                    denom = jnp.sum(e, axis=1, keepdims=True) + jnp.exp(sink - m)
                    p_scr[ch, row0:row0 + rb, :] = (e / denom).astype(BF16)

    outs = []
    for c in range(n_chunks):
        win = windows[c]()
        o_tiles = [None] * (N_HEADS // 2)
        for g in range(N_KV_HEADS):
            t = g // 2
            v_pad = _padded_heads(win[:, half_kv + t * LANES:half_kv + (t + 1) * LANES], g)
            o_stack = (jnp.dot(p_scr[chain(c, g, 0)], v_pad[0], preferred_element_type=F32)
                       + jnp.dot(p_scr[chain(c, g, 1)], v_pad[1], preferred_element_type=F32))
            o_tiles[2 * g] = o_stack[:CHUNK]
            o_tiles[2 * g + 1] = o_stack[CHUNK:]
        outs.append(jnp.concatenate(o_tiles, axis=1))
    return outs[0] if n_chunks == 1 else jnp.concatenate(outs, axis=0)


def _attn_scratch(n_chunks):
    n_chains = n_chunks * N_KV_HEADS * 2
    return [pltpu.VMEM((n_chains, ATTN_PAIR, ATTN_KEYS), F32),
            pltpu.VMEM((n_chains, ATTN_PAIR, ATTN_KEYS), BF16)]


def _attn_prompt_kernel(sink_ref, q_ref, kvp_ref, kvc_ref, h_ref, wo_ref, out_ref,
                        kv_s, s_scr, p_scr, *, qt):
    t = pl.program_id(1)
    kv_s[0:WINDOW, :] = kvp_ref[...]
    kv_s[WINDOW:WINDOW + qt, :] = kvc_ref[...]
    n_chunks = qt // CHUNK
    windows = [functools.partial(lambda r0: kv_s[r0:r0 + ATTN_KEYS, :], c * CHUNK)
               for c in range(n_chunks)]
    first_valids = [WINDOW - CHUNK * (t * n_chunks + c) for c in range(n_chunks)]
    o = _attn_chunks(q_ref, windows, first_valids, sink_ref, s_scr, p_scr).astype(BF16)
    out_ref[...] = h_ref[...] + jnp.dot(o, wo_ref[...], preferred_element_type=F32)


def _attn_prompt(sinks, q, kv, h, wo, batch, seq, qt):
    t = h.shape[0]
    tiles = seq // qt
    cur = lambda b, i: (b * tiles + i, 0)
    prev = lambda b, i: (jnp.maximum((b * tiles + i) * (qt // WINDOW) - 1, 0), 0)
    const2 = lambda b, i: (0, 0)
    return pl.pallas_call(
        functools.partial(_attn_prompt_kernel, qt=qt),
        grid=(batch, tiles),
        in_specs=[
            pl.BlockSpec((1, N_HEADS), const2, memory_space=pltpu.SMEM),
            pl.BlockSpec((qt, D_MODEL), cur),
            pl.BlockSpec((WINDOW, KV_DIM), prev),
            pl.BlockSpec((qt, KV_DIM), cur),
            pl.BlockSpec((qt, D_MODEL), cur),
            pl.BlockSpec((D_MODEL, D_MODEL), const2),
        ],
        out_specs=pl.BlockSpec((qt, D_MODEL), cur),
        out_shape=jax.ShapeDtypeStruct((t, D_MODEL), F32),
        scratch_shapes=[pltpu.VMEM((WINDOW + qt, KV_DIM), F32)] + _attn_scratch(qt // CHUNK),
        compiler_params=_cparams(2),
        name="attn_prompt",
    )(sinks, q, kv, kv, h, wo)


def _attn_sample_kernel(sink_ref, q_ref, ck_ref, cv_ref, kvn_ref, h_ref, wo_ref, out_ref,
                        s_scr, p_scr, *, nb):
    half_kv = KV_DIM // 2

    def window(b):
        kvn = kvn_ref[b * CHUNK:(b + 1) * CHUNK, :]
        k = jnp.concatenate([ck_ref[b * WINDOW:(b + 1) * WINDOW, :], kvn[:, :half_kv]], axis=0)
        v = jnp.concatenate([cv_ref[b * WINDOW:(b + 1) * WINDOW, :], kvn[:, half_kv:]], axis=0)
        return jnp.concatenate([k, v], axis=1)

    windows = [functools.partial(window, b) for b in range(nb)]
    o = _attn_chunks(q_ref, windows, [0] * nb, sink_ref, s_scr, p_scr).astype(BF16)
    out_ref[...] = h_ref[...] + jnp.dot(o, wo_ref[...], preferred_element_type=F32)


def _attn_sample(sinks, q, ck, cv, kv, h, wo, batch, nb):
    t = h.shape[0]
    row = lambda b: (b, 0)
    const2 = lambda b: (0, 0)
    return pl.pallas_call(
        functools.partial(_attn_sample_kernel, nb=nb),
        grid=(batch // nb,),
        in_specs=[
            pl.BlockSpec((1, N_HEADS), const2, memory_space=pltpu.SMEM),
            pl.BlockSpec((nb * CHUNK, D_MODEL), row),
            pl.BlockSpec((nb * WINDOW, KV_DIM // 2), row),
            pl.BlockSpec((nb * WINDOW, KV_DIM // 2), row),
            pl.BlockSpec((nb * CHUNK, KV_DIM), row),
            pl.BlockSpec((nb * CHUNK, D_MODEL), row),
            pl.BlockSpec((D_MODEL, D_MODEL), const2),
        ],
        scratch_shapes=_attn_scratch(nb),
        out_specs=pl.BlockSpec((nb * CHUNK, D_MODEL), row),
        out_shape=jax.ShapeDtypeStruct((t, D_MODEL), F32),
        compiler_params=_cparams(1),
        name="attn_sample",
    )(sinks, q, ck, cv, kv, h, wo)


def kernel(x_prompt, x_sample, cache_k_win, cache_v_win, norm_mix, norm_ffn, gmlp_w_in, gmlp_norm_v, gmlp_w_s, gmlp_b_s, gmlp_w_out, norm_kv, w_kv, attn_w_q, attn_sinks, attn_w_o, peer_w_q, peer_sub_keys, peer_u, peer_v, norm_final):
    batch, seq, _ = x_prompt.shape
    dec_batch, dec_seq, _ = x_sample.shape
    w_cache = cache_k_win.shape[1]
    assert dec_seq == CHUNK and w_cache == WINDOW and seq % MLP_CHUNK == 0

    row2 = lambda a: a.reshape(1, -1)
    win = gmlp_w_in[0].astype(BF16)
    wout = gmlp_w_out[0].astype(BF16)
    wkv = w_kv.astype(BF16)
    wq_attn = attn_w_q[0].astype(BF16)
    wo = attn_w_o[0].astype(BF16)
    wq_peer = peer_w_q.astype(BF16)
    sub_keys = peer_sub_keys.reshape(2, 2 * PEER_HEADS, N_KEYS, D_HALF).astype(BF16)
    tables = [_pack_experts(peer_u, peer_v, l) for l in range(2)]

    pos = jnp.arange(MLP_CHUNK)
    blk_i = pos[:, None] // CHUNK
    blk_j = pos[None, :] // CHUNK
    ws = gmlp_w_s[0]
    bs = gmlp_b_s[0]
    wmix_p = jnp.where((blk_j <= blk_i)[None], ws, 0.0).astype(BF16)
    ws64 = ws[:, :CHUNK, :CHUNK]
    wmix_s = jnp.where((blk_j == blk_i)[None], jnp.tile(ws64, (1, 2, 2)), 0.0).astype(BF16)
    bias_p = jnp.repeat(bs.T, LANES, axis=1)
    bias_s = jnp.repeat(jnp.tile(bs[:, :CHUNK], (1, 2)).T, LANES, axis=1)

    def trunk(x, prompt):
        t = x.shape[0]
        tm = 512
        tm_route = 1024
        tt = 256
        h, v_rows = _gmlp(x, row2(norm_mix[0]), win, row2(gmlp_norm_v[0]),
                          wmix_p if prompt else wmix_s, bias_p if prompt else bias_s, wout, tm)
        idx, gate = _route(h, row2(norm_ffn[0]), wq_peer[0], sub_keys[0], tm_route)
        h = _gather(idx, gate, h, row2(norm_ffn[0]), row2(norm_final), tables[0], tt, False)
        kv, q = _proj(h, row2(norm_kv), row2(norm_mix[1]), wkv, wq_attn, tm)
        if prompt:
            h = _attn_prompt(attn_sinks, q, kv, h, wo, batch, seq, 256)
        else:
            ck = cache_k_win.reshape(dec_batch * w_cache, KV_DIM // 2)
            cv = cache_v_win.reshape(dec_batch * w_cache, KV_DIM // 2)
            h = _attn_sample(attn_sinks, q, ck, cv, kv, h, wo, dec_batch, 4)
        idx, gate = _route(h, row2(norm_ffn[1]), wq_peer[1], sub_keys[1], tm_route)
        y = _gather(idx, gate, h, row2(norm_ffn[1]), row2(norm_final), tables[1], tt, True)
        return y, kv, v_rows

    y_p, kv_p, _ = trunk(x_prompt.reshape(batch * seq, D_MODEL), True)
    y_s, kv_s, v_s = trunk(x_sample.reshape(dec_batch * dec_seq, D_MODEL), False)

    half = KV_DIM // 2
    kv_p = kv_p.reshape(batch, seq, KV_DIM)[:, seq - WINDOW:]
    prompt_k_win = kv_p[..., :half].reshape(batch, WINDOW, N_KV_HEADS, HEAD_DIM)
    prompt_v_win = kv_p[..., half:].reshape(batch, WINDOW, N_KV_HEADS, HEAD_DIM)
    kv_s = kv_s.reshape(dec_batch, dec_seq, KV_DIM)
    k_s = kv_s[..., :half].reshape(dec_batch, dec_seq, N_KV_HEADS, HEAD_DIM)
    v_s_new = kv_s[..., half:].reshape(dec_batch, dec_seq, N_KV_HEADS, HEAD_DIM)
    sample_k_win = jnp.concatenate([cache_k_win, k_s], axis=1)[:, -w_cache:]
    sample_v_win = jnp.concatenate([cache_v_win, v_s_new], axis=1)[:, -w_cache:]
    sample_gmlp_v = v_s.reshape(1, dec_batch, dec_seq, D_GATE)

    return (y_p.reshape(batch, seq, D_MODEL), y_s.reshape(dec_batch, dec_seq, D_MODEL),
            prompt_k_win, prompt_v_win, sample_k_win, sample_v_win, sample_gmlp_v)
```

```python
import functools
import math

import jax
import jax.numpy as jnp
from jax import lax
from jax.experimental import pallas as pl
from jax.experimental.pallas import tpu as pltpu

F32 = jnp.float32
BF16 = jnp.bfloat16

D_MODEL = 1024
CHUNK = 64
MLP_CHUNK = 128
GMLP_GROUPS = 8
D_GATE = D_MODEL
N_HEADS = 16
N_KV_HEADS = 4
HEAD_DIM = 64
GQA_GROUP = N_HEADS // N_KV_HEADS
WINDOW = 128
PEER_HEADS = 8
N_KEYS = 128
N_EXPERTS = N_KEYS * N_KEYS
PEER_TOPK = 16
D_HALF = 128
N_PICKS = PEER_HEADS * PEER_TOPK
EPS = 1e-6
NEG = -1e30
KV_DIM = 2 * N_KV_HEADS * HEAD_DIM

LANES = 128
SUBLANES = 8
WORD_ROWS = D_MODEL // 2 // LANES
ROWS_PER_EXPERT = 2 * WORD_ROWS
HIGH_HALF = -65536
PICK_PITCH = 12
SLOT_ROWS = N_PICKS * PICK_PITCH
N_SLOTS = SUBLANES
PREFETCH_DIST = 7

VMEM_LIMIT = 48 * 1024 * 1024


def _rms(x, g):
    return x * lax.rsqrt(jnp.mean(x * x, axis=-1, keepdims=True) + EPS) * g


def _cparams(n_axes):
    return pltpu.CompilerParams(
        dimension_semantics=("arbitrary",) * n_axes, vmem_limit_bytes=VMEM_LIMIT)


def _gmlp_kernel(x_ref, gmix_ref, win_ref, gv_ref, wmix_ref, bias_ref, wout_ref,
                 h_ref, v_ref, *, tm):
    x = x_ref[...]
    hn = _rms(x, gmix_ref[...])
    hmid = jax.nn.gelu(jnp.dot(hn.astype(BF16), win_ref[...], preferred_element_type=F32))
    u = hmid[:, :D_GATE]
    v = _rms(hmid[:, D_GATE:], gv_ref[...])
    v_ref[...] = v
    vb = v.astype(BF16)
    bias = bias_ref[...]
    outs = []
    for c in range(tm // MLP_CHUNK):
        r0 = c * MLP_CHUNK
        cols = []
        for g in range(GMLP_GROUPS):
            c0 = g * LANES
            cols.append(jnp.dot(wmix_ref[g], vb[r0:r0 + MLP_CHUNK, c0:c0 + LANES],
                                preferred_element_type=F32))
        mixed = jnp.concatenate(cols, axis=1) + bias
        outs.append(u[r0:r0 + MLP_CHUNK, :] * mixed)
    out = outs[0] if len(outs) == 1 else jnp.concatenate(outs, axis=0)
    y = jnp.dot(out.astype(BF16), wout_ref[...], preferred_element_type=F32)
    h_ref[...] = x + y


def _gmlp(x, gmix, win, gv, wmix, bias, wout, tm):
    t = x.shape[0]
    row = lambda i: (i, 0)
    const2 = lambda i: (0, 0)
    return pl.pallas_call(
        functools.partial(_gmlp_kernel, tm=tm),
        grid=(t // tm,),
        in_specs=[
            pl.BlockSpec((tm, D_MODEL), row),
            pl.BlockSpec((1, D_MODEL), const2),
            pl.BlockSpec((D_MODEL, 2 * D_GATE), const2),
            pl.BlockSpec((1, D_GATE), const2),
            pl.BlockSpec((GMLP_GROUPS, MLP_CHUNK, MLP_CHUNK), lambda i: (0, 0, 0)),
            pl.BlockSpec((MLP_CHUNK, D_GATE), const2),
            pl.BlockSpec((D_GATE, D_MODEL), const2),
        ],
        out_specs=[pl.BlockSpec((tm, D_MODEL), row), pl.BlockSpec((tm, D_GATE), row)],
        out_shape=[jax.ShapeDtypeStruct((t, D_MODEL), F32),
                   jax.ShapeDtypeStruct((t, D_GATE), F32)],
        compiler_params=_cparams(1),
        name="gmlp_mixer",
    )(x, gmix, win, gv, wmix, bias, wout)


def _merge_exchange_network(lo, hi, r=None):
    def merge(lo, hi, r):
        step = r * 2
        if step < hi - lo:
            yield from merge(lo, hi, step)
            yield from merge(lo + r, hi, step)
            yield from [(i, i + r) for i in range(lo + r, hi - r, step)]
        else:
            yield (lo, lo + r)

    if hi - lo >= 1:
        mid = lo + (hi - lo) // 2
        yield from _merge_exchange_network(lo, mid)
        yield from _merge_exchange_network(mid + 1, hi)
        yield from merge(lo, hi, 1)


SORT16_NETWORK = list(_merge_exchange_network(0, N_KEYS // SUBLANES - 1))

LEVEL_COUNT = [PEER_TOPK // (b + 1) for b in range(PEER_TOPK)]


def _route_kernel(h_ref, g_ref, wq_ref, sk_ref, idx_ref, gate_ref,
                  sc_s, ts_s, ti_s, best_s, sel_s, e_s, gt_s, *, tm):
    hf = _rms(h_ref[...], g_ref[...])
    q = jnp.dot(hf.astype(BF16), wq_ref[...], preferred_element_type=F32).astype(BF16)
    n_hp = 2 * PEER_HEADS
    for hp in range(n_hp):
        sc_s[hp] = lax.dot_general(sk_ref[hp], q[:, hp * D_HALF:(hp + 1) * D_HALF],
                                   (((1,), (1,)), ((), ())), preferred_element_type=F32)

    n_rows = N_KEYS // SUBLANES
    sub_key = lax.broadcasted_iota(jnp.int32, (SUBLANES, LANES), 0).astype(F32)

    def stage1(hp, carry):
        for lg in range(tm // LANES):
            l0 = lg * LANES
            val = [sc_s[hp, v * SUBLANES:(v + 1) * SUBLANES, l0:l0 + LANES] for v in range(n_rows)]
            key = [sub_key + float(v * SUBLANES) for v in range(n_rows)]
            for a, b in SORT16_NETWORK:
                swap = (val[b] > val[a]) | ((val[b] == val[a]) & (key[b] < key[a]))
                val[a], val[b] = jnp.where(swap, val[b], val[a]), jnp.where(swap, val[a], val[b])
                key[a], key[b] = jnp.where(swap, key[b], key[a]), jnp.where(swap, key[a], key[b])
            for k in range(PEER_TOPK):
                m = jnp.max(val[0], axis=0, keepdims=True)
                i = jnp.min(jnp.where(val[0] == m, key[0], float(N_KEYS)), axis=0, keepdims=True)
                ts_s[hp, k:k + 1, l0:l0 + LANES] = m
                ti_s[hp, k:k + 1, l0:l0 + LANES] = i
                if k + 1 < PEER_TOPK:
                    taken = key[0] == i
                    for p in range(PEER_TOPK - 1 - k):
                        val[p] = jnp.where(taken, val[p + 1], val[p])
                        key[p] = jnp.where(taken, key[p + 1], key[p])
        return carry

    lax.fori_loop(0, n_hp, stage1, 0)

    n_cand = float(PEER_TOPK * PEER_TOPK)
    level_flat = [sub_key * float(PEER_TOPK) + float(b) for b in range(PEER_TOPK)]
    high_flat = (sub_key + float(SUBLANES)) * float(PEER_TOPK)

    def stage2(hd, carry):
        for lg in range(tm // LANES):
            l0 = lg * LANES
            s0 = ts_s[2 * hd, :, l0:l0 + LANES]
            s1 = ts_s[2 * hd + 1, :, l0:l0 + LANES]
            i0 = ti_s[2 * hd, :, l0:l0 + LANES]
            i1 = ti_s[2 * hd + 1, :, l0:l0 + LANES]
            val = []
            for b in range(PEER_TOPK):
                v = s0[0:SUBLANES, :] + s1[b:b + 1, :]
                if LEVEL_COUNT[b] < SUBLANES:
                    v = jnp.where(sub_key < float(LEVEL_COUNT[b]), v, -jnp.inf)
                val.append(v)
            flat = list(level_flat)
            high = s0[SUBLANES:, :] + s1[0:1, :]
            for k in range(PEER_TOPK):
                m = jnp.max(jnp.maximum(val[0], high), axis=0, keepdims=True)
                c = jnp.min(jnp.minimum(jnp.where(val[0] == m, flat[0], n_cand),
                                        jnp.where(high == m, high_flat, n_cand)),
                            axis=0, keepdims=True)
                best_s[k:k + 1, :] = m
                sel_s[k:k + 1, :] = c
                if k + 1 < PEER_TOPK:
                    taken = flat[0] == c
                    high = jnp.where(high_flat == c, -jnp.inf, high)
                    for p in range(PEER_TOPK - 1 - k):
                        val[p] = jnp.where(taken, val[p + 1], val[p])
                        flat[p] = jnp.where(taken, flat[p + 1], flat[p])
            best = best_s[...]
            sel = sel_s[...]
            sel_a = jnp.floor(sel * (1.0 / PEER_TOPK))
            sel_b = sel - sel_a * PEER_TOPK
            e0 = jnp.zeros_like(sel)
            e1 = jnp.zeros_like(sel)
            for a in range(PEER_TOPK):
                e0 = jnp.where(sel_a == float(a), i0[a:a + 1, :], e0)
                e1 = jnp.where(sel_b == float(a), i1[a:a + 1, :], e1)
            ex = jnp.exp(best - best[0:1, :])
            gate = ex / jnp.sum(ex, axis=0, keepdims=True)
            r0 = pl.multiple_of(hd * PEER_TOPK, PEER_TOPK)
            e_s[pl.ds(r0, PEER_TOPK), l0:l0 + LANES] = e0 * float(N_KEYS) + e1
            gt_s[pl.ds(r0, PEER_TOPK), l0:l0 + LANES] = gate
        return carry

    lax.fori_loop(0, PEER_HEADS, stage2, 0)

    ids = jnp.clip(e_s[...], 0.0, float(N_EXPERTS - 1))
    idx_ref[...] = jnp.transpose(ids).astype(jnp.int32)
    gate_ref[...] = jnp.transpose(gt_s[...])


def _route(h, g, wq, sk, tm):
    t = h.shape[0]
    row = lambda i: (i, 0)
    const2 = lambda i: (0, 0)
    n_hp = 2 * PEER_HEADS
    return pl.pallas_call(
        functools.partial(_route_kernel, tm=tm),
        grid=(t // tm,),
        in_specs=[
            pl.BlockSpec((tm, D_MODEL), row),
            pl.BlockSpec((1, D_MODEL), const2),
            pl.BlockSpec((D_MODEL, n_hp * D_HALF), const2),
            pl.BlockSpec((n_hp, N_KEYS, D_HALF), lambda i: (0, 0, 0)),
        ],
        out_specs=[pl.BlockSpec((tm, N_PICKS), row), pl.BlockSpec((tm, N_PICKS), row)],
        out_shape=[jax.ShapeDtypeStruct((t, N_PICKS), jnp.int32),
                   jax.ShapeDtypeStruct((t, N_PICKS), F32)],
        scratch_shapes=[
            pltpu.VMEM((n_hp, N_KEYS, tm), F32),
            pltpu.VMEM((n_hp, PEER_TOPK, tm), F32),
            pltpu.VMEM((n_hp, PEER_TOPK, tm), F32),
            pltpu.VMEM((PEER_TOPK, LANES), F32),
            pltpu.VMEM((PEER_TOPK, LANES), F32),
            pltpu.VMEM((N_PICKS, tm), F32),
            pltpu.VMEM((N_PICKS, tm), F32),
        ],
        compiler_params=_cparams(1),
        name="peer_route",
    )(h, g, wq, sk)


def _pack_kernel(u_ref, v_ref, o_ref):
    n = u_ref.shape[0]
    half = D_MODEL // 2
    for src, row0 in ((u_ref, 0), (v_ref, WORD_ROWS)):
        bits = pltpu.bitcast(src[...].astype(BF16).astype(F32), jnp.uint32)
        for t in range(WORD_ROWS):
            lo = bits[:, t * LANES:(t + 1) * LANES] >> 16
            hi = bits[:, half + t * LANES:half + (t + 1) * LANES] & jnp.uint32(0xFFFF0000)
            o_ref[pl.ds(row0 + t, n, stride=ROWS_PER_EXPERT), :] = pltpu.bitcast(lo | hi, jnp.int32)


def _pack_experts(u, v, layer):
    n_blk = 256
    row = lambda i: (i, 0)
    layer_row = lambda i: (layer, i, 0)
    out = pl.pallas_call(
        _pack_kernel,
        grid=(N_EXPERTS // n_blk,),
        in_specs=[pl.BlockSpec((None, n_blk, D_MODEL), layer_row),
                  pl.BlockSpec((None, n_blk, D_MODEL), layer_row)],
        out_specs=pl.BlockSpec((n_blk * ROWS_PER_EXPERT, LANES), row),
        out_shape=jax.ShapeDtypeStruct((N_EXPERTS * ROWS_PER_EXPERT, LANES), jnp.int32),
        compiler_params=_cparams(1),
        name="pack_experts",
    )(u, v)
    return out.reshape(N_EXPERTS, ROWS_PER_EXPERT, LANES)


def _gather_kernel(idx_ref, gate_ref, h_ref, g_ref, gfin_ref, tab_ref, out_ref,
                   hf_s, buf, sem, *, tt, final_norm):
    hf_s[...] = _rms(h_ref[...], g_ref[...])
    n_sub = D_MODEL // LANES
    n_groups = tt // N_SLOTS
    n_pieces = 2 * WORD_ROWS
    picks_per_piece = N_PICKS // n_pieces

    def unpack(words):
        return (pltpu.bitcast(words << 16, F32), pltpu.bitcast(words & HIGH_HALF, F32))

    def pick_copy(e, slot, r):
        dst = buf.at[pl.ds(slot * SLOT_ROWS + r * PICK_PITCH, ROWS_PER_EXPERT), :]
        return pltpu.make_async_copy(tab_ref.at[e], dst, sem.at[slot])

    def issue(j, slot, r_lo, r_hi):
        for r in range(r_lo, r_hi):
            pick_copy(idx_ref[j, r], slot, r).start(priority=r % 2)

    def wait(slot):
        for r in range(N_PICKS):
            pick_copy(0, slot, r).wait()

    eye = (lax.broadcasted_iota(jnp.int32, (N_PICKS, N_PICKS), 0)
           == lax.broadcasted_iota(jnp.int32, (N_PICKS, N_PICKS), 1))
    sub = lax.broadcasted_iota(jnp.int32, (SUBLANES, LANES), 0)

    def token(k, x8, gate8, tiles, prefetch):
        base = k * SLOT_ROWS
        wait(k)
        piece = 0
        acc = jnp.zeros((N_PICKS, LANES), F32)
        for t in range(WORD_ROWS):
            u_lo, u_hi = unpack(buf[pl.ds(base + t, N_PICKS, stride=PICK_PITCH), :])
            acc = acc + u_lo * x8[t][k:k + 1, :] + u_hi * x8[WORD_ROWS + t][k:k + 1, :]
            prefetch(piece)
            piece += 1
        act = jax.nn.gelu(jnp.sum(acc, axis=1, keepdims=True))
        gcol = jnp.sum(jnp.where(eye, gate8[k:k + 1, :], 0.0), axis=1, keepdims=True)
        w = gcol * act
        new_tiles = list(tiles)
        for t in range(WORD_ROWS):
            v_lo, v_hi = unpack(buf[pl.ds(base + WORD_ROWS + t, N_PICKS, stride=PICK_PITCH), :])
            o_lo = jnp.sum(v_lo * w, axis=0, keepdims=True)
            o_hi = jnp.sum(v_hi * w, axis=0, keepdims=True)
            new_tiles[t] = jnp.where(sub == k, o_lo, tiles[t])
            new_tiles[WORD_ROWS + t] = jnp.where(sub == k, o_hi, tiles[WORD_ROWS + t])
            prefetch(piece)
            piece += 1
        return new_tiles

    def group(gi, last):
        r0 = pl.multiple_of(gi * N_SLOTS, N_SLOTS)
        x8 = [hf_s[pl.ds(r0, N_SLOTS), s * LANES:(s + 1) * LANES] for s in range(n_sub)]
        gate8 = gate_ref[pl.ds(r0, N_SLOTS), :]
        tiles = [jnp.zeros((SUBLANES, LANES), F32) for _ in range(n_sub)]
        for k in range(N_SLOTS):
            ahead = k + PREFETCH_DIST
            if last and ahead >= N_SLOTS:
                prefetch = lambda piece: None
            else:
                def prefetch(piece, ahead=ahead):
                    issue(r0 + ahead, ahead % N_SLOTS,
                          piece * picks_per_piece, (piece + 1) * picks_per_piece)
            tiles = token(k, x8, gate8, tiles, prefetch)
        res = [h_ref[pl.ds(r0, N_SLOTS), s * LANES:(s + 1) * LANES] + tiles[s] for s in range(n_sub)]
        if final_norm:
            ss = res[0] * res[0]
            for s in range(1, n_sub):
                ss = ss + res[s] * res[s]
            scale = lax.rsqrt(jnp.sum(ss, axis=1, keepdims=True) * (1.0 / D_MODEL) + EPS)
            res = [res[s] * scale * gfin_ref[:, s * LANES:(s + 1) * LANES] for s in range(n_sub)]
        for s in range(n_sub):
            out_ref[pl.ds(r0, N_SLOTS), s * LANES:(s + 1) * LANES] = res[s]

    for j in range(PREFETCH_DIST):
        issue(j, j, 0, N_PICKS)

    def body(gi, carry):
        group(gi, False)
        return carry

    lax.fori_loop(0, n_groups - 1, body, 0)
    group(n_groups - 1, True)


def _gather(idx, gate, h, g, gfin, tab, tt, final_norm):
    t = h.shape[0]
    row = lambda i: (i, 0)
    const2 = lambda i: (0, 0)
    return pl.pallas_call(
        functools.partial(_gather_kernel, tt=tt, final_norm=final_norm),
        grid=(t // tt,),
        in_specs=[
            pl.BlockSpec((tt, N_PICKS), row, memory_space=pltpu.SMEM),
            pl.BlockSpec((tt, N_PICKS), row),
            pl.BlockSpec((tt, D_MODEL), row),
            pl.BlockSpec((1, D_MODEL), const2),
            pl.BlockSpec((1, D_MODEL), const2),
            pl.BlockSpec(memory_space=pl.ANY),
        ],
        out_specs=pl.BlockSpec((tt, D_MODEL), row),
        out_shape=jax.ShapeDtypeStruct((t, D_MODEL), F32),
        scratch_shapes=[
            pltpu.VMEM((tt, D_MODEL), F32),
            pltpu.VMEM((N_SLOTS * SLOT_ROWS, LANES), jnp.int32),
            pltpu.SemaphoreType.DMA((N_SLOTS,)),
        ],
        compiler_params=_cparams(1),
        name="peer_experts",
    )(idx, gate, h, g, gfin, tab)


def _proj_kernel(h_ref, gkv_ref, gq_ref, wkv_ref, wq_ref, kv_ref, q_ref):
    h = h_ref[...]
    n = h * lax.rsqrt(jnp.mean(h * h, axis=-1, keepdims=True) + EPS)
    kv_ref[...] = jnp.dot((n * gkv_ref[...]).astype(BF16), wkv_ref[...], preferred_element_type=F32)
    q_ref[...] = jnp.dot((n * gq_ref[...]).astype(BF16), wq_ref[...],
                         preferred_element_type=F32).astype(BF16)


def _proj(h, gkv, gq, wkv, wq, tm):
    t = h.shape[0]
    row = lambda i: (i, 0)
    const2 = lambda i: (0, 0)
    return pl.pallas_call(
        _proj_kernel,
        grid=(t // tm,),
        in_specs=[
            pl.BlockSpec((tm, D_MODEL), row),
            pl.BlockSpec((1, D_MODEL), const2),
            pl.BlockSpec((1, D_MODEL), const2),
            pl.BlockSpec((D_MODEL, KV_DIM), const2),
            pl.BlockSpec((D_MODEL, D_MODEL), const2),
        ],
        out_specs=[pl.BlockSpec((tm, KV_DIM), row), pl.BlockSpec((tm, D_MODEL), row)],
        out_shape=[jax.ShapeDtypeStruct((t, KV_DIM), F32),
                   jax.ShapeDtypeStruct((t, D_MODEL), BF16)],
        compiler_params=_cparams(1),
        name="kv_q_proj",
    )(h, gkv, gq, wkv, wq)


def _alibi_slopes():
    return [2.0 ** (-8.0 * (h + 1) / N_HEADS) for h in range(N_HEADS)]


ATTN_KEYS = WINDOW + CHUNK
ATTN_PAIR = 2 * CHUNK
ATTN_ROW_BLOCK = 32


def _padded_heads(tile, g):
    low = lax.broadcasted_iota(jnp.int32, tile.shape, 1) < HEAD_DIM
    rolled = pltpu.roll(tile, HEAD_DIM, axis=1)
    if g % 2 == 0:
        pads = [jnp.where(low, tile, 0.0), jnp.where(low, 0.0, rolled)]
    else:
        pads = [jnp.where(low, rolled, 0.0), jnp.where(low, 0.0, tile)]
    return [p.astype(BF16) for p in pads]


def _attn_chunks(q_ref, windows, first_valids, sink_ref, s_scr, p_scr):
    half_kv = KV_DIM // 2
    slopes = _alibi_slopes()
    n_chunks = len(windows)

    def chain(c, g, half):
        return (c * N_KV_HEADS + g) * 2 + half

    for c in range(n_chunks):
        win = windows[c]()
        r0 = c * CHUNK
        for g in range(N_KV_HEADS):
            t = g // 2
            k_pad = _padded_heads(win[:, t * LANES:(t + 1) * LANES], g)
            qstack = jnp.concatenate(
                [q_ref[r0:r0 + CHUNK, (2 * g) * LANES:(2 * g + 1) * LANES],
                 q_ref[r0:r0 + CHUNK, (2 * g + 1) * LANES:(2 * g + 2) * LANES]], axis=0)
            for half in range(2):
                s_scr[chain(c, g, half)] = lax.dot_general(
                    qstack, k_pad[half], (((1,), (1,)), ((), ())), preferred_element_type=F32)

    rb = ATTN_ROW_BLOCK
    qi = lax.broadcasted_iota(jnp.int32, (rb, ATTN_KEYS), 0)
    kj = lax.broadcasted_iota(jnp.int32, (rb, ATTN_KEYS), 1)
    dists = [jnp.abs(WINDOW + qi + off - kj).astype(F32) for off in range(0, CHUNK, rb)]
    for c in range(n_chunks):
        valid = kj >= first_valids[c]
        for g in range(N_KV_HEADS):
            for half in range(2):
                ch = chain(c, g, half)
                for b in range(ATTN_PAIR // rb):
                    row0 = b * rb
                    head = GQA_GROUP * g + half + (2 if row0 >= CHUNK else 0)
                    s = s_scr[ch, row0:row0 + rb, :]
                    s = s * (HEAD_DIM ** -0.5) - slopes[head] * dists[(row0 % CHUNK) // rb]
                    s = jnp.where(valid, s, NEG)
                    sink = sink_ref[0, head]
                    m = jnp.maximum(jnp.max(s, axis=1, keepdims=True), sink)
                    e = jnp.exp(s - m)
                    denom = jnp.sum(e, axis=1, keepdims=True) + jnp.exp(sink - m)
                    p_scr[ch, row0:row0 + rb, :] = (e / denom).astype(BF16)

    outs = []
    for c in range(n_chunks):
        win = windows[c]()
        o_tiles = [None] * (N_HEADS // 2)
        for g in range(N_KV_HEADS):
            t = g // 2
            v_pad = _padded_heads(win[:, half_kv + t * LANES:half_kv + (t + 1) * LANES], g)
            o_stack = (jnp.dot(p_scr[chain(c, g, 0)], v_pad[0], preferred_element_type=F32)
                       + jnp.dot(p_scr[chain(c, g, 1)], v_pad[1], preferred_element_type=F32))
            o_tiles[2 * g] = o_stack[:CHUNK]
            o_tiles[2 * g + 1] = o_stack[CHUNK:]
        outs.append(jnp.concatenate(o_tiles, axis=1))
    return outs[0] if n_chunks == 1 else jnp.concatenate(outs, axis=0)


def _attn_scratch(n_chunks):
    n_chains = n_chunks * N_KV_HEADS * 2
    return [pltpu.VMEM((n_chains, ATTN_PAIR, ATTN_KEYS), F32),
            pltpu.VMEM((n_chains, ATTN_PAIR, ATTN_KEYS), BF16)]


def _attn_prompt_kernel(sink_ref, q_ref, kvp_ref, kvc_ref, h_ref, wo_ref, out_ref,
                        kv_s, s_scr, p_scr, *, qt):
    t = pl.program_id(1)
    kv_s[0:WINDOW, :] = kvp_ref[...]
    kv_s[WINDOW:WINDOW + qt, :] = kvc_ref[...]
    n_chunks = qt // CHUNK
    windows = [functools.partial(lambda r0: kv_s[r0:r0 + ATTN_KEYS, :], c * CHUNK)
               for c in range(n_chunks)]
    first_valids = [WINDOW - CHUNK * (t * n_chunks + c) for c in range(n_chunks)]
    o = _attn_chunks(q_ref, windows, first_valids, sink_ref, s_scr, p_scr).astype(BF16)
    out_ref[...] = h_ref[...] + jnp.dot(o, wo_ref[...], preferred_element_type=F32)


def _attn_prompt(sinks, q, kv, h, wo, batch, seq, qt):
    t = h.shape[0]
    tiles = seq // qt
    cur = lambda b, i: (b * tiles + i, 0)
    prev = lambda b, i: (jnp.maximum((b * tiles + i) * (qt // WINDOW) - 1, 0), 0)
    const2 = lambda b, i: (0, 0)
    return pl.pallas_call(
        functools.partial(_attn_prompt_kernel, qt=qt),
        grid=(batch, tiles),
        in_specs=[
            pl.BlockSpec((1, N_HEADS), const2, memory_space=pltpu.SMEM),
            pl.BlockSpec((qt, D_MODEL), cur),
            pl.BlockSpec((WINDOW, KV_DIM), prev),
            pl.BlockSpec((qt, KV_DIM), cur),
            pl.BlockSpec((qt, D_MODEL), cur),
            pl.BlockSpec((D_MODEL, D_MODEL), const2),
        ],
        out_specs=pl.BlockSpec((qt, D_MODEL), cur),
        out_shape=jax.ShapeDtypeStruct((t, D_MODEL), F32),
        scratch_shapes=[pltpu.VMEM((WINDOW + qt, KV_DIM), F32)] + _attn_scratch(qt // CHUNK),
        compiler_params=_cparams(2),
        name="attn_prompt",
    )(sinks, q, kv, kv, h, wo)


def _attn_sample_kernel(sink_ref, q_ref, ck_ref, cv_ref, kvn_ref, h_ref, wo_ref, out_ref,
                        s_scr, p_scr, *, nb):
    half_kv = KV_DIM // 2

    def window(b):
        kvn = kvn_ref[b * CHUNK:(b + 1) * CHUNK, :]
        k = jnp.concatenate([ck_ref[b * WINDOW:(b + 1) * WINDOW, :], kvn[:, :half_kv]], axis=0)
        v = jnp.concatenate([cv_ref[b * WINDOW:(b + 1) * WINDOW, :], kvn[:, half_kv:]], axis=0)
        return jnp.concatenate([k, v], axis=1)

    windows = [functools.partial(window, b) for b in range(nb)]
    o = _attn_chunks(q_ref, windows, [0] * nb, sink_ref, s_scr, p_scr).astype(BF16)
    out_ref[...] = h_ref[...] + jnp.dot(o, wo_ref[...], preferred_element_type=F32)


def _attn_sample(sinks, q, ck, cv, kv, h, wo, batch, nb):
    t = h.shape[0]
    row = lambda b: (b, 0)
    const2 = lambda b: (0, 0)
    return pl.pallas_call(
        functools.partial(_attn_sample_kernel, nb=nb),
        grid=(batch // nb,),
        in_specs=[
            pl.BlockSpec((1, N_HEADS), const2, memory_space=pltpu.SMEM),
            pl.BlockSpec((nb * CHUNK, D_MODEL), row),
            pl.BlockSpec((nb * WINDOW, KV_DIM // 2), row),
            pl.BlockSpec((nb * WINDOW, KV_DIM // 2), row),
            pl.BlockSpec((nb * CHUNK, KV_DIM), row),
            pl.BlockSpec((nb * CHUNK, D_MODEL), row),
            pl.BlockSpec((D_MODEL, D_MODEL), const2),
        ],
        scratch_shapes=_attn_scratch(nb),
        out_specs=pl.BlockSpec((nb * CHUNK, D_MODEL), row),
        out_shape=jax.ShapeDtypeStruct((t, D_MODEL), F32),
        compiler_params=_cparams(1),
        name="attn_sample",
    )(sinks, q, ck, cv, kv, h, wo)


def kernel(x_prompt, x_sample, cache_k_win, cache_v_win, norm_mix, norm_ffn, gmlp_w_in, gmlp_norm_v, gmlp_w_s, gmlp_b_s, gmlp_w_out, norm_kv, w_kv, attn_w_q, attn_sinks, attn_w_o, peer_w_q, peer_sub_keys, peer_u, peer_v, norm_final):
    batch, seq, _ = x_prompt.shape
    dec_batch, dec_seq, _ = x_sample.shape
    w_cache = cache_k_win.shape[1]
    assert dec_seq == CHUNK and w_cache == WINDOW and seq % MLP_CHUNK == 0

    row2 = lambda a: a.reshape(1, -1)
    win = gmlp_w_in[0].astype(BF16)
    wout = gmlp_w_out[0].astype(BF16)
    wkv = w_kv.astype(BF16)
    wq_attn = attn_w_q[0].astype(BF16)
    wo = attn_w_o[0].astype(BF16)
    wq_peer = peer_w_q.astype(BF16)
    sub_keys = peer_sub_keys.reshape(2, 2 * PEER_HEADS, N_KEYS, D_HALF).astype(BF16)
    tables = [_pack_experts(peer_u, peer_v, l) for l in range(2)]

    pos = jnp.arange(MLP_CHUNK)
    blk_i = pos[:, None] // CHUNK
    blk_j = pos[None, :] // CHUNK
    ws = gmlp_w_s[0]
    bs = gmlp_b_s[0]
    wmix_p = jnp.where((blk_j <= blk_i)[None], ws, 0.0).astype(BF16)
    ws64 = ws[:, :CHUNK, :CHUNK]
    wmix_s = jnp.where((blk_j == blk_i)[None], jnp.tile(ws64, (1, 2, 2)), 0.0).astype(BF16)
    bias_p = jnp.repeat(bs.T, LANES, axis=1)
    bias_s = jnp.repeat(jnp.tile(bs[:, :CHUNK], (1, 2)).T, LANES, axis=1)

    def trunk(x, prompt):
        t = x.shape[0]
        tm = 512
        tm_route = 1024
        tt = 256
        h, v_rows = _gmlp(x, row2(norm_mix[0]), win, row2(gmlp_norm_v[0]),
                          wmix_p if prompt else wmix_s, bias_p if prompt else bias_s, wout, tm)
        idx, gate = _route(h, row2(norm_ffn[0]), wq_peer[0], sub_keys[0], tm_route)
        h = _gather(idx, gate, h, row2(norm_ffn[0]), row2(norm_final), tables[0], tt, False)
        kv, q = _proj(h, row2(norm_kv), row2(norm_mix[1]), wkv, wq_attn, tm)
        if prompt:
            h = _attn_prompt(attn_sinks, q, kv, h, wo, batch, seq, 256)
        else:
            ck = cache_k_win.reshape(dec_batch * w_cache, KV_DIM // 2)
            cv = cache_v_win.reshape(dec_batch * w_cache, KV_DIM // 2)
            h = _attn_sample(attn_sinks, q, ck, cv, kv, h, wo, dec_batch, 4)
        idx, gate = _route(h, row2(norm_ffn[1]), wq_peer[1], sub_keys[1], tm_route)
        y = _gather(idx, gate, h, row2(norm_ffn[1]), row2(norm_final), tables[1], tt, True)
        return y, kv, v_rows

    y_p, kv_p, _ = trunk(x_prompt.reshape(batch * seq, D_MODEL), True)
    y_s, kv_s, v_s = trunk(x_sample.reshape(dec_batch * dec_seq, D_MODEL), False)

    half = KV_DIM // 2
    kv_p = kv_p.reshape(batch, seq, KV_DIM)[:, seq - WINDOW:]
    prompt_k_win = kv_p[..., :half].reshape(batch, WINDOW, N_KV_HEADS, HEAD_DIM)
    prompt_v_win = kv_p[..., half:].reshape(batch, WINDOW, N_KV_HEADS, HEAD_DIM)
    kv_s = kv_s.reshape(dec_batch, dec_seq, KV_DIM)
    k_s = kv_s[..., :half].reshape(dec_batch, dec_seq, N_KV_HEADS, HEAD_DIM)
    v_s_new = kv_s[..., half:].reshape(dec_batch, dec_seq, N_KV_HEADS, HEAD_DIM)
    sample_k_win = jnp.concatenate([cache_k_win, k_s], axis=1)[:, -w_cache:]
    sample_v_win = jnp.concatenate([cache_v_win, v_s_new], axis=1)[:, -w_cache:]
    sample_gmlp_v = v_s.reshape(1, dec_batch, dec_seq, D_GATE)

    return (y_p.reshape(batch, seq, D_MODEL), y_s.reshape(dec_batch, dec_seq, D_MODEL),
            prompt_k_win, prompt_v_win, sample_k_win, sample_v_win, sample_gmlp_v)
```

```python
import functools
import math

import jax
import jax.numpy as jnp
from jax import lax
from jax.experimental import pallas as pl
from jax.experimental.pallas import tpu as pltpu

F32 = jnp.float32
BF16 = jnp.bfloat16

D_MODEL = 1024
CHUNK = 64
MLP_CHUNK = 128
GMLP_GROUPS = 8
D_GATE = D_MODEL
N_HEADS = 16
N_KV_HEADS = 4
HEAD_DIM = 64
GQA_GROUP = N_HEADS // N_KV_HEADS
WINDOW = 128
PEER_HEADS = 8
N_KEYS = 128
N_EXPERTS = N_KEYS * N_KEYS
PEER_TOPK = 16
D_HALF = 128
N_PICKS = PEER_HEADS * PEER_TOPK
EPS = 1e-6
NEG = -1e30
KV_DIM = 2 * N_KV_HEADS * HEAD_DIM

LANES = 128
SUBLANES = 8
WORD_ROWS = D_MODEL // 2 // LANES
ROWS_PER_EXPERT = 2 * WORD_ROWS
HIGH_HALF = -65536
PICK_PITCH = 12
SLOT_ROWS = N_PICKS * PICK_PITCH
N_SLOTS = SUBLANES
PREFETCH_DIST = 7

VMEM_LIMIT = 48 * 1024 * 1024


def _rms(x, g):
    return x * lax.rsqrt(jnp.mean(x * x, axis=-1, keepdims=True) + EPS) * g


def _cparams(n_axes):
    return pltpu.CompilerParams(
        dimension_semantics=("arbitrary",) * n_axes, vmem_limit_bytes=VMEM_LIMIT)


def _gmlp_kernel(x_ref, gmix_ref, win_ref, gv_ref, wmix_ref, bias_ref, wout_ref,
                 h_ref, v_ref, *, tm):
    x = x_ref[...]
    hn = _rms(x, gmix_ref[...])
    hmid = jax.nn.gelu(jnp.dot(hn.astype(BF16), win_ref[...], preferred_element_type=F32))
    u = hmid[:, :D_GATE]
    v = _rms(hmid[:, D_GATE:], gv_ref[...])
    v_ref[...] = v
    vb = v.astype(BF16)
    bias = bias_ref[...]
    outs = []
    for c in range(tm // MLP_CHUNK):
        r0 = c * MLP_CHUNK
        cols = []
        for g in range(GMLP_GROUPS):
            c0 = g * LANES
            cols.append(jnp.dot(wmix_ref[g], vb[r0:r0 + MLP_CHUNK, c0:c0 + LANES],
                                preferred_element_type=F32))
        mixed = jnp.concatenate(cols, axis=1) + bias
        outs.append(u[r0:r0 + MLP_CHUNK, :] * mixed)
    out = outs[0] if len(outs) == 1 else jnp.concatenate(outs, axis=0)
    y = jnp.dot(out.astype(BF16), wout_ref[...], preferred_element_type=F32)
    h_ref[...] = x + y


def _gmlp(x, gmix, win, gv, wmix, bias, wout, tm):
    t = x.shape[0]
    row = lambda i: (i, 0)
    const2 = lambda i: (0, 0)
    return pl.pallas_call(
        functools.partial(_gmlp_kernel, tm=tm),
        grid=(t // tm,),
        in_specs=[
            pl.BlockSpec((tm, D_MODEL), row),
            pl.BlockSpec((1, D_MODEL), const2),
            pl.BlockSpec((D_MODEL, 2 * D_GATE), const2),
            pl.BlockSpec((1, D_GATE), const2),
            pl.BlockSpec((GMLP_GROUPS, MLP_CHUNK, MLP_CHUNK), lambda i: (0, 0, 0)),
            pl.BlockSpec((MLP_CHUNK, D_GATE), const2),
            pl.BlockSpec((D_GATE, D_MODEL), const2),
        ],
        out_specs=[pl.BlockSpec((tm, D_MODEL), row), pl.BlockSpec((tm, D_GATE), row)],
        out_shape=[jax.ShapeDtypeStruct((t, D_MODEL), F32),
                   jax.ShapeDtypeStruct((t, D_GATE), F32)],
        compiler_params=_cparams(1),
        name="gmlp_mixer",
    )(x, gmix, win, gv, wmix, bias, wout)


def _merge_exchange_network(lo, hi, r=None):
    def merge(lo, hi, r):
        step = r * 2
        if step < hi - lo:
            yield from merge(lo, hi, step)
            yield from merge(lo + r, hi, step)
            yield from [(i, i + r) for i in range(lo + r, hi - r, step)]
        else:
            yield (lo, lo + r)

    if hi - lo >= 1:
        mid = lo + (hi - lo) // 2
        yield from _merge_exchange_network(lo, mid)
        yield from _merge_exchange_network(mid + 1, hi)
        yield from merge(lo, hi, 1)


SORT16_NETWORK = list(_merge_exchange_network(0, N_KEYS // SUBLANES - 1))

LEVEL_COUNT = [PEER_TOPK // (b + 1) for b in range(PEER_TOPK)]


def _route_kernel(h_ref, g_ref, wq_ref, sk_ref, idx_ref, gate_ref,
                  sc_s, ts_s, ti_s, best_s, sel_s, e_s, gt_s, *, tm):
    hf = _rms(h_ref[...], g_ref[...])
    q = jnp.dot(hf.astype(BF16), wq_ref[...], preferred_element_type=F32).astype(BF16)
    n_hp = 2 * PEER_HEADS
    for hp in range(n_hp):
        sc_s[hp] = lax.dot_general(sk_ref[hp], q[:, hp * D_HALF:(hp + 1) * D_HALF],
                                   (((1,), (1,)), ((), ())), preferred_element_type=F32)

    n_rows = N_KEYS // SUBLANES
    sub_key = lax.broadcasted_iota(jnp.int32, (SUBLANES, LANES), 0).astype(F32)

    def stage1(hp, carry):
        for lg in range(tm // LANES):
            l0 = lg * LANES
            val = [sc_s[hp, v * SUBLANES:(v + 1) * SUBLANES, l0:l0 + LANES] for v in range(n_rows)]
            key = [sub_key + float(v * SUBLANES) for v in range(n_rows)]
            for a, b in SORT16_NETWORK:
                swap = (val[b] > val[a]) | ((val[b] == val[a]) & (key[b] < key[a]))
                val[a], val[b] = jnp.where(swap, val[b], val[a]), jnp.where(swap, val[a], val[b])
                key[a], key[b] = jnp.where(swap, key[b], key[a]), jnp.where(swap, key[a], key[b])
            for k in range(PEER_TOPK):
                m = jnp.max(val[0], axis=0, keepdims=True)
                i = jnp.min(jnp.where(val[0] == m, key[0], float(N_KEYS)), axis=0, keepdims=True)
                ts_s[hp, k:k + 1, l0:l0 + LANES] = m
                ti_s[hp, k:k + 1, l0:l0 + LANES] = i
                if k + 1 < PEER_TOPK:
                    taken = key[0] == i
                    for p in range(PEER_TOPK - 1 - k):
                        val[p] = jnp.where(taken, val[p + 1], val[p])
                        key[p] = jnp.where(taken, key[p + 1], key[p])
        return carry

    lax.fori_loop(0, n_hp, stage1, 0)

    n_cand = float(PEER_TOPK * PEER_TOPK)
    level_flat = [sub_key * float(PEER_TOPK) + float(b) for b in range(PEER_TOPK)]
    high_flat = (sub_key + float(SUBLANES)) * float(PEER_TOPK)

    def stage2(hd, carry):
        for lg in range(tm // LANES):
            l0 = lg * LANES
            s0 = ts_s[2 * hd, :, l0:l0 + LANES]
            s1 = ts_s[2 * hd + 1, :, l0:l0 + LANES]
            i0 = ti_s[2 * hd, :, l0:l0 + LANES]
            i1 = ti_s[2 * hd + 1, :, l0:l0 + LANES]
            val = []
            for b in range(PEER_TOPK):
                v = s0[0:SUBLANES, :] + s1[b:b + 1, :]
                if LEVEL_COUNT[b] < SUBLANES:
                    v = jnp.where(sub_key < float(LEVEL_COUNT[b]), v, -jnp.inf)
                val.append(v)
            flat = list(level_flat)
            high = s0[SUBLANES:, :] + s1[0:1, :]
            for k in range(PEER_TOPK):
                m = jnp.max(jnp.maximum(val[0], high), axis=0, keepdims=True)
                c = jnp.min(jnp.minimum(jnp.where(val[0] == m, flat[0], n_cand),
                                        jnp.where(high == m, high_flat, n_cand)),
                            axis=0, keepdims=True)
                best_s[k:k + 1, :] = m
                sel_s[k:k + 1, :] = c
                if k + 1 < PEER_TOPK:
                    taken = flat[0] == c
                    high = jnp.where(high_flat == c, -jnp.inf, high)
                    for p in range(PEER_TOPK - 1 - k):
                        val[p] = jnp.where(taken, val[p + 1], val[p])
                        flat[p] = jnp.where(taken, flat[p + 1], flat[p])
            best = best_s[...]
            sel = sel_s[...]
            sel_a = jnp.floor(sel * (1.0 / PEER_TOPK))
            sel_b = sel - sel_a * PEER_TOPK
            e0 = jnp.zeros_like(sel)
            e1 = jnp.zeros_like(sel)
            for a in range(PEER_TOPK):
                e0 = jnp.where(sel_a == float(a), i0[a:a + 1, :], e0)
                e1 = jnp.where(sel_b == float(a), i1[a:a + 1, :], e1)
            ex = jnp.exp(best - best[0:1, :])
            gate = ex / jnp.sum(ex, axis=0, keepdims=True)
            r0 = pl.multiple_of(hd * PEER_TOPK, PEER_TOPK)
            e_s[pl.ds(r0, PEER_TOPK), l0:l0 + LANES] = e0 * float(N_KEYS) + e1
            gt_s[pl.ds(r0, PEER_TOPK), l0:l0 + LANES] = gate
        return carry

    lax.fori_loop(0, PEER_HEADS, stage2, 0)

    ids = jnp.clip(e_s[...], 0.0, float(N_EXPERTS - 1))
    idx_ref[...] = jnp.transpose(ids).astype(jnp.int32)
    gate_ref[...] = jnp.transpose(gt_s[...])


def _route(h, g, wq, sk, tm):
    t = h.shape[0]
    row = lambda i: (i, 0)
    const2 = lambda i: (0, 0)
    n_hp = 2 * PEER_HEADS
    return pl.pallas_call(
        functools.partial(_route_kernel, tm=tm),
        grid=(t // tm,),
        in_specs=[
            pl.BlockSpec((tm, D_MODEL), row),
            pl.BlockSpec((1, D_MODEL), const2),
            pl.BlockSpec((D_MODEL, n_hp * D_HALF), const2),
            pl.BlockSpec((n_hp, N_KEYS, D_HALF), lambda i: (0, 0, 0)),
        ],
        out_specs=[pl.BlockSpec((tm, N_PICKS), row), pl.BlockSpec((tm, N_PICKS), row)],
        out_shape=[jax.ShapeDtypeStruct((t, N_PICKS), jnp.int32),
                   jax.ShapeDtypeStruct((t, N_PICKS), F32)],
        scratch_shapes=[
            pltpu.VMEM((n_hp, N_KEYS, tm), F32),
            pltpu.VMEM((n_hp, PEER_TOPK, tm), F32),
            pltpu.VMEM((n_hp, PEER_TOPK, tm), F32),
            pltpu.VMEM((PEER_TOPK, LANES), F32),
            pltpu.VMEM((PEER_TOPK, LANES), F32),
            pltpu.VMEM((N_PICKS, tm), F32),
            pltpu.VMEM((N_PICKS, tm), F32),
        ],
        compiler_params=_cparams(1),
        name="peer_route",
    )(h, g, wq, sk)


def _pack_kernel(u_ref, v_ref, o_ref):
    n = u_ref.shape[0]
    half = D_MODEL // 2
    for src, row0 in ((u_ref, 0), (v_ref, WORD_ROWS)):
        bits = pltpu.bitcast(src[...].astype(BF16).astype(F32), jnp.uint32)
        for t in range(WORD_ROWS):
            lo = bits[:, t * LANES:(t + 1) * LANES] >> 16
            hi = bits[:, half + t * LANES:half + (t + 1) * LANES] & jnp.uint32(0xFFFF0000)
            o_ref[pl.ds(row0 + t, n, stride=ROWS_PER_EXPERT), :] = pltpu.bitcast(lo | hi, jnp.int32)


def _pack_experts(u, v, layer):
    n_blk = 256
    row = lambda i: (i, 0)
    layer_row = lambda i: (layer, i, 0)
    out = pl.pallas_call(
        _pack_kernel,
        grid=(N_EXPERTS // n_blk,),
        in_specs=[pl.BlockSpec((None, n_blk, D_MODEL), layer_row),
                  pl.BlockSpec((None, n_blk, D_MODEL), layer_row)],
        out_specs=pl.BlockSpec((n_blk * ROWS_PER_EXPERT, LANES), row),
        out_shape=jax.ShapeDtypeStruct((N_EXPERTS * ROWS_PER_EXPERT, LANES), jnp.int32),
        compiler_params=_cparams(1),
        name="pack_experts",
    )(u, v)
    return out.reshape(N_EXPERTS, ROWS_PER_EXPERT, LANES)


def _gather_kernel(idx_ref, gate_ref, h_ref, g_ref, gfin_ref, tab_ref, out_ref,
                   hf_s, buf, sem, *, tt, final_norm):
    hf_s[...] = _rms(h_ref[...], g_ref[...])
    n_sub = D_MODEL // LANES
    n_groups = tt // N_SLOTS
    n_pieces = 2 * WORD_ROWS
    picks_per_piece = N_PICKS // n_pieces

    def unpack(words):
        return (pltpu.bitcast(words << 16, F32), pltpu.bitcast(words & HIGH_HALF, F32))

    def pick_copy(e, slot, r):
        dst = buf.at[pl.ds(slot * SLOT_ROWS + r * PICK_PITCH, ROWS_PER_EXPERT), :]
        return pltpu.make_async_copy(tab_ref.at[e], dst, sem.at[slot])

    def issue(j, slot, r_lo, r_hi):
        for r in range(r_lo, r_hi):
            pick_copy(idx_ref[j, r], slot, r).start(priority=r % 2)

    def wait(slot):
        for r in range(N_PICKS):
            pick_copy(0, slot, r).wait()

    eye = (lax.broadcasted_iota(jnp.int32, (N_PICKS, N_PICKS), 0)
           == lax.broadcasted_iota(jnp.int32, (N_PICKS, N_PICKS), 1))
    sub = lax.broadcasted_iota(jnp.int32, (SUBLANES, LANES), 0)

    def token(k, x8, gate8, tiles, prefetch):
        base = k * SLOT_ROWS
        wait(k)
        piece = 0
        acc = jnp.zeros((N_PICKS, LANES), F32)
        for t in range(WORD_ROWS):
            u_lo, u_hi = unpack(buf[pl.ds(base + t, N_PICKS, stride=PICK_PITCH), :])
            acc = acc + u_lo * x8[t][k:k + 1, :] + u_hi * x8[WORD_ROWS + t][k:k + 1, :]
            prefetch(piece)
            piece += 1
        act = jax.nn.gelu(jnp.sum(acc, axis=1, keepdims=True))
        gcol = jnp.sum(jnp.where(eye, gate8[k:k + 1, :], 0.0), axis=1, keepdims=True)
        w = gcol * act
        new_tiles = list(tiles)
        for t in range(WORD_ROWS):
            v_lo, v_hi = unpack(buf[pl.ds(base + WORD_ROWS + t, N_PICKS, stride=PICK_PITCH), :])
            o_lo = jnp.sum(v_lo * w, axis=0, keepdims=True)
            o_hi = jnp.sum(v_hi * w, axis=0, keepdims=True)
            new_tiles[t] = jnp.where(sub == k, o_lo, tiles[t])
            new_tiles[WORD_ROWS + t] = jnp.where(sub == k, o_hi, tiles[WORD_ROWS + t])
            prefetch(piece)
            piece += 1
        return new_tiles

    def group(gi, last):
        r0 = pl.multiple_of(gi * N_SLOTS, N_SLOTS)
        x8 = [hf_s[pl.ds(r0, N_SLOTS), s * LANES:(s + 1) * LANES] for s in range(n_sub)]
        gate8 = gate_ref[pl.ds(r0, N_SLOTS), :]
        tiles = [jnp.zeros((SUBLANES, LANES), F32) for _ in range(n_sub)]
        for k in range(N_SLOTS):
            ahead = k + PREFETCH_DIST
            if last and ahead >= N_SLOTS:
                prefetch = lambda piece: None
            else:
                def prefetch(piece, ahead=ahead):
                    issue(r0 + ahead, ahead % N_SLOTS,
                          piece * picks_per_piece, (piece + 1) * picks_per_piece)
            tiles = token(k, x8, gate8, tiles, prefetch)
        res = [h_ref[pl.ds(r0, N_SLOTS), s * LANES:(s + 1) * LANES] + tiles[s] for s in range(n_sub)]
        if final_norm:
            ss = res[0] * res[0]
            for s in range(1, n_sub):
                ss = ss + res[s] * res[s]
            scale = lax.rsqrt(jnp.sum(ss, axis=1, keepdims=True) * (1.0 / D_MODEL) + EPS)
            res = [res[s] * scale * gfin_ref[:, s * LANES:(s + 1) * LANES] for s in range(n_sub)]
        for s in range(n_sub):
            out_ref[pl.ds(r0, N_SLOTS), s * LANES:(s + 1) * LANES] = res[s]

    for j in range(PREFETCH_DIST):
        issue(j, j, 0, N_PICKS)

    def body(gi, carry):
        group(gi, False)
        return carry

    lax.fori_loop(0, n_groups - 1, body, 0)
    group(n_groups - 1, True)


def _gather(idx, gate, h, g, gfin, tab, tt, final_norm):
    t = h.shape[0]
    row = lambda i: (i, 0)
    const2 = lambda i: (0, 0)
    return pl.pallas_call(
        functools.partial(_gather_kernel, tt=tt, final_norm=final_norm),
        grid=(t // tt,),
        in_specs=[
            pl.BlockSpec((tt, N_PICKS), row, memory_space=pltpu.SMEM),
            pl.BlockSpec((tt, N_PICKS), row),
            pl.BlockSpec((tt, D_MODEL), row),
            pl.BlockSpec((1, D_MODEL), const2),
            pl.BlockSpec((1, D_MODEL), const2),
            pl.BlockSpec(memory_space=pl.ANY),
        ],
        out_specs=pl.BlockSpec((tt, D_MODEL), row),
        out_shape=jax.ShapeDtypeStruct((t, D_MODEL), F32),
        scratch_shapes=[
            pltpu.VMEM((tt, D_MODEL), F32),
            pltpu.VMEM((N_SLOTS * SLOT_ROWS, LANES), jnp.int32),
            pltpu.SemaphoreType.DMA((N_SLOTS,)),
        ],
        compiler_params=_cparams(1),
        name="peer_experts",
    )(idx, gate, h, g, gfin, tab)


def _proj_kernel(h_ref, gkv_ref, gq_ref, wkv_ref, wq_ref, kv_ref, q_ref):
    h = h_ref[...]
    n = h * lax.rsqrt(jnp.mean(h * h, axis=-1, keepdims=True) + EPS)
    kv_ref[...] = jnp.dot((n * gkv_ref[...]).astype(BF16), wkv_ref[...], preferred_element_type=F32)
    q_ref[...] = jnp.dot((n * gq_ref[...]).astype(BF16), wq_ref[...],
                         preferred_element_type=F32).astype(BF16)


def _proj(h, gkv, gq, wkv, wq, tm):
    t = h.shape[0]
    row = lambda i: (i, 0)
    const2 = lambda i: (0, 0)
    return pl.pallas_call(
        _proj_kernel,
        grid=(t // tm,),
        in_specs=[
            pl.BlockSpec((tm, D_MODEL), row),
            pl.BlockSpec((1, D_MODEL), const2),
            pl.BlockSpec((1, D_MODEL), const2),
            pl.BlockSpec((D_MODEL, KV_DIM), const2),
            pl.BlockSpec((D_MODEL, D_MODEL), const2),
        ],
        out_specs=[pl.BlockSpec((tm, KV_DIM), row), pl.BlockSpec((tm, D_MODEL), row)],
        out_shape=[jax.ShapeDtypeStruct((t, KV_DIM), F32),
                   jax.ShapeDtypeStruct((t, D_MODEL), BF16)],
        compiler_params=_cparams(1),
        name="kv_q_proj",
    )(h, gkv, gq, wkv, wq)


def _alibi_slopes():
    return [2.0 ** (-8.0 * (h + 1) / N_HEADS) for h in range(N_HEADS)]


ATTN_KEYS = WINDOW + CHUNK
ATTN_PAIR = 2 * CHUNK
ATTN_ROW_BLOCK = 32


def _padded_heads(tile, g):
    low = lax.broadcasted_iota(jnp.int32, tile.shape, 1) < HEAD_DIM
    rolled = pltpu.roll(tile, HEAD_DIM, axis=1)
    if g % 2 == 0:
        pads = [jnp.where(low, tile, 0.0), jnp.where(low, 0.0, rolled)]
    else:
        pads = [jnp.where(low, rolled, 0.0), jnp.where(low, 0.0, tile)]
    return [p.astype(BF16) for p in pads]


def _attn_chunks(q_ref, windows, first_valids, sink_ref, s_scr, p_scr):
    half_kv = KV_DIM // 2
    slopes = _alibi_slopes()
    n_chunks = len(windows)

    def chain(c, g, half):
        return (c * N_KV_HEADS + g) * 2 + half

    for c in range(n_chunks):
        win = windows[c]()
        r0 = c * CHUNK
        for g in range(N_KV_HEADS):
            t = g // 2
            k_pad = _padded_heads(win[:, t * LANES:(t + 1) * LANES], g)
            qstack = jnp.concatenate(
                [q_ref[r0:r0 + CHUNK, (2 * g) * LANES:(2 * g + 1) * LANES],
                 q_ref[r0:r0 + CHUNK, (2 * g + 1) * LANES:(2 * g + 2) * LANES]], axis=0)
            for half in range(2):
                s_scr[chain(c, g, half)] = lax.dot_general(
                    qstack, k_pad[half], (((1,), (1,)), ((), ())), preferred_element_type=F32)

    rb = ATTN_ROW_BLOCK
    qi = lax.broadcasted_iota(jnp.int32, (rb, ATTN_KEYS), 0)
    kj = lax.broadcasted_iota(jnp.int32, (rb, ATTN_KEYS), 1)
    dists = [jnp.abs(WINDOW + qi + off - kj).astype(F32) for off in range(0, CHUNK, rb)]
    for c in range(n_chunks):
        valid = kj >= first_valids[c]
        for g in range(N_KV_HEADS):
            for half in range(2):
                ch = chain(c, g, half)
                for b in range(ATTN_PAIR // rb):
                    row0 = b * rb
                    head = GQA_GROUP * g + half + (2 if row0 >= CHUNK else 0)
                    s = s_scr[ch, row0:row0 + rb, :]
                    s = s * (HEAD_DIM ** -0.5) - slopes[head] * dists[(row0 % CHUNK) // rb]
                    s = jnp.where(valid, s, NEG)
                    sink = sink_ref[0, head]
                    m = jnp.maximum(jnp.max(s, axis=1, keepdims=True), sink)
                    e = jnp.exp(s - m)
                    denom = jnp.sum(e, axis=1, keepdims=True) + jnp.exp(sink - m)
                    p_scr[ch, row0:row0 + rb, :] = (e / denom).astype(BF16)

    outs = []
    for c in range(n_chunks):
        win = windows[c]()
        o_tiles = [None] * (N_HEADS // 2)
        for g in range(N_KV_HEADS):
            t = g // 2
            v_pad = _padded_heads(win[:, half_kv + t * LANES:half_kv + (t + 1) * LANES], g)
            o_stack = (jnp.dot(p_scr[chain(c, g, 0)], v_pad[0], preferred_element_type=F32)
                       + jnp.dot(p_scr[chain(c, g, 1)], v_pad[1], preferred_element_type=F32))
            o_tiles[2 * g] = o_stack[:CHUNK]
            o_tiles[2 * g + 1] = o_stack[CHUNK:]
        outs.append(jnp.concatenate(o_tiles, axis=1))
    return outs[0] if n_chunks == 1 else jnp.concatenate(outs, axis=0)


def _attn_scratch(n_chunks):
    n_chains = n_chunks * N_KV_HEADS * 2
    return [pltpu.VMEM((n_chains, ATTN_PAIR, ATTN_KEYS), F32),
            pltpu.VMEM((n_chains, ATTN_PAIR, ATTN_KEYS), BF16)]


def _attn_prompt_kernel(sink_ref, q_ref, kvp_ref, kvc_ref, h_ref, wo_ref, out_ref,
                        kv_s, s_scr, p_scr, *, qt):
    t = pl.program_id(1)
    kv_s[0:WINDOW, :] = kvp_ref[...]
    kv_s[WINDOW:WINDOW + qt, :] = kvc_ref[...]
    n_chunks = qt // CHUNK
    windows = [functools.partial(lambda r0: kv_s[r0:r0 + ATTN_KEYS, :], c * CHUNK)
               for c in range(n_chunks)]
    first_valids = [WINDOW - CHUNK * (t * n_chunks + c) for c in range(n_chunks)]
    o = _attn_chunks(q_ref, windows, first_valids, sink_ref, s_scr, p_scr).astype(BF16)
    out_ref[...] = h_ref[...] + jnp.dot(o, wo_ref[...], preferred_element_type=F32)


def _attn_prompt(sinks, q, kv, h, wo, batch, seq, qt):
    t = h.shape[0]
    tiles = seq // qt
    cur = lambda b, i: (b * tiles + i, 0)
    prev = lambda b, i: (jnp.maximum((b * tiles + i) * (qt // WINDOW) - 1, 0), 0)
    const2 = lambda b, i: (0, 0)
    return pl.pallas_call(
        functools.partial(_attn_prompt_kernel, qt=qt),
        grid=(batch, tiles),
        in_specs=[
            pl.BlockSpec((1, N_HEADS), const2, memory_space=pltpu.SMEM),
            pl.BlockSpec((qt, D_MODEL), cur),
            pl.BlockSpec((WINDOW, KV_DIM), prev),
            pl.BlockSpec((qt, KV_DIM), cur),
            pl.BlockSpec((qt, D_MODEL), cur),
            pl.BlockSpec((D_MODEL, D_MODEL), const2),
        ],
        out_specs=pl.BlockSpec((qt, D_MODEL), cur),
        out_shape=jax.ShapeDtypeStruct((t, D_MODEL), F32),
        scratch_shapes=[pltpu.VMEM((WINDOW + qt, KV_DIM), F32)] + _attn_scratch(qt // CHUNK),
        compiler_params=_cparams(2),
        name="attn_prompt",
    )(sinks, q, kv, kv, h, wo)


def _attn_sample_kernel(sink_ref, q_ref, ck_ref, cv_ref, kvn_ref, h_ref, wo_ref, out_ref,
                        s_scr, p_scr, *, nb):
    half_kv = KV_DIM // 2

    def window(b):
        kvn = kvn_ref[b * CHUNK:(b + 1) * CHUNK, :]
        k = jnp.concatenate([ck_ref[b * WINDOW:(b + 1) * WINDOW, :], kvn[:, :half_kv]], axis=0)
        v = jnp.concatenate([cv_ref[b * WINDOW:(b + 1) * WINDOW, :], kvn[:, half_kv:]], axis=0)
        return jnp.concatenate([k, v], axis=1)

    windows = [functools.partial(window, b) for b in range(nb)]
    o = _attn_chunks(q_ref, windows, [0] * nb, sink_ref, s_scr, p_scr).astype(BF16)
    out_ref[...] = h_ref[...] + jnp.dot(o, wo_ref[...], preferred_element_type=F32)


def _attn_sample(sinks, q, ck, cv, kv, h, wo, batch, nb):
    t = h.shape[0]
    row = lambda b: (b, 0)
    const2 = lambda b: (0, 0)
    return pl.pallas_call(
        functools.partial(_attn_sample_kernel, nb=nb),
        grid=(batch // nb,),
        in_specs=[
            pl.BlockSpec((1, N_HEADS), const2, memory_space=pltpu.SMEM),
            pl.BlockSpec((nb * CHUNK, D_MODEL), row),
            pl.BlockSpec((nb * WINDOW, KV_DIM // 2), row),
            pl.BlockSpec((nb * WINDOW, KV_DIM // 2), row),
            pl.BlockSpec((nb * CHUNK, KV_DIM), row),
            pl.BlockSpec((nb * CHUNK, D_MODEL), row),
            pl.BlockSpec((D_MODEL, D_MODEL), const2),
        ],
        scratch_shapes=_attn_scratch(nb),
        out_specs=pl.BlockSpec((nb * CHUNK, D_MODEL), row),
        out_shape=jax.ShapeDtypeStruct((t, D_MODEL), F32),
        compiler_params=_cparams(1),
        name="attn_sample",
    )(sinks, q, ck, cv, kv, h, wo)


def kernel(x_prompt, x_sample, cache_k_win, cache_v_win, norm_mix, norm_ffn, gmlp_w_in, gmlp_norm_v, gmlp_w_s, gmlp_b_s, gmlp_w_out, norm_kv, w_kv, attn_w_q, attn_sinks, attn_w_o, peer_w_q, peer_sub_keys, peer_u, peer_v, norm_final):
    batch, seq, _ = x_prompt.shape
    dec_batch, dec_seq, _ = x_sample.shape
    w_cache = cache_k_win.shape[1]
    assert dec_seq == CHUNK and w_cache == WINDOW and seq % MLP_CHUNK == 0

    row2 = lambda a: a.reshape(1, -1)
    win = gmlp_w_in[0].astype(BF16)
    wout = gmlp_w_out[0].astype(BF16)
    wkv = w_kv.astype(BF16)
    wq_attn = attn_w_q[0].astype(BF16)
    wo = attn_w_o[0].astype(BF16)
    wq_peer = peer_w_q.astype(BF16)
    sub_keys = peer_sub_keys.reshape(2, 2 * PEER_HEADS, N_KEYS, D_HALF).astype(BF16)
    tables = [_pack_experts(peer_u, peer_v, l) for l in range(2)]

    pos = jnp.arange(MLP_CHUNK)
    blk_i = pos[:, None] // CHUNK
    blk_j = pos[None, :] // CHUNK
    ws = gmlp_w_s[0]
    bs = gmlp_b_s[0]
    wmix_p = jnp.where((blk_j <= blk_i)[None], ws, 0.0).astype(BF16)
    ws64 = ws[:, :CHUNK, :CHUNK]
    wmix_s = jnp.where((blk_j == blk_i)[None], jnp.tile(ws64, (1, 2, 2)), 0.0).astype(BF16)
    bias_p = jnp.repeat(bs.T, LANES, axis=1)
    bias_s = jnp.repeat(jnp.tile(bs[:, :CHUNK], (1, 2)).T, LANES, axis=1)

    def trunk(x, prompt):
        t = x.shape[0]
        tm = 512
        tm_route = 1024
        tt = 512
        h, v_rows = _gmlp(x, row2(norm_mix[0]), win, row2(gmlp_norm_v[0]),
                          wmix_p if prompt else wmix_s, bias_p if prompt else bias_s, wout, tm)
        idx, gate = _route(h, row2(norm_ffn[0]), wq_peer[0], sub_keys[0], tm_route)
        h = _gather(idx, gate, h, row2(norm_ffn[0]), row2(norm_final), tables[0], tt, False)
        kv, q = _proj(h, row2(norm_kv), row2(norm_mix[1]), wkv, wq_attn, tm)
        if prompt:
            h = _attn_prompt(attn_sinks, q, kv, h, wo, batch, seq, 256)
        else:
            ck = cache_k_win.reshape(dec_batch * w_cache, KV_DIM // 2)
            cv = cache_v_win.reshape(dec_batch * w_cache, KV_DIM // 2)
            h = _attn_sample(attn_sinks, q, ck, cv, kv, h, wo, dec_batch, 4)
        idx, gate = _route(h, row2(norm_ffn[1]), wq_peer[1], sub_keys[1], tm_route)
        y = _gather(idx, gate, h, row2(norm_ffn[1]), row2(norm_final), tables[1], tt, True)
        return y, kv, v_rows

    y_p, kv_p, _ = trunk(x_prompt.reshape(batch * seq, D_MODEL), True)
    y_s, kv_s, v_s = trunk(x_sample.reshape(dec_batch * dec_seq, D_MODEL), False)

    half = KV_DIM // 2
    kv_p = kv_p.reshape(batch, seq, KV_DIM)[:, seq - WINDOW:]
    prompt_k_win = kv_p[..., :half].reshape(batch, WINDOW, N_KV_HEADS, HEAD_DIM)
    prompt_v_win = kv_p[..., half:].reshape(batch, WINDOW, N_KV_HEADS, HEAD_DIM)
    kv_s = kv_s.reshape(dec_batch, dec_seq, KV_DIM)
    k_s = kv_s[..., :half].reshape(dec_batch, dec_seq, N_KV_HEADS, HEAD_DIM)
    v_s_new = kv_s[..., half:].reshape(dec_batch, dec_seq, N_KV_HEADS, HEAD_DIM)
    sample_k_win = jnp.concatenate([cache_k_win, k_s], axis=1)[:, -w_cache:]
    sample_v_win = jnp.concatenate([cache_v_win, v_s_new], axis=1)[:, -w_cache:]
    sample_gmlp_v = v_s.reshape(1, dec_batch, dec_seq, D_GATE)

    return (y_p.reshape(batch, seq, D_MODEL), y_s.reshape(dec_batch, dec_seq, D_MODEL),
            prompt_k_win, prompt_v_win, sample_k_win, sample_v_win, sample_gmlp_v)
```

```python
import functools
import math

import jax
import jax.numpy as jnp
from jax import lax
from jax.experimental import pallas as pl
from jax.experimental.pallas import tpu as pltpu

F32 = jnp.float32
BF16 = jnp.bfloat16

D_MODEL = 1024
CHUNK = 64
MLP_CHUNK = 128
GMLP_GROUPS = 8
D_GATE = D_MODEL
N_HEADS = 16
N_KV_HEADS = 4
HEAD_DIM = 64
GQA_GROUP = N_HEADS // N_KV_HEADS
WINDOW = 128
PEER_HEADS = 8
N_KEYS = 128
N_EXPERTS = N_KEYS * N_KEYS
PEER_TOPK = 16
D_HALF = 128
N_PICKS = PEER_HEADS * PEER_TOPK
EPS = 1e-6
NEG = -1e30
KV_DIM = 2 * N_KV_HEADS * HEAD_DIM

LANES = 128
SUBLANES = 8
WORD_ROWS = D_MODEL // 2 // LANES
ROWS_PER_EXPERT = 2 * WORD_ROWS
HIGH_HALF = -65536
PICK_PITCH = 12
SLOT_ROWS = N_PICKS * PICK_PITCH
N_SLOTS = SUBLANES
PREFETCH_DIST = 7

VMEM_LIMIT = 48 * 1024 * 1024


def _rms(x, g):
    return x * lax.rsqrt(jnp.mean(x * x, axis=-1, keepdims=True) + EPS) * g


def _cparams(n_axes):
    return pltpu.CompilerParams(
        dimension_semantics=("arbitrary",) * n_axes, vmem_limit_bytes=VMEM_LIMIT)


def _gmlp_kernel(x_ref, gmix_ref, win_ref, gv_ref, wmix_ref, bias_ref, wout_ref,
                 h_ref, v_ref, *, tm):
    x = x_ref[...]
    hn = _rms(x, gmix_ref[...])
    hmid = jax.nn.gelu(jnp.dot(hn.astype(BF16), win_ref[...], preferred_element_type=F32))
    u = hmid[:, :D_GATE]
    v = _rms(hmid[:, D_GATE:], gv_ref[...])
    v_ref[...] = v
    vb = v.astype(BF16)
    bias = bias_ref[...]
    outs = []
    for c in range(tm // MLP_CHUNK):
        r0 = c * MLP_CHUNK
        cols = []
        for g in range(GMLP_GROUPS):
            c0 = g * LANES
            cols.append(jnp.dot(wmix_ref[g], vb[r0:r0 + MLP_CHUNK, c0:c0 + LANES],
                                preferred_element_type=F32))
        mixed = jnp.concatenate(cols, axis=1) + bias
        outs.append(u[r0:r0 + MLP_CHUNK, :] * mixed)
    out = outs[0] if len(outs) == 1 else jnp.concatenate(outs, axis=0)
    y = jnp.dot(out.astype(BF16), wout_ref[...], preferred_element_type=F32)
    h_ref[...] = x + y


def _gmlp(x, gmix, win, gv, wmix, bias, wout, tm):
    t = x.shape[0]
    row = lambda i: (i, 0)
    const2 = lambda i: (0, 0)
    return pl.pallas_call(
        functools.partial(_gmlp_kernel, tm=tm),
        grid=(t // tm,),
        in_specs=[
            pl.BlockSpec((tm, D_MODEL), row),
            pl.BlockSpec((1, D_MODEL), const2),
            pl.BlockSpec((D_MODEL, 2 * D_GATE), const2),
            pl.BlockSpec((1, D_GATE), const2),
            pl.BlockSpec((GMLP_GROUPS, MLP_CHUNK, MLP_CHUNK), lambda i: (0, 0, 0)),
            pl.BlockSpec((MLP_CHUNK, D_GATE), const2),
            pl.BlockSpec((D_GATE, D_MODEL), const2),
        ],
        out_specs=[pl.BlockSpec((tm, D_MODEL), row), pl.BlockSpec((tm, D_GATE), row)],
        out_shape=[jax.ShapeDtypeStruct((t, D_MODEL), F32),
                   jax.ShapeDtypeStruct((t, D_GATE), F32)],
        compiler_params=_cparams(1),
        name="gmlp_mixer",
    )(x, gmix, win, gv, wmix, bias, wout)


def _merge_exchange_network(lo, hi, r=None):
    def merge(lo, hi, r):
        step = r * 2
        if step < hi - lo:
            yield from merge(lo, hi, step)
            yield from merge(lo + r, hi, step)
            yield from [(i, i + r) for i in range(lo + r, hi - r, step)]
        else:
            yield (lo, lo + r)

    if hi - lo >= 1:
        mid = lo + (hi - lo) // 2
        yield from _merge_exchange_network(lo, mid)
        yield from _merge_exchange_network(mid + 1, hi)
        yield from merge(lo, hi, 1)


SORT16_NETWORK = list(_merge_exchange_network(0, N_KEYS // SUBLANES - 1))

LEVEL_COUNT = [PEER_TOPK // (b + 1) for b in range(PEER_TOPK)]


def _route_kernel(h_ref, g_ref, wq_ref, sk_ref, idx_ref, gate_ref,
                  sc_s, ts_s, ti_s, best_s, sel_s, e_s, gt_s, *, tm):
    hf = _rms(h_ref[...], g_ref[...])
    q = jnp.dot(hf.astype(BF16), wq_ref[...], preferred_element_type=F32).astype(BF16)
    n_hp = 2 * PEER_HEADS
    for hp in range(n_hp):
        sc_s[hp] = lax.dot_general(sk_ref[hp], q[:, hp * D_HALF:(hp + 1) * D_HALF],
                                   (((1,), (1,)), ((), ())), preferred_element_type=F32)

    n_rows = N_KEYS // SUBLANES
    sub_key = lax.broadcasted_iota(jnp.int32, (SUBLANES, LANES), 0).astype(F32)

    def stage1(hp, carry):
        for lg in range(tm // LANES):
            l0 = lg * LANES
            val = [sc_s[hp, v * SUBLANES:(v + 1) * SUBLANES, l0:l0 + LANES] for v in range(n_rows)]
            key = [sub_key + float(v * SUBLANES) for v in range(n_rows)]
            for a, b in SORT16_NETWORK:
                swap = (val[b] > val[a]) | ((val[b] == val[a]) & (key[b] < key[a]))
                val[a], val[b] = jnp.where(swap, val[b], val[a]), jnp.where(swap, val[a], val[b])
                key[a], key[b] = jnp.where(swap, key[b], key[a]), jnp.where(swap, key[a], key[b])
            for k in range(PEER_TOPK):
                m = jnp.max(val[0], axis=0, keepdims=True)
                i = jnp.min(jnp.where(val[0] == m, key[0], float(N_KEYS)), axis=0, keepdims=True)
                ts_s[hp, k:k + 1, l0:l0 + LANES] = m
                ti_s[hp, k:k + 1, l0:l0 + LANES] = i
                if k + 1 < PEER_TOPK:
                    taken = key[0] == i
                    for p in range(PEER_TOPK - 1 - k):
                        val[p] = jnp.where(taken, val[p + 1], val[p])
                        key[p] = jnp.where(taken, key[p + 1], key[p])
        return carry

    lax.fori_loop(0, n_hp, stage1, 0)

    n_cand = float(PEER_TOPK * PEER_TOPK)
    level_flat = [sub_key * float(PEER_TOPK) + float(b) for b in range(PEER_TOPK)]
    high_flat = (sub_key + float(SUBLANES)) * float(PEER_TOPK)

    def stage2(hd, carry):
        for lg in range(tm // LANES):
            l0 = lg * LANES
            s0 = ts_s[2 * hd, :, l0:l0 + LANES]
            s1 = ts_s[2 * hd + 1, :, l0:l0 + LANES]
            i0 = ti_s[2 * hd, :, l0:l0 + LANES]
            i1 = ti_s[2 * hd + 1, :, l0:l0 + LANES]
            val = []
            for b in range(PEER_TOPK):
                v = s0[0:SUBLANES, :] + s1[b:b + 1, :]
                if LEVEL_COUNT[b] < SUBLANES:
                    v = jnp.where(sub_key < float(LEVEL_COUNT[b]), v, -jnp.inf)
                val.append(v)
            flat = list(level_flat)
            high = s0[SUBLANES:, :] + s1[0:1, :]
            for k in range(PEER_TOPK):
                m = jnp.max(jnp.maximum(val[0], high), axis=0, keepdims=True)
                c = jnp.min(jnp.minimum(jnp.where(val[0] == m, flat[0], n_cand),
                                        jnp.where(high == m, high_flat, n_cand)),
                            axis=0, keepdims=True)
                best_s[k:k + 1, :] = m
                sel_s[k:k + 1, :] = c
                if k + 1 < PEER_TOPK:
                    taken = flat[0] == c
                    high = jnp.where(high_flat == c, -jnp.inf, high)
                    for p in range(PEER_TOPK - 1 - k):
                        val[p] = jnp.where(taken, val[p + 1], val[p])
                        flat[p] = jnp.where(taken, flat[p + 1], flat[p])
            best = best_s[...]
            sel = sel_s[...]
            sel_a = jnp.floor(sel * (1.0 / PEER_TOPK))
            sel_b = sel - sel_a * PEER_TOPK
            e0 = jnp.zeros_like(sel)
            e1 = jnp.zeros_like(sel)
            for a in range(PEER_TOPK):
                e0 = jnp.where(sel_a == float(a), i0[a:a + 1, :], e0)
                e1 = jnp.where(sel_b == float(a), i1[a:a + 1, :], e1)
            ex = jnp.exp(best - best[0:1, :])
            gate = ex / jnp.sum(ex, axis=0, keepdims=True)
            r0 = pl.multiple_of(hd * PEER_TOPK, PEER_TOPK)
            e_s[pl.ds(r0, PEER_TOPK), l0:l0 + LANES] = e0 * float(N_KEYS) + e1
            gt_s[pl.ds(r0, PEER_TOPK), l0:l0 + LANES] = gate
        return carry

    lax.fori_loop(0, PEER_HEADS, stage2, 0)

    ids = jnp.clip(e_s[...], 0.0, float(N_EXPERTS - 1))
    idx_ref[...] = jnp.transpose(ids).astype(jnp.int32)
    gate_ref[...] = jnp.transpose(gt_s[...])


def _route(h, g, wq, sk, tm):
    t = h.shape[0]
    row = lambda i: (i, 0)
    const2 = lambda i: (0, 0)
    n_hp = 2 * PEER_HEADS
    return pl.pallas_call(
        functools.partial(_route_kernel, tm=tm),
        grid=(t // tm,),
        in_specs=[
            pl.BlockSpec((tm, D_MODEL), row),
            pl.BlockSpec((1, D_MODEL), const2),
            pl.BlockSpec((D_MODEL, n_hp * D_HALF), const2),
            pl.BlockSpec((n_hp, N_KEYS, D_HALF), lambda i: (0, 0, 0)),
        ],
        out_specs=[pl.BlockSpec((tm, N_PICKS), row), pl.BlockSpec((tm, N_PICKS), row)],
        out_shape=[jax.ShapeDtypeStruct((t, N_PICKS), jnp.int32),
                   jax.ShapeDtypeStruct((t, N_PICKS), F32)],
        scratch_shapes=[
            pltpu.VMEM((n_hp, N_KEYS, tm), F32),
            pltpu.VMEM((n_hp, PEER_TOPK, tm), F32),
            pltpu.VMEM((n_hp, PEER_TOPK, tm), F32),
            pltpu.VMEM((PEER_TOPK, LANES), F32),
            pltpu.VMEM((PEER_TOPK, LANES), F32),
            pltpu.VMEM((N_PICKS, tm), F32),
            pltpu.VMEM((N_PICKS, tm), F32),
        ],
        compiler_params=_cparams(1),
        name="peer_route",
    )(h, g, wq, sk)


def _pack_kernel(u_ref, v_ref, o_ref):
    n = u_ref.shape[0]
    half = D_MODEL // 2
    for src, row0 in ((u_ref, 0), (v_ref, WORD_ROWS)):
        bits = pltpu.bitcast(src[...].astype(BF16).astype(F32), jnp.uint32)
        for t in range(WORD_ROWS):
            lo = bits[:, t * LANES:(t + 1) * LANES] >> 16
            hi = bits[:, half + t * LANES:half + (t + 1) * LANES] & jnp.uint32(0xFFFF0000)
            o_ref[pl.ds(row0 + t, n, stride=ROWS_PER_EXPERT), :] = pltpu.bitcast(lo | hi, jnp.int32)


def _pack_experts(u, v, layer):
    n_blk = 256
    row = lambda i: (i, 0)
    layer_row = lambda i: (layer, i, 0)
    out = pl.pallas_call(
        _pack_kernel,
        grid=(N_EXPERTS // n_blk,),
        in_specs=[pl.BlockSpec((None, n_blk, D_MODEL), layer_row),
                  pl.BlockSpec((None, n_blk, D_MODEL), layer_row)],
        out_specs=pl.BlockSpec((n_blk * ROWS_PER_EXPERT, LANES), row),
        out_shape=jax.ShapeDtypeStruct((N_EXPERTS * ROWS_PER_EXPERT, LANES), jnp.int32),
        compiler_params=_cparams(1),
        name="pack_experts",
    )(u, v)
    return out.reshape(N_EXPERTS, ROWS_PER_EXPERT, LANES)


def _gather_kernel(idx_ref, gate_ref, h_ref, g_ref, gfin_ref, tab_ref, out_ref,
                   hf_s, buf, sem, *, tt, final_norm):
    hf_s[...] = _rms(h_ref[...], g_ref[...])
    n_sub = D_MODEL // LANES
    n_groups = tt // N_SLOTS
    n_pieces = 2 * WORD_ROWS
    picks_per_piece = N_PICKS // n_pieces

    def unpack(words):
        return (pltpu.bitcast(words << 16, F32), pltpu.bitcast(words & HIGH_HALF, F32))

    def pick_copy(e, slot, r):
        dst = buf.at[pl.ds(slot * SLOT_ROWS + r * PICK_PITCH, ROWS_PER_EXPERT), :]
        return pltpu.make_async_copy(tab_ref.at[e], dst, sem.at[slot])

    def issue(j, slot, r_lo, r_hi):
        for r in range(r_lo, r_hi):
            pick_copy(idx_ref[j, r], slot, r).start(priority=r % 2)

    def wait(slot):
        for r in range(N_PICKS):
            pick_copy(0, slot, r).wait()

    eye = (lax.broadcasted_iota(jnp.int32, (N_PICKS, N_PICKS), 0)
           == lax.broadcasted_iota(jnp.int32, (N_PICKS, N_PICKS), 1))
    sub = lax.broadcasted_iota(jnp.int32, (SUBLANES, LANES), 0)

    def token(k, x8, gate8, tiles, prefetch):
        base = k * SLOT_ROWS
        wait(k)
        piece = 0
        acc = jnp.zeros((N_PICKS, LANES), F32)
        for t in range(WORD_ROWS):
            u_lo, u_hi = unpack(buf[pl.ds(base + t, N_PICKS, stride=PICK_PITCH), :])
            acc = acc + u_lo * x8[t][k:k + 1, :] + u_hi * x8[WORD_ROWS + t][k:k + 1, :]
            prefetch(piece)
            piece += 1
        act = jax.nn.gelu(jnp.sum(acc, axis=1, keepdims=True))
        gcol = jnp.sum(jnp.where(eye, gate8[k:k + 1, :], 0.0), axis=1, keepdims=True)
        w = gcol * act
        new_tiles = list(tiles)
        for t in range(WORD_ROWS):
            v_lo, v_hi = unpack(buf[pl.ds(base + WORD_ROWS + t, N_PICKS, stride=PICK_PITCH), :])
            o_lo = jnp.sum(v_lo * w, axis=0, keepdims=True)
            o_hi = jnp.sum(v_hi * w, axis=0, keepdims=True)
            new_tiles[t] = jnp.where(sub == k, o_lo, tiles[t])
            new_tiles[WORD_ROWS + t] = jnp.where(sub == k, o_hi, tiles[WORD_ROWS + t])
            prefetch(piece)
            piece += 1
        return new_tiles

    def group(gi, last):
        r0 = pl.multiple_of(gi * N_SLOTS, N_SLOTS)
        x8 = [hf_s[pl.ds(r0, N_SLOTS), s * LANES:(s + 1) * LANES] for s in range(n_sub)]
        gate8 = gate_ref[pl.ds(r0, N_SLOTS), :]
        tiles = [jnp.zeros((SUBLANES, LANES), F32) for _ in range(n_sub)]
        for k in range(N_SLOTS):
            ahead = k + PREFETCH_DIST
            if last and ahead >= N_SLOTS:
                prefetch = lambda piece: None
            else:
                def prefetch(piece, ahead=ahead):
                    issue(r0 + ahead, ahead % N_SLOTS,
                          piece * picks_per_piece, (piece + 1) * picks_per_piece)
            tiles = token(k, x8, gate8, tiles, prefetch)
        res = [h_ref[pl.ds(r0, N_SLOTS), s * LANES:(s + 1) * LANES] + tiles[s] for s in range(n_sub)]
        if final_norm:
            ss = res[0] * res[0]
            for s in range(1, n_sub):
                ss = ss + res[s] * res[s]
            scale = lax.rsqrt(jnp.sum(ss, axis=1, keepdims=True) * (1.0 / D_MODEL) + EPS)
            res = [res[s] * scale * gfin_ref[:, s * LANES:(s + 1) * LANES] for s in range(n_sub)]
        for s in range(n_sub):
            out_ref[pl.ds(r0, N_SLOTS), s * LANES:(s + 1) * LANES] = res[s]

    for j in range(PREFETCH_DIST):
        issue(j, j, 0, N_PICKS)

    def body(gi, carry):
        group(gi, False)
        return carry

    lax.fori_loop(0, n_groups - 1, body, 0)
    group(n_groups - 1, True)


def _gather(idx, gate, h, g, gfin, tab, tt, final_norm):
    t = h.shape[0]
    row = lambda i: (i, 0)
    const2 = lambda i: (0, 0)
    return pl.pallas_call(
        functools.partial(_gather_kernel, tt=tt, final_norm=final_norm),
        grid=(t // tt,),
        in_specs=[
            pl.BlockSpec((tt, N_PICKS), row, memory_space=pltpu.SMEM),
            pl.BlockSpec((tt, N_PICKS), row),
            pl.BlockSpec((tt, D_MODEL), row),
            pl.BlockSpec((1, D_MODEL), const2),
            pl.BlockSpec((1, D_MODEL), const2),
            pl.BlockSpec(memory_space=pl.ANY),
        ],
        out_specs=pl.BlockSpec((tt, D_MODEL), row),
        out_shape=jax.ShapeDtypeStruct((t, D_MODEL), F32),
        scratch_shapes=[
            pltpu.VMEM((tt, D_MODEL), F32),
            pltpu.VMEM((N_SLOTS * SLOT_ROWS, LANES), jnp.int32),
            pltpu.SemaphoreType.DMA((N_SLOTS,)),
        ],
        compiler_params=_cparams(1),
        name="peer_experts",
    )(idx, gate, h, g, gfin, tab)


def _gather_proj_kernel(idx_ref, gate_ref, h_ref, g_ref, gfin_ref, tab_ref,
                        gkv_ref, gq_ref, wkv_ref, wq_ref, out_ref, kv_ref, q_ref,
                        hf_s, buf, sem, *, tt):
    _gather_kernel(idx_ref, gate_ref, h_ref, g_ref, gfin_ref, tab_ref, out_ref,
                   hf_s, buf, sem, tt=tt, final_norm=False)
    _proj_kernel(out_ref, gkv_ref, gq_ref, wkv_ref, wq_ref, kv_ref, q_ref)


def _gather_proj(idx, gate, h, g, gfin, tab, gkv, gq, wkv, wq, tt):
    t = h.shape[0]
    row = lambda i: (i, 0)
    const2 = lambda i: (0, 0)
    return pl.pallas_call(
        functools.partial(_gather_proj_kernel, tt=tt),
        grid=(t // tt,),
        in_specs=[
            pl.BlockSpec((tt, N_PICKS), row, memory_space=pltpu.SMEM),
            pl.BlockSpec((tt, N_PICKS), row),
            pl.BlockSpec((tt, D_MODEL), row),
            pl.BlockSpec((1, D_MODEL), const2),
            pl.BlockSpec((1, D_MODEL), const2),
            pl.BlockSpec(memory_space=pl.ANY),
            pl.BlockSpec((1, D_MODEL), const2),
            pl.BlockSpec((1, D_MODEL), const2),
            pl.BlockSpec((D_MODEL, KV_DIM), const2),
            pl.BlockSpec((D_MODEL, D_MODEL), const2),
        ],
        out_specs=[pl.BlockSpec((tt, D_MODEL), row), pl.BlockSpec((tt, KV_DIM), row),
                   pl.BlockSpec((tt, D_MODEL), row)],
        out_shape=[jax.ShapeDtypeStruct((t, D_MODEL), F32),
                   jax.ShapeDtypeStruct((t, KV_DIM), F32),
                   jax.ShapeDtypeStruct((t, D_MODEL), BF16)],
        scratch_shapes=[
            pltpu.VMEM((tt, D_MODEL), F32),
            pltpu.VMEM((N_SLOTS * SLOT_ROWS, LANES), jnp.int32),
            pltpu.SemaphoreType.DMA((N_SLOTS,)),
        ],
        compiler_params=_cparams(1),
        name="peer_experts_proj",
    )(idx, gate, h, g, gfin, tab, gkv, gq, wkv, wq)


def _proj_kernel(h_ref, gkv_ref, gq_ref, wkv_ref, wq_ref, kv_ref, q_ref):
    h = h_ref[...]
    n = h * lax.rsqrt(jnp.mean(h * h, axis=-1, keepdims=True) + EPS)
    kv_ref[...] = jnp.dot((n * gkv_ref[...]).astype(BF16), wkv_ref[...], preferred_element_type=F32)
    q_ref[...] = jnp.dot((n * gq_ref[...]).astype(BF16), wq_ref[...],
                         preferred_element_type=F32).astype(BF16)


def _alibi_slopes():
    return [2.0 ** (-8.0 * (h + 1) / N_HEADS) for h in range(N_HEADS)]


ATTN_KEYS = WINDOW + CHUNK
ATTN_PAIR = 2 * CHUNK
ATTN_ROW_BLOCK = 32


def _padded_heads(tile, g):
    low = lax.broadcasted_iota(jnp.int32, tile.shape, 1) < HEAD_DIM
    rolled = pltpu.roll(tile, HEAD_DIM, axis=1)
    if g % 2 == 0:
        pads = [jnp.where(low, tile, 0.0), jnp.where(low, 0.0, rolled)]
    else:
        pads = [jnp.where(low, rolled, 0.0), jnp.where(low, 0.0, tile)]
    return [p.astype(BF16) for p in pads]


def _attn_chunks(q_ref, windows, first_valids, sink_ref, s_scr, p_scr):
    half_kv = KV_DIM // 2
    slopes = _alibi_slopes()
    n_chunks = len(windows)

    def chain(c, g, half):
        return (c * N_KV_HEADS + g) * 2 + half

    for c in range(n_chunks):
        win = windows[c]()
        r0 = c * CHUNK
        for g in range(N_KV_HEADS):
            t = g // 2
            k_pad = _padded_heads(win[:, t * LANES:(t + 1) * LANES], g)
            qstack = jnp.concatenate(
                [q_ref[r0:r0 + CHUNK, (2 * g) * LANES:(2 * g + 1) * LANES],
                 q_ref[r0:r0 + CHUNK, (2 * g + 1) * LANES:(2 * g + 2) * LANES]], axis=0)
            for half in range(2):
                s_scr[chain(c, g, half)] = lax.dot_general(
                    qstack, k_pad[half], (((1,), (1,)), ((), ())), preferred_element_type=F32)

    rb = ATTN_ROW_BLOCK
    qi = lax.broadcasted_iota(jnp.int32, (rb, ATTN_KEYS), 0)
    kj = lax.broadcasted_iota(jnp.int32, (rb, ATTN_KEYS), 1)
    dists = [jnp.abs(WINDOW + qi + off - kj).astype(F32) for off in range(0, CHUNK, rb)]
    for c in range(n_chunks):
        valid = kj >= first_valids[c]
        for g in range(N_KV_HEADS):
            for half in range(2):
                ch = chain(c, g, half)
                for b in range(ATTN_PAIR // rb):
                    row0 = b * rb
                    head = GQA_GROUP * g + half + (2 if row0 >= CHUNK else 0)
                    s = s_scr[ch, row0:row0 + rb, :]
                    s = s * (HEAD_DIM ** -0.5) - slopes[head] * dists[(row0 % CHUNK) // rb]
                    s = jnp.where(valid, s, NEG)
                    sink = sink_ref[0, head]
                    m = jnp.maximum(jnp.max(s, axis=1, keepdims=True), sink)
                    e = jnp.exp(s - m)
                    denom = jnp.sum(e, axis=1, keepdims=True) + jnp.exp(sink - m)
                    p_scr[ch, row0:row0 + rb, :] = (e / denom).astype(BF16)

    outs = []
    for c in range(n_chunks):
        win = windows[c]()
        o_tiles = [None] * (N_HEADS // 2)
        for g in range(N_KV_HEADS):
            t = g // 2
            v_pad = _padded_heads(win[:, half_kv + t * LANES:half_kv + (t + 1) * LANES], g)
            o_stack = (jnp.dot(p_scr[chain(c, g, 0)], v_pad[0], preferred_element_type=F32)
                       + jnp.dot(p_scr[chain(c, g, 1)], v_pad[1], preferred_element_type=F32))
            o_tiles[2 * g] = o_stack[:CHUNK]
            o_tiles[2 * g + 1] = o_stack[CHUNK:]
        outs.append(jnp.concatenate(o_tiles, axis=1))
    return outs[0] if n_chunks == 1 else jnp.concatenate(outs, axis=0)


def _attn_scratch(n_chunks):
    n_chains = n_chunks * N_KV_HEADS * 2
    return [pltpu.VMEM((n_chains, ATTN_PAIR, ATTN_KEYS), F32),
            pltpu.VMEM((n_chains, ATTN_PAIR, ATTN_KEYS), BF16)]


def _attn_prompt_kernel(sink_ref, q_ref, kvp_ref, kvc_ref, h_ref, wo_ref, out_ref,
                        kv_s, s_scr, p_scr, *, qt):
    t = pl.program_id(1)
    kv_s[0:WINDOW, :] = kvp_ref[...]
    kv_s[WINDOW:WINDOW + qt, :] = kvc_ref[...]
    n_chunks = qt // CHUNK
    windows = [functools.partial(lambda r0: kv_s[r0:r0 + ATTN_KEYS, :], c * CHUNK)
               for c in range(n_chunks)]
    first_valids = [WINDOW - CHUNK * (t * n_chunks + c) for c in range(n_chunks)]
    o = _attn_chunks(q_ref, windows, first_valids, sink_ref, s_scr, p_scr).astype(BF16)
    out_ref[...] = h_ref[...] + jnp.dot(o, wo_ref[...], preferred_element_type=F32)


def _attn_prompt(sinks, q, kv, h, wo, batch, seq, qt):
    t = h.shape[0]
    tiles = seq // qt
    cur = lambda b, i: (b * tiles + i, 0)
    prev = lambda b, i: (jnp.maximum((b * tiles + i) * (qt // WINDOW) - 1, 0), 0)
    const2 = lambda b, i: (0, 0)
    return pl.pallas_call(
        functools.partial(_attn_prompt_kernel, qt=qt),
        grid=(batch, tiles),
        in_specs=[
            pl.BlockSpec((1, N_HEADS), const2, memory_space=pltpu.SMEM),
            pl.BlockSpec((qt, D_MODEL), cur),
            pl.BlockSpec((WINDOW, KV_DIM), prev),
            pl.BlockSpec((qt, KV_DIM), cur),
            pl.BlockSpec((qt, D_MODEL), cur),
            pl.BlockSpec((D_MODEL, D_MODEL), const2),
        ],
        out_specs=pl.BlockSpec((qt, D_MODEL), cur),
        out_shape=jax.ShapeDtypeStruct((t, D_MODEL), F32),
        scratch_shapes=[pltpu.VMEM((WINDOW + qt, KV_DIM), F32)] + _attn_scratch(qt // CHUNK),
        compiler_params=_cparams(2),
        name="attn_prompt",
    )(sinks, q, kv, kv, h, wo)


def _attn_sample_kernel(sink_ref, q_ref, ck_ref, cv_ref, kvn_ref, h_ref, wo_ref, out_ref,
                        s_scr, p_scr, *, nb):
    half_kv = KV_DIM // 2

    def window(b):
        kvn = kvn_ref[b * CHUNK:(b + 1) * CHUNK, :]
        k = jnp.concatenate([ck_ref[b * WINDOW:(b + 1) * WINDOW, :], kvn[:, :half_kv]], axis=0)
        v = jnp.concatenate([cv_ref[b * WINDOW:(b + 1) * WINDOW, :], kvn[:, half_kv:]], axis=0)
        return jnp.concatenate([k, v], axis=1)

    windows = [functools.partial(window, b) for b in range(nb)]
    o = _attn_chunks(q_ref, windows, [0] * nb, sink_ref, s_scr, p_scr).astype(BF16)
    out_ref[...] = h_ref[...] + jnp.dot(o, wo_ref[...], preferred_element_type=F32)


def _attn_sample(sinks, q, ck, cv, kv, h, wo, batch, nb):
    t = h.shape[0]
    row = lambda b: (b, 0)
    const2 = lambda b: (0, 0)
    return pl.pallas_call(
        functools.partial(_attn_sample_kernel, nb=nb),
        grid=(batch // nb,),
        in_specs=[
            pl.BlockSpec((1, N_HEADS), const2, memory_space=pltpu.SMEM),
            pl.BlockSpec((nb * CHUNK, D_MODEL), row),
            pl.BlockSpec((nb * WINDOW, KV_DIM // 2), row),
            pl.BlockSpec((nb * WINDOW, KV_DIM // 2), row),
            pl.BlockSpec((nb * CHUNK, KV_DIM), row),
            pl.BlockSpec((nb * CHUNK, D_MODEL), row),
            pl.BlockSpec((D_MODEL, D_MODEL), const2),
        ],
        scratch_shapes=_attn_scratch(nb),
        out_specs=pl.BlockSpec((nb * CHUNK, D_MODEL), row),
        out_shape=jax.ShapeDtypeStruct((t, D_MODEL), F32),
        compiler_params=_cparams(1),
        name="attn_sample",
    )(sinks, q, ck, cv, kv, h, wo)


def kernel(x_prompt, x_sample, cache_k_win, cache_v_win, norm_mix, norm_ffn, gmlp_w_in, gmlp_norm_v, gmlp_w_s, gmlp_b_s, gmlp_w_out, norm_kv, w_kv, attn_w_q, attn_sinks, attn_w_o, peer_w_q, peer_sub_keys, peer_u, peer_v, norm_final):
    batch, seq, _ = x_prompt.shape
    dec_batch, dec_seq, _ = x_sample.shape
    w_cache = cache_k_win.shape[1]
    assert dec_seq == CHUNK and w_cache == WINDOW and seq % MLP_CHUNK == 0

    row2 = lambda a: a.reshape(1, -1)
    win = gmlp_w_in[0].astype(BF16)
    wout = gmlp_w_out[0].astype(BF16)
    wkv = w_kv.astype(BF16)
    wq_attn = attn_w_q[0].astype(BF16)
    wo = attn_w_o[0].astype(BF16)
    wq_peer = peer_w_q.astype(BF16)
    sub_keys = peer_sub_keys.reshape(2, 2 * PEER_HEADS, N_KEYS, D_HALF).astype(BF16)
    tables = [_pack_experts(peer_u, peer_v, l) for l in range(2)]

    pos = jnp.arange(MLP_CHUNK)
    blk_i = pos[:, None] // CHUNK
    blk_j = pos[None, :] // CHUNK
    ws = gmlp_w_s[0]
    bs = gmlp_b_s[0]
    wmix_p = jnp.where((blk_j <= blk_i)[None], ws, 0.0).astype(BF16)
    ws64 = ws[:, :CHUNK, :CHUNK]
    wmix_s = jnp.where((blk_j == blk_i)[None], jnp.tile(ws64, (1, 2, 2)), 0.0).astype(BF16)
    bias_p = jnp.repeat(bs.T, LANES, axis=1)
    bias_s = jnp.repeat(jnp.tile(bs[:, :CHUNK], (1, 2)).T, LANES, axis=1)

    def trunk(x, prompt):
        t = x.shape[0]
        tm = 512
        tm_route = 1024
        tt = 256
        h, v_rows = _gmlp(x, row2(norm_mix[0]), win, row2(gmlp_norm_v[0]),
                          wmix_p if prompt else wmix_s, bias_p if prompt else bias_s, wout, tm)
        idx, gate = _route(h, row2(norm_ffn[0]), wq_peer[0], sub_keys[0], tm_route)
        h, kv, q = _gather_proj(idx, gate, h, row2(norm_ffn[0]), row2(norm_final), tables[0],
                                row2(norm_kv), row2(norm_mix[1]), wkv, wq_attn, tt)
        if prompt:
            h = _attn_prompt(attn_sinks, q, kv, h, wo, batch, seq, 256)
        else:
            ck = cache_k_win.reshape(dec_batch * w_cache, KV_DIM // 2)
            cv = cache_v_win.reshape(dec_batch * w_cache, KV_DIM // 2)
            h = _attn_sample(attn_sinks, q, ck, cv, kv, h, wo, dec_batch, 4)
        idx, gate = _route(h, row2(norm_ffn[1]), wq_peer[1], sub_keys[1], tm_route)
        y = _gather(idx, gate, h, row2(norm_ffn[1]), row2(norm_final), tables[1], tt, True)
        return y, kv, v_rows

    y_p, kv_p, _ = trunk(x_prompt.reshape(batch * seq, D_MODEL), True)
    y_s, kv_s, v_s = trunk(x_sample.reshape(dec_batch * dec_seq, D_MODEL), False)

    half = KV_DIM // 2
    kv_p = kv_p.reshape(batch, seq, KV_DIM)[:, seq - WINDOW:]
    prompt_k_win = kv_p[..., :half].reshape(batch, WINDOW, N_KV_HEADS, HEAD_DIM)
    prompt_v_win = kv_p[..., half:].reshape(batch, WINDOW, N_KV_HEADS, HEAD_DIM)
    kv_s = kv_s.reshape(dec_batch, dec_seq, KV_DIM)
    k_s = kv_s[..., :half].reshape(dec_batch, dec_seq, N_KV_HEADS, HEAD_DIM)
    v_s_new = kv_s[..., half:].reshape(dec_batch, dec_seq, N_KV_HEADS, HEAD_DIM)
    sample_k_win = jnp.concatenate([cache_k_win, k_s], axis=1)[:, -w_cache:]
    sample_v_win = jnp.concatenate([cache_v_win, v_s_new], axis=1)[:, -w_cache:]
    sample_gmlp_v = v_s.reshape(1, dec_batch, dec_seq, D_GATE)

    return (y_p.reshape(batch, seq, D_MODEL), y_s.reshape(dec_batch, dec_seq, D_MODEL),
            prompt_k_win, prompt_v_win, sample_k_win, sample_v_win, sample_gmlp_v)
```
